```python
import math
import jax, jax.numpy as jnp
from jax import lax
import numpy as np

D_MODEL = 2048
BATCH = 4
SEQ = 4096
DEPTH = 2

N_META = 16
MLA_HEADS = 8
Q_LORA = 512
KV_LORA = 512
QK_NOPE = 128
QK_ROPE = 64
V_HEAD = 128
ROPE_THETA = 10000.0
Q_BLOCK = 128
ML_HEADS = 8
ML_QK = 128
ML_V = 256
CONV_K = 4
CHUNK = 64
N_EXPERTS = 64
TOP_K = 8
N_GROUPS = 8
TOPK_GROUPS = 4
D_EXPERT = 512
ROUTED_SCALE = 2.5
EXPERT_BLOCK = 128
ALPHA = (2 * DEPTH) ** 0.25
BETA = (8 * DEPTH) ** -0.25
LN_EPS = 1e-5
RMS_EPS = 1e-6
NEG = -1e30

IN_SIZES = (Q_LORA, KV_LORA, QK_ROPE,
            ML_HEADS * ML_QK, ML_HEADS * ML_QK, ML_HEADS * ML_V, ML_HEADS * ML_V,
            ML_HEADS, ML_HEADS, D_MODEL, D_MODEL)
IN_WIDTH = sum(IN_SIZES)
FG_OFFSET = sum(IN_SIZES[:8])

kernel_name = "hybrid_mla_mlstm_moe_deepnorm"


def _split(z, sizes):
    offs = tuple(int(o) for o in np.cumsum(sizes)[:-1])
    return jnp.split(z, offs, axis=-1)


def layer_norm(x, g, b):
    xf = x.astype(jnp.float32)
    mu = xf.mean(-1, keepdims=True)
    var = jnp.square(xf - mu).mean(-1, keepdims=True)
    return ((xf - mu) * lax.rsqrt(var + LN_EPS) * g + b).astype(x.dtype)


def rms_norm(x, g):
    xf = x.astype(jnp.float32)
    return (xf * lax.rsqrt(jnp.square(xf).mean(-1, keepdims=True) + RMS_EPS) * g).astype(x.dtype)


def rope(x, cos, sin):
    x1, x2 = jnp.split(x, 2, axis=-1)
    return jnp.concatenate([x1 * cos - x2 * sin, x1 * sin + x2 * cos], axis=-1).astype(x.dtype)


def mla_attention(cq, ckv, kr, q_norm_g, kv_norm_g, w_uq, w_ukv):
    B, T, _ = cq.shape
    H = MLA_HEADS
    q = (rms_norm(cq, q_norm_g) @ w_uq).reshape(B, T, H, QK_NOPE + QK_ROPE)
    qn, qr = q[..., :QK_NOPE], q[..., QK_NOPE:]
    kv = (rms_norm(ckv, kv_norm_g) @ w_ukv).reshape(B, T, H, QK_NOPE + V_HEAD)
    kn, v = kv[..., :QK_NOPE], kv[..., QK_NOPE:]
    pos = jnp.arange(T, dtype=jnp.float32)
    inv_freq = 1.0 / (ROPE_THETA ** (jnp.arange(0, QK_ROPE, 2, dtype=jnp.float32) / QK_ROPE))
    ang = pos[:, None] * inv_freq[None, :]
    cos, sin = jnp.cos(ang), jnp.sin(ang)
    qr = rope(qr, cos[None, :, None, :], sin[None, :, None, :])
    kr = rope(kr, cos[None], sin[None])
    nb = -(-T // Q_BLOCK)
    Tp = nb * Q_BLOCK
    padT = Tp - T
    pad4 = ((0, 0), (0, padT), (0, 0), (0, 0))
    qn, qr, kn, v = (jnp.pad(a, pad4) for a in (qn, qr, kn, v))
    kr = jnp.pad(kr, ((0, 0), (0, padT), (0, 0)))
    qn_b = qn.reshape(B, nb, Q_BLOCK, H, QK_NOPE).swapaxes(0, 1)
    qr_b = qr.reshape(B, nb, Q_BLOCK, H, QK_ROPE).swapaxes(0, 1)
    kpos = jnp.arange(Tp)
    scale = (QK_NOPE + QK_ROPE) ** -0.5

    def block(args):
        qn_i, qr_i, i = args
        s = (jnp.einsum('bqhd,bkhd->bhqk', qn_i, kn)
             + jnp.einsum('bqhr,bkr->bhqk', qr_i, kr)).astype(jnp.float32) * scale
        qpos = i * Q_BLOCK + jnp.arange(Q_BLOCK)
        s = jnp.where(kpos[None, :] <= qpos[:, None], s, -jnp.inf)
        p = jax.nn.softmax(s, axis=-1).astype(v.dtype)
        return jnp.einsum('bhqk,bkhd->bqhd', p, v)

    o = lax.map(block, (qn_b, qr_b, jnp.arange(nb)))
    return o.swapaxes(0, 1).reshape(B, Tp, H * V_HEAD)[:, :T]


def causal_conv(x, w, b):
    T = x.shape[1]
    xp = jnp.pad(x, ((0, 0), (CONV_K - 1, 0), (0, 0)))
    y = b
    for j in range(CONV_K):
        y = y + w[j] * xp[:, j:j + T]
    return y


def mlstm(q, k, v, ig, fg):
    B, T, _ = q.shape
    H, L = ML_HEADS, CHUNK
    padl = CHUNK - N_META
    f32 = jnp.float32
    q = q.astype(f32).reshape(B, T, H, ML_QK)
    k = k.astype(f32).reshape(B, T, H, ML_QK) * (ML_QK ** -0.5)
    v = v.astype(f32).reshape(B, T, H, ML_V)
    lf = jax.nn.log_sigmoid(fg.astype(f32))
    ig = ig.astype(f32)
    p4 = ((0, 0), (padl, 0), (0, 0), (0, 0))
    q, k, v = jnp.pad(q, p4), jnp.pad(k, p4), jnp.pad(v, p4)
    lf = jnp.pad(lf, ((0, 0), (padl, 0), (0, 0)))
    ig = jnp.pad(ig, ((0, 0), (padl, 0), (0, 0)), constant_values=NEG)
    Lm = T + padl
    nc = Lm // L

    def chunks4(a):
        return a.reshape(B, nc, L, H, a.shape[-1]).transpose(1, 0, 3, 2, 4)

    def chunks3(a):
        return a.reshape(B, nc, L, H).transpose(1, 0, 3, 2)

    causal = jnp.tril(jnp.ones((L, L), dtype=bool))

    def step(carry, inp):
        C, n, m = carry
        qc, kc, vc, lfc, igc = inp
        b = jnp.cumsum(lfc, axis=-1)
        dmat = jnp.where(causal, b[..., :, None] - b[..., None, :] + igc[..., None, :], NEG)
        inter = b + m[..., None]
        mrow = jnp.maximum(dmat.max(-1), inter)
        s = jnp.einsum('bhtd,bhsd->bhts', qc, kc) * jnp.exp(dmat - mrow[..., None])
        e_in = jnp.exp(inter - mrow)
        num = jnp.einsum('bhts,bhsv->bhtv', s, vc) + e_in[..., None] * jnp.einsum('bhtd,bhdv->bhtv', qc, C)
        den = s.sum(-1) + e_in * jnp.einsum('bhtd,bhd->bht', qc, n)
        hc = num / jnp.maximum(jnp.abs(den), jnp.exp(-mrow))[..., None]
        g = b[..., -1]
        a = g[..., None] - b + igc
        m_new = jnp.maximum(g + m, a.max(-1))
        ea = jnp.exp(a - m_new[..., None])
        decay = jnp.exp(g + m - m_new)
        C = decay[..., None, None] * C + jnp.einsum('bhs,bhsd,bhsv->bhdv', ea, kc, vc)
        n = decay[..., None] * n + jnp.einsum('bhs,bhsd->bhd', ea, kc)
        return (C, n, m_new), hc

    init = (jnp.zeros((B, H, ML_QK, ML_V), f32), jnp.zeros((B, H, ML_QK), f32), jnp.zeros((B, H), f32))
    _, hs = lax.scan(step, init, (chunks4(q), chunks4(k), chunks4(v), chunks3(lf), chunks3(ig)))
    hs = hs.transpose(1, 0, 3, 2, 4).reshape(B, Lm, H, ML_V)
    return hs[:, padl:]


def mixer(h, w_in, b_in, q_norm_g, kv_norm_g, w_uq, w_ukv, conv_w, conv_b, ml_norm_g,
          w_br_mla, w_br_mlstm, w_out):
    B, T, _ = h.shape
    z = h @ w_in + b_in
    cq, ckv, kr, mq, mk, mv, mo, mi, mf, ga, gb = _split(z, IN_SIZES)
    y_a = mla_attention(cq, ckv, kr, q_norm_g, kv_norm_g, w_uq, w_ukv) @ w_br_mla
    qk = jax.nn.silu(causal_conv(jnp.concatenate([mq, mk], -1), conv_w, conv_b))
    mq_c, mk_c = jnp.split(qk, 2, axis=-1)
    hm = mlstm(mq_c, mk_c, mv, mi, mf)
    hm = rms_norm(hm, ml_norm_g.reshape(ML_HEADS, ML_V).astype(jnp.float32))
    hm = hm.reshape(B, T, ML_HEADS * ML_V).astype(h.dtype) * jax.nn.sigmoid(mo)
    y_b = hm @ w_br_mlstm
    y = jax.nn.sigmoid(ga) * y_a + jax.nn.sigmoid(gb) * y_b
    return y @ w_out


def moe(h, w_router, e_bias, w1, w3, w2, ws1, ws3, ws2):
    B, T, D = h.shape
    xf = h.reshape(B * T, D)
    Tt = B * T
    scores = jax.nn.sigmoid(xf.astype(jnp.float32) @ w_router.astype(jnp.float32))
    biased = scores + e_bias.astype(jnp.float32)
    gs = lax.top_k(biased.reshape(Tt, N_GROUPS, N_EXPERTS // N_GROUPS), 2)[0].sum(-1)
    _, gidx = lax.top_k(gs, TOPK_GROUPS)
    gmask = jax.nn.one_hot(gidx, N_GROUPS).sum(1) > 0
    emask = jnp.repeat(gmask, N_EXPERTS // N_GROUPS, axis=1)
    _, idx = lax.top_k(jnp.where(emask, biased, -jnp.inf), TOP_K)
    wts = jnp.take_along_axis(scores, idx, axis=1)
    wts = wts / wts.sum(-1, keepdims=True) * ROUTED_SCALE
    N = Tt * TOP_K
    e_flat = idx.reshape(-1)
    tok_flat = jnp.arange(N, dtype=jnp.int32) // TOP_K
    w_flat = wts.reshape(-1)
    order = jnp.argsort(e_flat)
    e_s = e_flat[order]
    counts = jnp.bincount(e_flat, length=N_EXPERTS)
    starts = jnp.cumsum(counts) - counts
    pcounts = (counts + EXPERT_BLOCK - 1) // EXPERT_BLOCK * EXPERT_BLOCK
    pends = jnp.cumsum(pcounts)
    pstarts = pends - pcounts
    dest = pstarts[e_s] + (jnp.arange(N) - starts[e_s])
    NB = -(-N // EXPERT_BLOCK) + N_EXPERTS
    P = NB * EXPERT_BLOCK
    buf_tok = jnp.full((P,), Tt, dtype=jnp.int32).at[dest].set(tok_flat[order])
    buf_w = jnp.zeros((P,), jnp.float32).at[dest].set(w_flat[order])
    blk_e = jnp.clip(jnp.searchsorted(pends, jnp.arange(NB) * EXPERT_BLOCK, side='right'), 0, N_EXPERTS - 1)
    xpad = jnp.concatenate([xf, jnp.zeros((1, D), xf.dtype)], axis=0)

    def step(acc, blk):
        toks, ww, e = blk
        xb = xpad[toks]
        hb = jax.nn.silu(xb @ w1[e]) * (xb @ w3[e])
        y = (hb @ w2[e]).astype(jnp.float32) * ww[:, None]
        return acc.at[toks].add(y), None

    acc, _ = lax.scan(step, jnp.zeros((Tt + 1, D), jnp.float32),
                      (buf_tok.reshape(NB, EXPERT_BLOCK), buf_w.reshape(NB, EXPERT_BLOCK), blk_e))
    shared = (jax.nn.silu(xf @ ws1) * (xf @ ws3)) @ ws2
    out = acc[:Tt] + shared.astype(jnp.float32)
    return out.astype(h.dtype).reshape(B, T, D)


def setup_inputs(seed: int = 0) -> dict:
    key = jax.random.key(seed)
    ks = jax.random.split(key, 32)
    f32 = jnp.float32

    def nrm(k, shape, scale):
        return jax.random.normal(k, shape, f32) * scale

    Dd, L = D_MODEL, DEPTH
    b_in = nrm(ks[5], (L, IN_WIDTH), 0.02)
    b_in = b_in.at[:, FG_OFFSET:FG_OFFSET + ML_HEADS].add(jnp.linspace(3.0, 6.0, ML_HEADS, dtype=f32))
    return {
        "x": nrm(ks[0], (BATCH, SEQ, Dd), 1.0),
        "meta": nrm(ks[1], (N_META, Dd), 1.0),
        "ln_in_g": 1.0 + nrm(ks[2], (Dd,), 0.02),
        "ln_in_b": nrm(ks[3], (Dd,), 0.02),
        "w_in": nrm(ks[4], (L, Dd, IN_WIDTH), Dd ** -0.5),
        "b_in": b_in,
        "q_norm_g": 1.0 + nrm(ks[6], (L, Q_LORA), 0.02),
        "kv_norm_g": 1.0 + nrm(ks[7], (L, KV_LORA), 0.02),
        "w_uq": nrm(ks[8], (L, Q_LORA, MLA_HEADS * (QK_NOPE + QK_ROPE)), Q_LORA ** -0.5),
        "w_ukv": nrm(ks[9], (L, KV_LORA, MLA_HEADS * (QK_NOPE + V_HEAD)), KV_LORA ** -0.5),
        "conv_w": nrm(ks[10], (L, CONV_K, 2 * ML_HEADS * ML_QK), CONV_K ** -0.5),
        "conv_b": nrm(ks[11], (L, 2 * ML_HEADS * ML_QK), 0.02),
        "ml_norm_g": 1.0 + nrm(ks[12], (L, ML_HEADS * ML_V), 0.02),
        "w_br_mla": nrm(ks[13], (L, MLA_HEADS * V_HEAD, Dd), (MLA_HEADS * V_HEAD) ** -0.5),
        "w_br_mlstm": nrm(ks[14], (L, ML_HEADS * ML_V, Dd), (ML_HEADS * ML_V) ** -0.5),
        "w_out": nrm(ks[15], (L, Dd, Dd), BETA * Dd ** -0.5),
        "ln1_g": 1.0 + nrm(ks[16], (L, Dd), 0.02),
        "ln1_b": nrm(ks[17], (L, Dd), 0.02),
        "w_router": nrm(ks[18], (L, Dd, N_EXPERTS), Dd ** -0.5),
        "e_bias": nrm(ks[19], (L, N_EXPERTS), 0.01),
        "w1": nrm(ks[20], (L, N_EXPERTS, Dd, D_EXPERT), Dd ** -0.5),
        "w3": nrm(ks[21], (L, N_EXPERTS, Dd, D_EXPERT), Dd ** -0.5),
        "w2": nrm(ks[22], (L, N_EXPERTS, D_EXPERT, Dd), BETA * D_EXPERT ** -0.5),
        "ws1": nrm(ks[23], (L, Dd, D_EXPERT), Dd ** -0.5),
        "ws3": nrm(ks[24], (L, Dd, D_EXPERT), Dd ** -0.5),
        "ws2": nrm(ks[25], (L, D_EXPERT, Dd), BETA * D_EXPERT ** -0.5),
        "ln2_g": 1.0 + nrm(ks[26], (L, Dd), 0.02),
        "ln2_b": nrm(ks[27], (L, Dd), 0.02),
    }


def reference(x, meta, ln_in_g, ln_in_b, w_in, b_in, q_norm_g, kv_norm_g, w_uq, w_ukv,
              conv_w, conv_b, ml_norm_g, w_br_mla, w_br_mlstm, w_out, ln1_g, ln1_b,
              w_router, e_bias, w1, w3, w2, ws1, ws3, ws2, ln2_g, ln2_b):
    B = x.shape[0]
    h = jnp.concatenate([jnp.broadcast_to(meta[None].astype(x.dtype), (B, N_META, D_MODEL)), x], axis=1)
    h = layer_norm(h, ln_in_g, ln_in_b)
    for l in range(DEPTH):
        y = mixer(h, w_in[l], b_in[l], q_norm_g[l], kv_norm_g[l], w_uq[l], w_ukv[l],
                  conv_w[l], conv_b[l], ml_norm_g[l], w_br_mla[l], w_br_mlstm[l], w_out[l])
        h = layer_norm(ALPHA * h + y, ln1_g[l], ln1_b[l])
        y = moe(h, w_router[l], e_bias[l], w1[l], w3[l], w2[l], ws1[l], ws3[l], ws2[l])
        h = layer_norm(ALPHA * h + y, ln2_g[l], ln2_b[l])
    return h[:, N_META:]
```

```python
import functools

import numpy as np
import jax
import jax.numpy as jnp
from jax import lax
from jax.experimental import pallas as pl
from jax.experimental.pallas import tpu as pltpu

N_META = 16
MLA_HEADS = 8
QK_NOPE = 128
QK_ROPE = 64
V_HEAD = 128
ROPE_THETA = 10000.0
ML_HEADS = 8
ML_QK = 128
ML_V = 256
CONV_K = 4
CHUNK = 64
N_EXPERTS = 64
TOP_K = 8
N_GROUPS = 8
TOPK_GROUPS = 4
ROUTED_SCALE = 2.5
DEPTH = 2
ALPHA = (2 * DEPTH) ** 0.25
LN_EPS = 1e-5
RMS_EPS = 1e-6
NEG = -1e30

LANES = 128
HEAD_SLAB = 2 * LANES
PADL = LANES - N_META
EXPERT_ROWS = 256
VMEM_LIMIT = 56 * 1024 * 1024

F32 = jnp.float32
BF16 = jnp.bfloat16
NT = (((1,), (1,)), ((), ()))
TN = (((0,), (0,)), ((), ()))


def _pick(n, cands):
    for c in cands:
        if n % c == 0:
            return c
    raise ValueError(f"no tile in {cands} divides {n}")


def _cp(sem, vmem=None):
    return pltpu.CompilerParams(dimension_semantics=sem, vmem_limit_bytes=vmem or VMEM_LIMIT)


def _ln_rows(x, g, b):
    mu = jnp.mean(x, axis=-1, keepdims=True)
    xc = x - mu
    var = jnp.mean(xc * xc, axis=-1, keepdims=True)
    return xc * lax.rsqrt(var + LN_EPS) * g + b


def _rms_rows(x, g):
    return x * lax.rsqrt(jnp.mean(x * x, axis=-1, keepdims=True) + RMS_EPS) * g


def _silu(x):
    return x * jax.nn.sigmoid(x)


def _ln_kernel(x_ref, g_ref, b_ref, of_ref, ob_ref):
    y = _ln_rows(x_ref[...], g_ref[...], b_ref[...])
    of_ref[...] = y
    ob_ref[...] = y.astype(BF16)


def _layer_norm(x, g, b):
    n, d = x.shape
    tm = _pick(n, (512, 256, 128))
    return pl.pallas_call(
        _ln_kernel,
        grid=(n // tm,),
        in_specs=[pl.BlockSpec((tm, d), lambda i: (i, 0)),
                  pl.BlockSpec((1, d), lambda i: (0, 0)),
                  pl.BlockSpec((1, d), lambda i: (0, 0))],
        out_specs=[pl.BlockSpec((tm, d), lambda i: (i, 0)),
                   pl.BlockSpec((tm, d), lambda i: (i, 0))],
        out_shape=[jax.ShapeDtypeStruct((n, d), F32), jax.ShapeDtypeStruct((n, d), BF16)],
        compiler_params=_cp(("parallel",)),
        name="ln_in",
    )(x, g.reshape(1, d), b.reshape(1, d))


def _mm_kernel(x_ref, w_ref, b_ref, o_ref, *, sig_tile):
    acc = jnp.dot(x_ref[...], w_ref[...], preferred_element_type=F32) + b_ref[...]
    if sig_tile is None:
        o_ref[...] = acc.astype(o_ref.dtype)
        return
    j = pl.program_id(0)

    @pl.when(j < sig_tile)
    def _():
        o_ref[...] = acc.astype(o_ref.dtype)

    @pl.when(j >= sig_tile)
    def _():
        o_ref[...] = jax.nn.sigmoid(acc).astype(o_ref.dtype)


def _matmul_bias(x, w, b, out_dtype, sig_col=None, name="mm"):
    n, kd = x.shape
    nc = w.shape[1]
    tm = _pick(n, (1024, 512, 256, 128))
    tn = _pick(nc, (1024, 512, 256, 128))
    sig_tile = None
    if sig_col is not None:
        assert sig_col % tn == 0
        sig_tile = sig_col // tn
    return pl.pallas_call(
        functools.partial(_mm_kernel, sig_tile=sig_tile),
        grid=(nc // tn, n // tm),
        in_specs=[pl.BlockSpec((tm, kd), lambda j, i: (i, 0)),
                  pl.BlockSpec((kd, tn), lambda j, i: (0, j)),
                  pl.BlockSpec((1, tn), lambda j, i: (0, j))],
        out_specs=pl.BlockSpec((tm, tn), lambda j, i: (i, j)),
        out_shape=jax.ShapeDtypeStruct((n, nc), out_dtype),
        compiler_params=_cp(("parallel", "parallel")),
        name=name,
    )(x, w, b.reshape(1, nc))


def _rope_slab(r, cos_t, sin_t):
    return r * cos_t + pltpu.roll(r, 2 * (QK_ROPE // 2), 1) * sin_t


def _qproj_kernel(c_ref, g_ref, w_ref, cos_ref, sin_ref, o_ref, *, heads, scale):
    xn = _rms_rows(c_ref[...].astype(F32), g_ref[...])
    q = jnp.dot(xn.astype(BF16), w_ref[...], preferred_element_type=F32) * scale
    cos_t, sin_t = cos_ref[...], sin_ref[...]
    for h in range(heads):
        lo = h * HEAD_SLAB
        o_ref[:, lo:lo + LANES] = q[:, lo:lo + LANES].astype(BF16)
        o_ref[:, lo + LANES:lo + HEAD_SLAB] = _rope_slab(q[:, lo + LANES:lo + HEAD_SLAB], cos_t, sin_t).astype(BF16)


def _kproj_kernel(c_ref, g_ref, w_ref, kr_ref, cos_ref, sin_ref, k_ref, v_ref, *, heads):
    xn = _rms_rows(c_ref[...].astype(F32), g_ref[...])
    kv = jnp.dot(xn.astype(BF16), w_ref[...], preferred_element_type=F32)
    rr = _rope_slab(kr_ref[...], cos_ref[...], sin_ref[...]).astype(BF16)
    for h in range(heads):
        lo = h * HEAD_SLAB
        k_ref[:, lo:lo + LANES] = kv[:, h * LANES:(h + 1) * LANES].astype(BF16)
        k_ref[:, lo + LANES:lo + HEAD_SLAB] = rr
    v_ref[...] = kv[:, heads * LANES:].astype(BF16)


def _mla_qkv(z, zs, qg, kvg, wq, wkv, cos_t, sin_t, tp, ql):
    n = z.shape[0]
    heads = MLA_HEADS
    tm = _pick(tp, (384, 256, 128))
    nt = tp // tm
    scale = (QK_NOPE + QK_ROPE) ** -0.5
    tab = pl.BlockSpec((tm, LANES), lambda i: (i % nt, 0))
    q = pl.pallas_call(
        functools.partial(_qproj_kernel, heads=heads, scale=scale),
        grid=(n // tm,),
        in_specs=[pl.BlockSpec((tm, ql), lambda i: (i, 0)),
                  pl.BlockSpec((1, ql), lambda i: (0, 0)),
                  pl.BlockSpec((ql, heads * HEAD_SLAB), lambda i: (0, 0)),
                  tab, tab],
        out_specs=pl.BlockSpec((tm, heads * HEAD_SLAB), lambda i: (i, 0)),
        out_shape=jax.ShapeDtypeStruct((n, heads * HEAD_SLAB), BF16),
        compiler_params=_cp(("parallel",)),
        name="mla_qproj",
    )(z, qg.reshape(1, ql), wq, cos_t, sin_t)
    k, v = pl.pallas_call(
        functools.partial(_kproj_kernel, heads=heads),
        grid=(n // tm,),
        in_specs=[pl.BlockSpec((tm, ql), lambda i: (i, 1)),
                  pl.BlockSpec((1, ql), lambda i: (0, 0)),
                  pl.BlockSpec((ql, 2 * heads * LANES), lambda i: (0, 0)),
                  pl.BlockSpec((tm, LANES), lambda i: (i, 0)),
                  tab, tab],
        out_specs=[pl.BlockSpec((tm, heads * HEAD_SLAB), lambda i: (i, 0)),
                   pl.BlockSpec((tm, heads * LANES), lambda i: (i, 0))],
        out_shape=[jax.ShapeDtypeStruct((n, heads * HEAD_SLAB), BF16),
                   jax.ShapeDtypeStruct((n, heads * LANES), BF16)],
        compiler_params=_cp(("parallel",)),
        name="mla_kvproj",
    )(z, kvg.reshape(1, ql), wkv, zs, cos_t, sin_t)
    return q, k, v


def _attn_kernel(q_ref, k_ref, v_ref, o_ref, m_sc, l_sc, acc_sc, *, t):
    i = pl.program_id(2)
    j = pl.program_id(3)

    @pl.when(j == 0)
    def _():
        m_sc[...] = jnp.full(m_sc.shape, NEG, F32)
        l_sc[...] = jnp.zeros(l_sc.shape, F32)
        acc_sc[...] = jnp.zeros(acc_sc.shape, F32)

    def step(masked):
        s = lax.dot_general(q_ref[...], k_ref[...], NT, preferred_element_type=F32)
        if masked:
            qpos = i * t + lax.broadcasted_iota(jnp.int32, (t, t), 0)
            kpos = j * t + lax.broadcasted_iota(jnp.int32, (t, t), 1)
            s = jnp.where((kpos <= qpos) & (kpos >= PADL), s, NEG)
        m_prev = m_sc[...]
        m_new = jnp.maximum(m_prev, jnp.max(s, axis=1, keepdims=True))
        alpha = jnp.exp(m_prev - m_new)
        p = jnp.exp(s - m_new)
        l_sc[...] = alpha * l_sc[...] + jnp.sum(p, axis=1, keepdims=True)
        acc_sc[...] = alpha * acc_sc[...] + jnp.dot(p.astype(BF16), v_ref[...], preferred_element_type=F32)
        m_sc[...] = m_new

    @pl.when((j < i) & (j > 0))
    def _():
        step(False)

    @pl.when((j <= i) & ((j == i) | (j == 0)))
    def _():
        step(True)

    @pl.when(j == i)
    def _():
        o_ref[...] = (acc_sc[...] / l_sc[...]).astype(o_ref.dtype)


def _attention(q, k, v, batch, tp):
    n = q.shape[0]
    heads = MLA_HEADS
    t = _pick(tp, (384, 256, 128))
    nt = tp // t
    return pl.pallas_call(
        functools.partial(_attn_kernel, t=t),
        grid=(batch, heads, nt, nt),
        in_specs=[pl.BlockSpec((t, HEAD_SLAB), lambda b, h, i, j: (b * nt + i, h)),
                  pl.BlockSpec((t, HEAD_SLAB), lambda b, h, i, j: (b * nt + jnp.minimum(i, j), h)),
                  pl.BlockSpec((t, V_HEAD), lambda b, h, i, j: (b * nt + jnp.minimum(i, j), h))],
        out_specs=pl.BlockSpec((t, V_HEAD), lambda b, h, i, j: (b * nt + i, h)),
        out_shape=jax.ShapeDtypeStruct((n, heads * V_HEAD), BF16),
        scratch_shapes=[pltpu.VMEM((t, 1), F32), pltpu.VMEM((t, 1), F32), pltpu.VMEM((t, V_HEAD), F32)],
        compiler_params=_cp(("parallel", "parallel", "parallel", "arbitrary")),
        name="mla_attention",
    )(q, k, v)


HALO = 16


def _conv_kernel(x_ref, halo_ref, w_ref, b_ref, s_ref, o_ref, *, tm, tp):
    i = pl.program_id(0)
    start = lax.rem(i * tm, tp)
    pos = start + lax.broadcasted_iota(jnp.int32, (tm, 1), 0)
    x = jnp.where(pos >= PADL, x_ref[...].astype(F32), 0.0)
    hpos = start - HALO + lax.broadcasted_iota(jnp.int32, (HALO, 1), 0)
    halo = jnp.where(hpos >= PADL, halo_ref[...].astype(F32), 0.0)
    ext = jnp.concatenate([halo, x], axis=0)
    w = w_ref[...]
    y = b_ref[...] + w[CONV_K - 1:CONV_K, :] * x
    for j in range(1, CONV_K):
        y = y + w[CONV_K - 1 - j:CONV_K - j, :] * pltpu.roll(ext, j, 0)[HALO:, :]
    o_ref[...] = (_silu(y) * s_ref[...]).astype(BF16)


def _qk_conv(z, conv_w, conv_b, col_scale, tp, off_blk):
    n = z.shape[0]
    cw = conv_w.shape[1] // 2
    tm = _pick(tp, (384, 256, 128))
    return pl.pallas_call(
        functools.partial(_conv_kernel, tm=tm, tp=tp),
        grid=(n // tm, 2),
        in_specs=[pl.BlockSpec((tm, cw), lambda i, c: (i, off_blk + c)),
                  pl.BlockSpec((HALO, cw), lambda i, c: (jnp.maximum(i * (tm // HALO) - 1, 0), off_blk + c)),
                  pl.BlockSpec((CONV_K, cw), lambda i, c: (0, c)),
                  pl.BlockSpec((1, cw), lambda i, c: (0, c)),
                  pl.BlockSpec((1, cw), lambda i, c: (0, c))],
        out_specs=pl.BlockSpec((tm, cw), lambda i, c: (i, c)),
        out_shape=jax.ShapeDtypeStruct((n, 2 * cw), BF16),
        compiler_params=_cp(("parallel", "parallel")),
        name="mlstm_qk_conv",
    )(z, z, conv_w, conv_b.reshape(1, -1), col_scale)


def _mlstm_kernel(q_ref, k_ref, v_ref, gt_ref, og_ref, ng_ref, o_ref, c_sc, n_sc, m_sc, *, group, heads):
    h = pl.program_id(1)
    g = pl.program_id(2)
    L = CHUNK

    @pl.when(g == 0)
    def _():
        c_sc[...] = jnp.zeros(c_sc.shape, F32)
        n_sc[...] = jnp.zeros(n_sc.shape, F32)
        m_sc[...] = jnp.zeros(m_sc.shape, F32)

    lane = lax.broadcasted_iota(jnp.int32, (L, LANES), 1)
    r_i = lax.broadcasted_iota(jnp.int32, (L, L), 0)
    c_i = lax.broadcasted_iota(jnp.int32, (L, L), 1)
    eye = r_i == c_i
    causal = c_i <= r_i

    def to_row(col):
        return jnp.sum(jnp.where(eye, col, 0.0), axis=0, keepdims=True)

    def to_col(row):
        return jnp.sum(jnp.where(eye, row, 0.0), axis=1, keepdims=True)

    for c in range(group):
        rows = slice(c * L, (c + 1) * L)
        gt = gt_ref[rows, :]
        i_col = jnp.sum(jnp.where(lane == h, gt, 0.0), axis=1, keepdims=True)
        f_col = jnp.sum(jnp.where(lane == heads + h, gt, 0.0), axis=1, keepdims=True)
        pos = (g * group + c) * L + lax.broadcasted_iota(jnp.int32, (L, 1), 0)
        pad = pos < PADL
        i_col = jnp.where(pad, NEG, i_col)
        lf_col = jnp.where(pad, 0.0, -(jnp.maximum(-f_col, 0.0) + jnp.log1p(jnp.exp(-jnp.abs(f_col)))))
        b_row = jnp.sum(jnp.where(r_i <= c_i, lf_col, 0.0), axis=0, keepdims=True)
        b_col = to_col(b_row)
        i_row = to_row(i_col)
        gtot = jnp.sum(lf_col, axis=0, keepdims=True)
        m_prev = m_sc[...]
        dmat = jnp.where(causal, b_col - b_row + i_row, NEG)
        inter = b_col + m_prev
        mrow = jnp.maximum(jnp.max(dmat, axis=1, keepdims=True), inter)
        dexp = jnp.exp(dmat - mrow)
        q = q_ref[rows, :]
        k = k_ref[rows, :]
        v = v_ref[rows, :]
        s = lax.dot_general(q, k, NT, preferred_element_type=F32) * dexp
        e_in = jnp.exp(inter - mrow)
        num = (jnp.dot(s.astype(BF16), v, preferred_element_type=F32)
               + e_in * jnp.dot(q, c_sc[...].astype(BF16), preferred_element_type=F32))
        den = (jnp.sum(s, axis=1, keepdims=True)
               + e_in * jnp.sum(q.astype(F32) * n_sc[...], axis=1, keepdims=True))
        hc = num / jnp.maximum(jnp.abs(den), jnp.exp(-mrow))
        a_row = gtot - b_row + i_row
        m_new = jnp.maximum(gtot + m_prev, jnp.max(a_row, axis=1, keepdims=True))
        ea_col = to_col(jnp.exp(a_row - m_new))
        decay = jnp.exp(gtot + m_prev - m_new)
        ks = k.astype(F32) * ea_col
        c_sc[...] = decay * c_sc[...] + lax.dot_general(ks.astype(BF16), v, TN, preferred_element_type=F32)
        n_sc[...] = decay * n_sc[...] + jnp.sum(ks, axis=0, keepdims=True)
        m_sc[...] = m_new
        hn = _rms_rows(hc, ng_ref[...])
        o_ref[rows, :] = (hn * og_ref[rows, :].astype(F32)).astype(BF16)


def _mlstm(qk, z, zs, norm_g, batch, tp, v_blk, og_blk):
    n = qk.shape[0]
    heads = ML_HEADS
    nchunks = tp // CHUNK
    group = _pick(nchunks, (6, 5, 4, 3, 2, 1))
    rows = group * CHUNK
    ng = nchunks // group
    return pl.pallas_call(
        functools.partial(_mlstm_kernel, group=group, heads=heads),
        grid=(batch, heads, ng),
        in_specs=[pl.BlockSpec((rows, ML_QK), lambda b, h, g: (b * ng + g, h)),
                  pl.BlockSpec((rows, ML_QK), lambda b, h, g: (b * ng + g, heads + h)),
                  pl.BlockSpec((rows, ML_V), lambda b, h, g: (b * ng + g, v_blk + h)),
                  pl.BlockSpec((rows, LANES), lambda b, h, g: (b * ng + g, 1)),
                  pl.BlockSpec((rows, ML_V), lambda b, h, g: (b * ng + g, og_blk + h)),
                  pl.BlockSpec((1, ML_V), lambda b, h, g: (0, h))],
        out_specs=pl.BlockSpec((rows, ML_V), lambda b, h, g: (b * ng + g, h)),
        out_shape=jax.ShapeDtypeStruct((n, heads * ML_V), BF16),
        scratch_shapes=[pltpu.VMEM((ML_QK, ML_V), F32), pltpu.VMEM((1, ML_QK), F32), pltpu.VMEM((1, 1), F32)],
        compiler_params=_cp(("parallel", "parallel", "arbitrary")),
        name="mlstm_scan",
    )(qk, qk, z, zs, z, norm_g.reshape(1, -1))


def _merge_kernel(a_ref, hm_ref, wa_ref, wb_ref, ga_ref, gb_ref, o_ref):
    ya = jnp.dot(a_ref[...], wa_ref[...], preferred_element_type=F32)
    yb = jnp.dot(hm_ref[...], wb_ref[...], preferred_element_type=F32)
    o_ref[...] = (ga_ref[...].astype(F32) * ya + gb_ref[...].astype(F32) * yb).astype(BF16)


def _merge(attn, hm, wa, wb, z, ga_off, gb_off):
    n = attn.shape[0]
    d = wa.shape[1]
    tm = _pick(n, (512, 256, 128))
    tn = _pick(d, (1024, 512, 256, 128))
    assert ga_off % tn == 0 and gb_off % tn == 0
    ga_blk, gb_blk = ga_off // tn, gb_off // tn
    return pl.pallas_call(
        _merge_kernel,
        grid=(d // tn, n // tm),
        in_specs=[pl.BlockSpec((tm, attn.shape[1]), lambda j, i: (i, 0)),
                  pl.BlockSpec((tm, hm.shape[1]), lambda j, i: (i, 0)),
                  pl.BlockSpec((wa.shape[0], tn), lambda j, i: (0, j)),
                  pl.BlockSpec((wb.shape[0], tn), lambda j, i: (0, j)),
                  pl.BlockSpec((tm, tn), lambda j, i: (i, ga_blk + j)),
                  pl.BlockSpec((tm, tn), lambda j, i: (i, gb_blk + j))],
        out_specs=pl.BlockSpec((tm, tn), lambda j, i: (i, j)),
        out_shape=jax.ShapeDtypeStruct((n, d), BF16),
        compiler_params=_cp(("parallel", "parallel")),
        name="branch_merge",
    )(attn, hm, wa, wb, z, z)


def _outln_kernel(y_ref, w_ref, h_ref, g_ref, b_ref, of_ref, ob_ref):
    o = jnp.dot(y_ref[...], w_ref[...], preferred_element_type=F32)
    r = _ln_rows(ALPHA * h_ref[...] + o, g_ref[...], b_ref[...])
    of_ref[...] = r
    ob_ref[...] = r.astype(BF16)


def _outproj_ln(y, w, h, g, b):
    n, d = h.shape
    tm = _pick(n, (256, 128))
    row = pl.BlockSpec((tm, d), lambda i: (i, 0))
    vec = pl.BlockSpec((1, d), lambda i: (0, 0))
    return pl.pallas_call(
        _outln_kernel,
        grid=(n // tm,),
        in_specs=[row, pl.BlockSpec((d, d), lambda i: (0, 0)), row, vec, vec],
        out_specs=[row, row],
        out_shape=[jax.ShapeDtypeStruct((n, d), F32), jax.ShapeDtypeStruct((n, d), BF16)],
        compiler_params=_cp(("parallel",)),
        name="outproj_ln",
    )(y, w, h, g.reshape(1, d), b.reshape(1, d))


def _router_kernel(x_ref, wh_ref, wl_ref, eb_ref, idx_ref, wt_ref, rank_ref, cnt_ref, base_sc, *, tm):
    i = pl.program_id(0)
    E, G = N_EXPERTS, N_GROUPS
    per = E // G

    @pl.when(i == 0)
    def _():
        base_sc[...] = jnp.zeros(base_sc.shape, F32)

    x = x_ref[...]
    xh = x.astype(BF16)
    xl = (x - xh.astype(F32)).astype(BF16)
    wh, wl = wh_ref[...], wl_ref[...]
    logits = (lax.dot_general(wh, xh, NT, preferred_element_type=F32)
              + lax.dot_general(wh, xl, NT, preferred_element_type=F32)
              + lax.dot_general(wl, xh, NT, preferred_element_type=F32))
    scores = jax.nn.sigmoid(logits)
    biased = scores + eb_ref[...]
    s3 = scores.reshape(G, per, tm)
    b3 = biased.reshape(G, per, tm)
    j_io = lax.broadcasted_iota(jnp.int32, (G, per, tm), 1).astype(F32)
    g_io3 = lax.broadcasted_iota(jnp.int32, (G, per, tm), 0).astype(F32)
    e_io = g_io3 * per + j_io
    g_io = lax.broadcasted_iota(jnp.int32, (G, 1, tm), 0).astype(F32)
    ninf = -jnp.inf

    m1 = jnp.max(b3, axis=1, keepdims=True)
    i1 = jnp.min(jnp.where(b3 == m1, j_io, float(per)), axis=1, keepdims=True)
    m2 = jnp.max(jnp.where(j_io == i1, ninf, b3), axis=1, keepdims=True)
    gs = m1 + m2
    gsel = jnp.zeros((G, 1, tm), F32)
    for _ in range(TOPK_GROUPS):
        gm = jnp.max(gs, axis=0, keepdims=True)
        gi = jnp.min(jnp.where(gs == gm, g_io, float(G)), axis=0, keepdims=True)
        hit = g_io == gi
        gsel = jnp.where(hit, 1.0, gsel)
        gs = jnp.where(hit, ninf, gs)
    masked = jnp.where(gsel > 0.0, b3, ninf)

    def red2(fn, a):
        return fn(fn(a, axis=1, keepdims=True), axis=0, keepdims=True)

    sel = jnp.zeros((G, per, tm), F32)
    idxs, scs = [], []
    for _ in range(TOP_K):
        mx = red2(jnp.max, masked)
        ei = red2(jnp.min, jnp.where(masked == mx, e_io, float(E)))
        hit = e_io == ei
        scs.append(red2(jnp.sum, jnp.where(hit, s3, 0.0)))
        idxs.append(ei)
        sel = jnp.where(hit, 1.0, sel)
        masked = jnp.where(hit, ninf, masked)
    wsum = scs[0]
    for k in range(1, TOP_K):
        wsum = wsum + scs[k]

    sel2 = sel.reshape(E, tm)
    upper = (lax.broadcasted_iota(jnp.int32, (tm, tm), 0) < lax.broadcasted_iota(jnp.int32, (tm, tm), 1))
    excl = jnp.dot(sel2.astype(BF16), upper.astype(BF16), preferred_element_type=F32)
    cnt3 = (excl + base_sc[...]).reshape(G, per, tm)
    for k in range(TOP_K):
        hit = e_io == idxs[k]
        rank = red2(jnp.sum, jnp.where(hit, cnt3, 0.0))
        idx_ref[k:k + 1, :] = idxs[k].reshape(1, tm).astype(jnp.int32)
        wt_ref[k:k + 1, :] = (scs[k] / wsum * ROUTED_SCALE).reshape(1, tm)
        rank_ref[k:k + 1, :] = rank.reshape(1, tm).astype(jnp.int32)
    base_sc[...] = base_sc[...] + jnp.sum(sel2, axis=1, keepdims=True)
    cnt_ref[...] = jnp.broadcast_to(base_sc[...], cnt_ref.shape).astype(jnp.int32)


def _router(h, wr_hi, wr_lo, e_bias):
    n, d = h.shape
    tm = _pick(n, (512, 256, 128))
    outk = pl.BlockSpec((TOP_K, tm), lambda i: (0, i))
    return pl.pallas_call(
        functools.partial(_router_kernel, tm=tm),
        grid=(n // tm,),
        in_specs=[pl.BlockSpec((tm, d), lambda i: (i, 0)),
                  pl.BlockSpec((N_EXPERTS, d), lambda i: (0, 0)),
                  pl.BlockSpec((N_EXPERTS, d), lambda i: (0, 0)),
                  pl.BlockSpec((N_EXPERTS, 1), lambda i: (0, 0))],
        out_specs=[outk, outk, outk, pl.BlockSpec((N_EXPERTS, LANES), lambda i: (0, 0))],
        out_shape=[jax.ShapeDtypeStruct((TOP_K, n), jnp.int32),
                   jax.ShapeDtypeStruct((TOP_K, n), F32),
                   jax.ShapeDtypeStruct((TOP_K, n), jnp.int32),
                   jax.ShapeDtypeStruct((N_EXPERTS, LANES), jnp.int32)],
        scratch_shapes=[pltpu.VMEM((N_EXPERTS, 1), F32)],
        compiler_params=_cp(("arbitrary",)),
        name="moe_router",
    )(h, wr_hi, wr_lo, e_bias.reshape(N_EXPERTS, 1))


def _row_copy(src, dst, sem):
    return pltpu.make_async_copy(src, dst, sem)


def _dispatch_kernel(dest_ref, x_ref, init_ref, xs_ref, sem, *, tm):
    del init_ref

    def start(r, carry):
        for k in range(TOP_K):
            d = dest_ref[k, r]
            _row_copy(x_ref.at[pl.ds(r, 1), :], xs_ref.at[pl.ds(d, 1), :], sem.at[0]).start()
        return carry

    def wait(r, carry):
        for k in range(TOP_K):
            d = dest_ref[k, r]
            _row_copy(x_ref.at[pl.ds(r, 1), :], xs_ref.at[pl.ds(d, 1), :], sem.at[0]).wait()
        return carry

    lax.fori_loop(0, tm, start, 0)
    lax.fori_loop(0, tm, wait, 0)


def _dispatch(h, dest, p_rows):
    n, d = h.shape
    tm = _pick(n, (256, 128))
    return pl.pallas_call(
        functools.partial(_dispatch_kernel, tm=tm),
        grid=(n // tm,),
        in_specs=[pl.BlockSpec((TOP_K, tm), lambda i: (0, i), memory_space=pltpu.SMEM),
                  pl.BlockSpec((tm, d), lambda i: (i, 0)),
                  pl.BlockSpec(memory_space=pl.ANY)],
        out_specs=pl.BlockSpec(memory_space=pl.ANY),
        out_shape=jax.ShapeDtypeStruct((p_rows, d), F32),
        scratch_shapes=[pltpu.SemaphoreType.DMA((1,))],
        input_output_aliases={2: 0},
        compiler_params=_cp(("arbitrary",)),
        name="moe_dispatch",
    )(dest, h, jnp.zeros((p_rows, d), F32))


def _expert_kernel(be_ref, x_ref, w1_ref, w3_ref, w2_ref, o_ref):
    del be_ref
    x = x_ref[...].astype(BF16)
    a = jnp.dot(x, w1_ref[...].astype(BF16), preferred_element_type=F32)
    b = jnp.dot(x, w3_ref[...].astype(BF16), preferred_element_type=F32)
    hb = (_silu(a) * b).astype(BF16)
    o_ref[...] = jnp.dot(hb, w2_ref[...].astype(BF16), preferred_element_type=F32)


def _experts(xs, blk_e, w1, w3, w2):
    p_rows, d = xs.shape
    f = w1.shape[2]
    tm = EXPERT_ROWS
    grid_spec = pltpu.PrefetchScalarGridSpec(
        num_scalar_prefetch=1,
        grid=(p_rows // tm,),
        in_specs=[pl.BlockSpec((tm, d), lambda j, be: (j, 0)),
                  pl.BlockSpec((None, d, f), lambda j, be: (be[j], 0, 0)),
                  pl.BlockSpec((None, d, f), lambda j, be: (be[j], 0, 0)),
                  pl.BlockSpec((None, f, d), lambda j, be: (be[j], 0, 0))],
        out_specs=pl.BlockSpec((tm, d), lambda j, be: (j, 0)),
    )
    return pl.pallas_call(
        _expert_kernel,
        grid_spec=grid_spec,
        out_shape=jax.ShapeDtypeStruct((p_rows, d), F32),
        compiler_params=_cp(("arbitrary",)),
        name="moe_experts",
    )(blk_e, xs, w1, w3, w2)


def _combine_kernel(dest_ref, h_ref, wt_ref, ws1_ref, ws3_ref, ws2_ref, g_ref, b_ref, ys_ref,
                    of_ref, ob_ref, gbuf, sem, *, tm):
    def start(r, carry):
        for k in range(TOP_K):
            d = dest_ref[k, r]
            _row_copy(ys_ref.at[pl.ds(d, 1), :], gbuf.at[k, pl.ds(r, 1), :], sem.at[0]).start()
        return carry

    def wait(r, carry):
        for k in range(TOP_K):
            d = dest_ref[k, r]
            _row_copy(ys_ref.at[pl.ds(d, 1), :], gbuf.at[k, pl.ds(r, 1), :], sem.at[0]).wait()
        return carry

    lax.fori_loop(0, tm, start, 0)
    h = h_ref[...]
    xb = h.astype(BF16)
    a = jnp.dot(xb, ws1_ref[...], preferred_element_type=F32)
    b = jnp.dot(xb, ws3_ref[...], preferred_element_type=F32)
    acc = jnp.dot((_silu(a) * b).astype(BF16), ws2_ref[...], preferred_element_type=F32)
    lax.fori_loop(0, tm, wait, 0)
    wt = wt_ref[...]
    for k in range(TOP_K):
        acc = acc + wt[:, k:k + 1] * gbuf[k]
    r = _ln_rows(ALPHA * h + acc, g_ref[...], b_ref[...])
    of_ref[...] = r
    ob_ref[...] = r.astype(BF16)


def _combine(h, dest, wt_tok, ws1, ws3, ws2, g, b, ys):
    n, d = h.shape
    f = ws1.shape[1]
    tm = 128
    row = pl.BlockSpec((tm, d), lambda i: (i, 0))
    vec = pl.BlockSpec((1, d), lambda i: (0, 0))
    return pl.pallas_call(
        functools.partial(_combine_kernel, tm=tm),
        grid=(n // tm,),
        in_specs=[pl.BlockSpec((TOP_K, tm), lambda i: (0, i), memory_space=pltpu.SMEM),
                  row,
                  pl.BlockSpec((tm, TOP_K), lambda i: (i, 0)),
                  pl.BlockSpec((d, f), lambda i: (0, 0)),
                  pl.BlockSpec((d, f), lambda i: (0, 0)),
                  pl.BlockSpec((f, d), lambda i: (0, 0)),
                  vec, vec,
                  pl.BlockSpec(memory_space=pl.ANY)],
        out_specs=[row, row],
        out_shape=[jax.ShapeDtypeStruct((n, d), F32), jax.ShapeDtypeStruct((n, d), BF16)],
        scratch_shapes=[pltpu.VMEM((TOP_K, tm, d), F32), pltpu.SemaphoreType.DMA((1,))],
        compiler_params=_cp(("arbitrary",)),
        name="moe_combine",
    )(dest, h, wt_tok, ws1, ws3, ws2, g.reshape(1, d), b.reshape(1, d), ys)


def _moe(h_f32, w_router, e_bias, w1, w3, w2, ws1, ws3, ws2, g, b):
    n = h_f32.shape[0]
    wr_t = w_router.T
    wr_hi = wr_t.astype(BF16)
    wr_lo = (wr_t - wr_hi.astype(F32)).astype(BF16)
    idx, wts, rank, cnt = _router(h_f32, wr_hi, wr_lo, e_bias)
    counts = cnt[:, 0]
    blk = EXPERT_ROWS
    pcounts = (counts + blk - 1) // blk * blk
    pends = jnp.cumsum(pcounts)
    pstarts = pends - pcounts
    dest = (pstarts[idx] + rank).astype(jnp.int32)
    nb = n * TOP_K // blk + N_EXPERTS
    blk_e = jnp.clip(jnp.searchsorted(pends, jnp.arange(nb, dtype=jnp.int32) * blk, side="right"),
                     0, N_EXPERTS - 1).astype(jnp.int32)
    xs = _dispatch(h_f32, dest, nb * blk)
    ys = _experts(xs, blk_e, w1, w3, w2)
    return _combine(h_f32, dest, wts.T, ws1.astype(BF16), ws3.astype(BF16), ws2.astype(BF16), g, b, ys)


def _in_proj_layout(w_in, b_in, d, ql, kvl):
    hq, hv = ML_HEADS * ML_QK, ML_HEADS * ML_V
    sizes = (ql, kvl, QK_ROPE, hq, hq, hv, hv, ML_HEADS, ML_HEADS, d, d)
    offs = np.concatenate([[0], np.cumsum(sizes)])
    seg = lambda a, i: a[..., int(offs[i]):int(offs[i + 1])]
    order = (0, 1, 3, 4, 5, 6, 9, 10)
    w_main = jnp.concatenate([seg(w_in, i) for i in order], axis=-1)
    b_main = jnp.concatenate([seg(b_in, i) for i in order], axis=-1)
    half = QK_ROPE // 2

    def small(a):
        kr = seg(a, 2)
        x1, x2 = kr[..., :half], kr[..., half:]
        pad = jnp.zeros(a.shape[:-1] + (LANES - 2 * ML_HEADS,), a.dtype)
        return jnp.concatenate([x1, x2, x2, x1, seg(a, 7), seg(a, 8), pad], axis=-1)

    main_offs = np.concatenate([[0], np.cumsum([sizes[i] for i in order])])
    return w_main, b_main, small(w_in), small(b_in), [int(o) for o in main_offs]


def _uq_layout(w_uq):
    kq = w_uq.shape[0]
    w = w_uq.reshape(kq, MLA_HEADS, QK_NOPE + QK_ROPE)
    half = QK_ROPE // 2
    nope, x1, x2 = w[..., :QK_NOPE], w[..., QK_NOPE:QK_NOPE + half], w[..., QK_NOPE + half:]
    return jnp.concatenate([nope, x1, x2, x2, x1], axis=-1).reshape(kq, MLA_HEADS * HEAD_SLAB)


def _ukv_layout(w_ukv):
    kk = w_ukv.shape[0]
    w = w_ukv.reshape(kk, MLA_HEADS, QK_NOPE + V_HEAD)
    return jnp.concatenate([w[..., :QK_NOPE].reshape(kk, -1), w[..., QK_NOPE:].reshape(kk, -1)], axis=-1)


def _rope_tables(tp):
    half = QK_ROPE // 2
    pos = jnp.arange(tp, dtype=F32) - PADL
    inv_freq = 1.0 / (ROPE_THETA ** (jnp.arange(0, QK_ROPE, 2, dtype=F32) / QK_ROPE))
    ang = pos[:, None] * inv_freq[None, :]
    cos, sin = jnp.cos(ang), jnp.sin(ang)
    zero = jnp.zeros((tp, LANES - 2 * half), F32)
    return jnp.concatenate([cos, cos, zero], axis=1), jnp.concatenate([-sin, sin, zero], axis=1)


def kernel(x, meta, ln_in_g, ln_in_b, w_in, b_in, q_norm_g, kv_norm_g, w_uq, w_ukv, conv_w, conv_b, ml_norm_g, w_br_mla, w_br_mlstm, w_out, ln1_g, ln1_b, w_router, e_bias, w1, w3, w2, ws1, ws3, ws2, ln2_g, ln2_b):
    batch, seq, d = x.shape
    depth = w_in.shape[0]
    ql, kvl = q_norm_g.shape[1], kv_norm_g.shape[1]
    assert ql == kvl and seq % LANES == 0
    tp = LANES + seq
    n = batch * tp
    hq = ML_HEADS * ML_QK

    hp = jnp.concatenate([jnp.zeros((batch, PADL, d), x.dtype),
                          jnp.broadcast_to(meta[None].astype(x.dtype), (batch, N_META, d)), x], axis=1)
    h_f32, h_bf = _layer_norm(hp.reshape(n, d), ln_in_g, ln_in_b)
    cos_t, sin_t = _rope_tables(tp)
    col_scale = jnp.concatenate([jnp.ones((1, hq), F32), jnp.full((1, hq), ML_QK ** -0.5, F32)], axis=1)

    for l in range(depth):
        w_main, b_main, w_small, b_small, offs = _in_proj_layout(w_in[l], b_in[l], d, ql, kvl)
        o_mq, o_mv, o_mo, o_ga, o_gb = offs[2], offs[4], offs[5], offs[6], offs[7]
        z = _matmul_bias(h_bf, w_main.astype(BF16), b_main, BF16, sig_col=o_mo, name="in_proj")
        zs = _matmul_bias(h_bf, w_small.astype(BF16), b_small, F32, name="in_proj_small")

        q, k, v = _mla_qkv(z, zs, q_norm_g[l], kv_norm_g[l], _uq_layout(w_uq[l]).astype(BF16),
                           _ukv_layout(w_ukv[l]).astype(BF16), cos_t, sin_t, tp, ql)
        attn = _attention(q, k, v, batch, tp)

        assert o_mq % hq == 0 and o_mv % ML_V == 0 and o_mo % ML_V == 0
        qk = _qk_conv(z, conv_w[l], conv_b[l], col_scale, tp, o_mq // hq)
        hm = _mlstm(qk, z, zs, ml_norm_g[l], batch, tp, o_mv // ML_V, o_mo // ML_V)

        y = _merge(attn, hm, w_br_mla[l].astype(BF16), w_br_mlstm[l].astype(BF16), z, o_ga, o_gb)
        h_f32, h_bf = _outproj_ln(y, w_out[l].astype(BF16), h_f32, ln1_g[l], ln1_b[l])
        h_f32, h_bf = _moe(h_f32, w_router[l], e_bias[l], w1[l], w3[l], w2[l], ws1[l], ws3[l], ws2[l],
                           ln2_g[l], ln2_b[l])

    return h_f32.reshape(batch, tp, d)[:, LANES:, :]
```

```python
import functools

import numpy as np
import jax
import jax.numpy as jnp
from jax import lax
from jax.experimental import pallas as pl
from jax.experimental.pallas import tpu as pltpu

N_META = 16
MLA_HEADS = 8
QK_NOPE = 128
QK_ROPE = 64
V_HEAD = 128
ROPE_THETA = 10000.0
ML_HEADS = 8
ML_QK = 128
ML_V = 256
CONV_K = 4
CHUNK = 64
N_EXPERTS = 64
TOP_K = 8
N_GROUPS = 8
TOPK_GROUPS = 4
ROUTED_SCALE = 2.5
DEPTH = 2
ALPHA = (2 * DEPTH) ** 0.25
LN_EPS = 1e-5
RMS_EPS = 1e-6
NEG = -1e30

LANES = 128
HEAD_SLAB = 2 * LANES
PADL = LANES - N_META
EXPERT_ROWS = 256
VMEM_LIMIT = 56 * 1024 * 1024

F32 = jnp.float32
BF16 = jnp.bfloat16
NT = (((1,), (1,)), ((), ()))
TN = (((0,), (0,)), ((), ()))


def _pick(n, cands):
    for c in cands:
        if n % c == 0:
            return c
    raise ValueError(f"no tile in {cands} divides {n}")


def _cp(sem, vmem=None):
    return pltpu.CompilerParams(dimension_semantics=sem, vmem_limit_bytes=vmem or VMEM_LIMIT)


def _ln_rows(x, g, b):
    mu = jnp.mean(x, axis=-1, keepdims=True)
    xc = x - mu
    var = jnp.mean(xc * xc, axis=-1, keepdims=True)
    return xc * lax.rsqrt(var + LN_EPS) * g + b


def _rms_rows(x, g):
    return x * lax.rsqrt(jnp.mean(x * x, axis=-1, keepdims=True) + RMS_EPS) * g


def _silu(x):
    return x * jax.nn.sigmoid(x)


def _ln_in_kernel(head_ref, x_ref, g_ref, b_ref, of_ref, ob_ref):
    i = pl.program_id(1)

    def emit(src):
        y = _ln_rows(src, g_ref[...], b_ref[...])
        of_ref[...] = y
        ob_ref[...] = y.astype(BF16)

    @pl.when(i == 0)
    def _():
        emit(head_ref[...])

    @pl.when(i > 0)
    def _():
        emit(x_ref[...])


def _layer_norm_in(x, head, g, b):
    batch, seq, d = x.shape
    nt = seq // LANES + 1
    row = pl.BlockSpec((LANES, d), lambda bb, i: (bb * nt + i, 0))
    vec = pl.BlockSpec((1, d), lambda bb, i: (0, 0))
    n = batch * nt * LANES
    return pl.pallas_call(
        _ln_in_kernel,
        grid=(batch, nt),
        in_specs=[pl.BlockSpec((LANES, d), lambda bb, i: (0, 0)),
                  pl.BlockSpec((None, LANES, d), lambda bb, i: (bb, jnp.maximum(i - 1, 0), 0)),
                  vec, vec],
        out_specs=[row, row],
        out_shape=[jax.ShapeDtypeStruct((n, d), F32), jax.ShapeDtypeStruct((n, d), BF16)],
        compiler_params=_cp(("parallel", "arbitrary")),
        name="ln_in",
    )(head, x, g.reshape(1, d), b.reshape(1, d))


def _mm_kernel(x_ref, w_ref, b_ref, o_ref, *, sig_tile):
    acc = jnp.dot(x_ref[...], w_ref[...], preferred_element_type=F32) + b_ref[...]
    if sig_tile is None:
        o_ref[...] = acc.astype(o_ref.dtype)
        return
    j = pl.program_id(0)

    @pl.when(j < sig_tile)
    def _():
        o_ref[...] = acc.astype(o_ref.dtype)

    @pl.when(j >= sig_tile)
    def _():
        o_ref[...] = jax.nn.sigmoid(acc).astype(o_ref.dtype)


def _matmul_bias(x, w, b, out_dtype, sig_col=None, name="mm"):
    n, kd = x.shape
    nc = w.shape[1]
    tm = _pick(n, (1024, 512, 256, 128))
    tn = _pick(nc, (1024, 512, 256, 128))
    sig_tile = None
    if sig_col is not None:
        assert sig_col % tn == 0
        sig_tile = sig_col // tn
    return pl.pallas_call(
        functools.partial(_mm_kernel, sig_tile=sig_tile),
        grid=(nc // tn, n // tm),
        in_specs=[pl.BlockSpec((tm, kd), lambda j, i: (i, 0)),
                  pl.BlockSpec((kd, tn), lambda j, i: (0, j)),
                  pl.BlockSpec((1, tn), lambda j, i: (0, j))],
        out_specs=pl.BlockSpec((tm, tn), lambda j, i: (i, j)),
        out_shape=jax.ShapeDtypeStruct((n, nc), out_dtype),
        compiler_params=_cp(("parallel", "parallel")),
        name=name,
    )(x, w, b.reshape(1, nc))


def _rope_slab(r, cos_t, sin_t):
    return r * cos_t + pltpu.roll(r, 2 * (QK_ROPE // 2), 1) * sin_t


def _qproj_kernel(c_ref, g_ref, w_ref, cos_ref, sin_ref, o_ref, *, heads, scale):
    xn = _rms_rows(c_ref[...].astype(F32), g_ref[...])
    q = jnp.dot(xn.astype(BF16), w_ref[...], preferred_element_type=F32) * scale
    cos_t, sin_t = cos_ref[...], sin_ref[...]
    for h in range(heads):
        lo = h * HEAD_SLAB
        o_ref[:, lo:lo + LANES] = q[:, lo:lo + LANES].astype(BF16)
        o_ref[:, lo + LANES:lo + HEAD_SLAB] = _rope_slab(q[:, lo + LANES:lo + HEAD_SLAB], cos_t, sin_t).astype(BF16)


def _kproj_kernel(c_ref, g_ref, w_ref, kr_ref, cos_ref, sin_ref, k_ref, v_ref, *, heads):
    xn = _rms_rows(c_ref[...].astype(F32), g_ref[...])
    kv = jnp.dot(xn.astype(BF16), w_ref[...], preferred_element_type=F32)
    rr = _rope_slab(kr_ref[...], cos_ref[...], sin_ref[...]).astype(BF16)
    for h in range(heads):
        lo = h * HEAD_SLAB
        k_ref[:, lo:lo + LANES] = kv[:, h * LANES:(h + 1) * LANES].astype(BF16)
        k_ref[:, lo + LANES:lo + HEAD_SLAB] = rr
    v_ref[...] = kv[:, heads * LANES:].astype(BF16)


def _mla_qkv(z, zs, qg, kvg, wq, wkv, cos_t, sin_t, tp, ql):
    n = z.shape[0]
    heads = MLA_HEADS
    tm = _pick(tp, (384, 256, 128))
    nt = tp // tm
    scale = (QK_NOPE + QK_ROPE) ** -0.5
    tab = pl.BlockSpec((tm, LANES), lambda i: (i % nt, 0))
    q = pl.pallas_call(
        functools.partial(_qproj_kernel, heads=heads, scale=scale),
        grid=(n // tm,),
        in_specs=[pl.BlockSpec((tm, ql), lambda i: (i, 0)),
                  pl.BlockSpec((1, ql), lambda i: (0, 0)),
                  pl.BlockSpec((ql, heads * HEAD_SLAB), lambda i: (0, 0)),
                  tab, tab],
        out_specs=pl.BlockSpec((tm, heads * HEAD_SLAB), lambda i: (i, 0)),
        out_shape=jax.ShapeDtypeStruct((n, heads * HEAD_SLAB), BF16),
        compiler_params=_cp(("parallel",)),
        name="mla_qproj",
    )(z, qg.reshape(1, ql), wq, cos_t, sin_t)
    k, v = pl.pallas_call(
        functools.partial(_kproj_kernel, heads=heads),
        grid=(n // tm,),
        in_specs=[pl.BlockSpec((tm, ql), lambda i: (i, 1)),
                  pl.BlockSpec((1, ql), lambda i: (0, 0)),
                  pl.BlockSpec((ql, 2 * heads * LANES), lambda i: (0, 0)),
                  pl.BlockSpec((tm, LANES), lambda i: (i, 0)),
                  tab, tab],
        out_specs=[pl.BlockSpec((tm, heads * HEAD_SLAB), lambda i: (i, 0)),
                   pl.BlockSpec((tm, heads * LANES), lambda i: (i, 0))],
        out_shape=[jax.ShapeDtypeStruct((n, heads * HEAD_SLAB), BF16),
                   jax.ShapeDtypeStruct((n, heads * LANES), BF16)],
        compiler_params=_cp(("parallel",)),
        name="mla_kvproj",
    )(z, kvg.reshape(1, ql), wkv, zs, cos_t, sin_t)
    return q, k, v


def _attn_kernel(q_ref, k_ref, v_ref, o_ref, m_sc, l_sc, acc_sc, *, t):
    i = pl.program_id(2)
    m_sc[...] = jnp.full(m_sc.shape, NEG, F32)
    l_sc[...] = jnp.zeros(l_sc.shape, F32)
    acc_sc[...] = jnp.zeros(acc_sc.shape, F32)
    q = q_ref[...]

    def step(j, masked):
        rows = pl.ds(pl.multiple_of(j * t, t), t)
        s = lax.dot_general(q, k_ref[rows, :], NT, preferred_element_type=F32)
        if masked:
            qpos = i * t + lax.broadcasted_iota(jnp.int32, (t, t), 0)
            kpos = j * t + lax.broadcasted_iota(jnp.int32, (t, t), 1)
            s = jnp.where((kpos <= qpos) & (kpos >= PADL), s, NEG)
        m_prev = m_sc[...]
        m_new = jnp.maximum(m_prev, jnp.max(s, axis=1, keepdims=True))
        alpha = jnp.exp(m_prev - m_new)
        p = jnp.exp(s - m_new)
        l_sc[...] = alpha * l_sc[...] + jnp.sum(p, axis=1, keepdims=True)
        acc_sc[...] = alpha * acc_sc[...] + jnp.dot(p.astype(BF16), v_ref[rows, :], preferred_element_type=F32)
        m_sc[...] = m_new

    step(0, True)

    def body(j, carry):
        step(j, False)
        return carry

    lax.fori_loop(1, i, body, 0)

    @pl.when(i > 0)
    def _():
        step(i, True)

    o_ref[...] = (acc_sc[...] / l_sc[...]).astype(o_ref.dtype)


def _attention(q, k, v, batch, tp):
    n = q.shape[0]
    heads = MLA_HEADS
    t = _pick(tp, (384, 256, 128))
    nt = tp // t
    return pl.pallas_call(
        functools.partial(_attn_kernel, t=t),
        grid=(batch, heads, nt),
        in_specs=[pl.BlockSpec((t, HEAD_SLAB), lambda b, h, i: (b * nt + i, h)),
                  pl.BlockSpec((tp, HEAD_SLAB), lambda b, h, i: (b, h)),
                  pl.BlockSpec((tp, V_HEAD), lambda b, h, i: (b, h))],
        out_specs=pl.BlockSpec((t, V_HEAD), lambda b, h, i: (b * nt + i, h)),
        out_shape=jax.ShapeDtypeStruct((n, heads * V_HEAD), BF16),
        scratch_shapes=[pltpu.VMEM((t, 1), F32), pltpu.VMEM((t, 1), F32), pltpu.VMEM((t, V_HEAD), F32)],
        compiler_params=_cp(("parallel", "parallel", "parallel")),
        name="mla_attention",
    )(q, k, v)


HALO = 16


def _conv_kernel(x_ref, halo_ref, w_ref, b_ref, s_ref, o_ref, *, tm, tp):
    i = pl.program_id(0)
    start = lax.rem(i * tm, tp)
    pos = start + lax.broadcasted_iota(jnp.int32, (tm, 1), 0)
    x = jnp.where(pos >= PADL, x_ref[...].astype(F32), 0.0)
    hpos = start - HALO + lax.broadcasted_iota(jnp.int32, (HALO, 1), 0)
    halo = jnp.where(hpos >= PADL, halo_ref[...].astype(F32), 0.0)
    ext = jnp.concatenate([halo, x], axis=0)
    w = w_ref[...]
    y = b_ref[...] + w[CONV_K - 1:CONV_K, :] * x
    for j in range(1, CONV_K):
        y = y + w[CONV_K - 1 - j:CONV_K - j, :] * pltpu.roll(ext, j, 0)[HALO:, :]
    o_ref[...] = (_silu(y) * s_ref[...]).astype(BF16)


def _qk_conv(z, conv_w, conv_b, col_scale, tp, off_blk):
    n = z.shape[0]
    cw = conv_w.shape[1] // 2
    tm = _pick(tp, (384, 256, 128))
    return pl.pallas_call(
        functools.partial(_conv_kernel, tm=tm, tp=tp),
        grid=(n // tm, 2),
        in_specs=[pl.BlockSpec((tm, cw), lambda i, c: (i, off_blk + c)),
                  pl.BlockSpec((HALO, cw), lambda i, c: (jnp.maximum(i * (tm // HALO) - 1, 0), off_blk + c)),
                  pl.BlockSpec((CONV_K, cw), lambda i, c: (0, c)),
                  pl.BlockSpec((1, cw), lambda i, c: (0, c)),
                  pl.BlockSpec((1, cw), lambda i, c: (0, c))],
        out_specs=pl.BlockSpec((tm, cw), lambda i, c: (i, c)),
        out_shape=jax.ShapeDtypeStruct((n, 2 * cw), BF16),
        compiler_params=_cp(("parallel", "parallel")),
        name="mlstm_qk_conv",
    )(z, z, conv_w, conv_b.reshape(1, -1), col_scale)


def _mlstm_kernel(q_ref, k_ref, v_ref, gt_ref, og_ref, ng_ref, o_ref, c_sc, n_sc, m_sc, *, group, heads):
    h = pl.program_id(1)
    g = pl.program_id(2)
    L = CHUNK

    @pl.when(g == 0)
    def _():
        c_sc[...] = jnp.zeros(c_sc.shape, F32)
        n_sc[...] = jnp.zeros(n_sc.shape, F32)
        m_sc[...] = jnp.zeros(m_sc.shape, F32)

    lane = lax.broadcasted_iota(jnp.int32, (L, LANES), 1)
    r_i = lax.broadcasted_iota(jnp.int32, (L, L), 0)
    c_i = lax.broadcasted_iota(jnp.int32, (L, L), 1)
    eye = r_i == c_i
    causal = c_i <= r_i

    def to_row(col):
        return jnp.sum(jnp.where(eye, col, 0.0), axis=0, keepdims=True)

    def to_col(row):
        return jnp.sum(jnp.where(eye, row, 0.0), axis=1, keepdims=True)

    for c in range(group):
        rows = slice(c * L, (c + 1) * L)
        gt = gt_ref[rows, :]
        i_col = jnp.sum(jnp.where(lane == h, gt, 0.0), axis=1, keepdims=True)
        f_col = jnp.sum(jnp.where(lane == heads + h, gt, 0.0), axis=1, keepdims=True)
        pos = (g * group + c) * L + lax.broadcasted_iota(jnp.int32, (L, 1), 0)
        pad = pos < PADL
        i_col = jnp.where(pad, NEG, i_col)
        lf_col = jnp.where(pad, 0.0, -(jnp.maximum(-f_col, 0.0) + jnp.log1p(jnp.exp(-jnp.abs(f_col)))))
        b_row = jnp.sum(jnp.where(r_i <= c_i, lf_col, 0.0), axis=0, keepdims=True)
        b_col = to_col(b_row)
        i_row = to_row(i_col)
        gtot = jnp.sum(lf_col, axis=0, keepdims=True)
        m_prev = m_sc[...]
        dmat = jnp.where(causal, b_col - b_row + i_row, NEG)
        inter = b_col + m_prev
        mrow = jnp.maximum(jnp.max(dmat, axis=1, keepdims=True), inter)
        dexp = jnp.exp(dmat - mrow)
        q = q_ref[rows, :]
        k = k_ref[rows, :]
        v = v_ref[rows, :]
        s = lax.dot_general(q, k, NT, preferred_element_type=F32) * dexp
        e_in = jnp.exp(inter - mrow)
        num = (jnp.dot(s.astype(BF16), v, preferred_element_type=F32)
               + e_in * jnp.dot(q, c_sc[...].astype(BF16), preferred_element_type=F32))
        den = (jnp.sum(s, axis=1, keepdims=True)
               + e_in * jnp.sum(q.astype(F32) * n_sc[...], axis=1, keepdims=True))
        hc = num / jnp.maximum(jnp.abs(den), jnp.exp(-mrow))
        a_row = gtot - b_row + i_row
        m_new = jnp.maximum(gtot + m_prev, jnp.max(a_row, axis=1, keepdims=True))
        ea_col = to_col(jnp.exp(a_row - m_new))
        decay = jnp.exp(gtot + m_prev - m_new)
        ks = k.astype(F32) * ea_col
        c_sc[...] = decay * c_sc[...] + lax.dot_general(ks.astype(BF16), v, TN, preferred_element_type=F32)
        n_sc[...] = decay * n_sc[...] + jnp.sum(ks, axis=0, keepdims=True)
        m_sc[...] = m_new
        hn = _rms_rows(hc, ng_ref[...])
        o_ref[rows, :] = (hn * og_ref[rows, :].astype(F32)).astype(BF16)


def _mlstm(qk, z, zs, norm_g, batch, tp, v_blk, og_blk):
    n = qk.shape[0]
    heads = ML_HEADS
    nchunks = tp // CHUNK
    group = _pick(nchunks, (6, 5, 4, 3, 2, 1))
    rows = group * CHUNK
    ng = nchunks // group
    return pl.pallas_call(
        functools.partial(_mlstm_kernel, group=group, heads=heads),
        grid=(batch, heads, ng),
        in_specs=[pl.BlockSpec((rows, ML_QK), lambda b, h, g: (b * ng + g, h)),
                  pl.BlockSpec((rows, ML_QK), lambda b, h, g: (b * ng + g, heads + h)),
                  pl.BlockSpec((rows, ML_V), lambda b, h, g: (b * ng + g, v_blk + h)),
                  pl.BlockSpec((rows, LANES), lambda b, h, g: (b * ng + g, 1)),
                  pl.BlockSpec((rows, ML_V), lambda b, h, g: (b * ng + g, og_blk + h)),
                  pl.BlockSpec((1, ML_V), lambda b, h, g: (0, h))],
        out_specs=pl.BlockSpec((rows, ML_V), lambda b, h, g: (b * ng + g, h)),
        out_shape=jax.ShapeDtypeStruct((n, heads * ML_V), BF16),
        scratch_shapes=[pltpu.VMEM((ML_QK, ML_V), F32), pltpu.VMEM((1, ML_QK), F32), pltpu.VMEM((1, 1), F32)],
        compiler_params=_cp(("parallel", "parallel", "arbitrary")),
        name="mlstm_scan",
    )(qk, qk, z, zs, z, norm_g.reshape(1, -1))


def _merge_kernel(a_ref, hm_ref, wa_ref, wb_ref, ga_ref, gb_ref, o_ref):
    ya = jnp.dot(a_ref[...], wa_ref[...], preferred_element_type=F32)
    yb = jnp.dot(hm_ref[...], wb_ref[...], preferred_element_type=F32)
    o_ref[...] = (ga_ref[...].astype(F32) * ya + gb_ref[...].astype(F32) * yb).astype(BF16)


def _merge(attn, hm, wa, wb, z, ga_off, gb_off):
    n = attn.shape[0]
    d = wa.shape[1]
    tm = _pick(n, (512, 256, 128))
    tn = _pick(d, (1024, 512, 256, 128))
    assert ga_off % tn == 0 and gb_off % tn == 0
    ga_blk, gb_blk = ga_off // tn, gb_off // tn
    return pl.pallas_call(
        _merge_kernel,
        grid=(d // tn, n // tm),
        in_specs=[pl.BlockSpec((tm, attn.shape[1]), lambda j, i: (i, 0)),
                  pl.BlockSpec((tm, hm.shape[1]), lambda j, i: (i, 0)),
                  pl.BlockSpec((wa.shape[0], tn), lambda j, i: (0, j)),
                  pl.BlockSpec((wb.shape[0], tn), lambda j, i: (0, j)),
                  pl.BlockSpec((tm, tn), lambda j, i: (i, ga_blk + j)),
                  pl.BlockSpec((tm, tn), lambda j, i: (i, gb_blk + j))],
        out_specs=pl.BlockSpec((tm, tn), lambda j, i: (i, j)),
        out_shape=jax.ShapeDtypeStruct((n, d), BF16),
        compiler_params=_cp(("parallel", "parallel")),
        name="branch_merge",
    )(attn, hm, wa, wb, z, z)


def _outln_kernel(y_ref, w_ref, h_ref, g_ref, b_ref, of_ref, ob_ref):
    o = jnp.dot(y_ref[...], w_ref[...], preferred_element_type=F32)
    r = _ln_rows(ALPHA * h_ref[...] + o, g_ref[...], b_ref[...])
    of_ref[...] = r
    ob_ref[...] = r.astype(BF16)


def _outproj_ln(y, w, h, g, b):
    n, d = h.shape
    tm = _pick(n, (256, 128))
    row = pl.BlockSpec((tm, d), lambda i: (i, 0))
    vec = pl.BlockSpec((1, d), lambda i: (0, 0))
    return pl.pallas_call(
        _outln_kernel,
        grid=(n // tm,),
        in_specs=[row, pl.BlockSpec((d, d), lambda i: (0, 0)), row, vec, vec],
        out_specs=[row, row],
        out_shape=[jax.ShapeDtypeStruct((n, d), F32), jax.ShapeDtypeStruct((n, d), BF16)],
        compiler_params=_cp(("parallel",)),
        name="outproj_ln",
    )(y, w, h, g.reshape(1, d), b.reshape(1, d))


def _router_kernel(x_ref, wh_ref, wl_ref, eb_ref, idx_ref, wt_ref, rank_ref, cnt_ref, base_sc, *, tm):
    i = pl.program_id(0)
    E, G = N_EXPERTS, N_GROUPS
    per = E // G

    @pl.when(i == 0)
    def _():
        base_sc[...] = jnp.zeros(base_sc.shape, F32)

    x = x_ref[...]
    xh = x.astype(BF16)
    xl = (x - xh.astype(F32)).astype(BF16)
    wh, wl = wh_ref[...], wl_ref[...]
    logits = (lax.dot_general(wh, xh, NT, preferred_element_type=F32)
              + lax.dot_general(wh, xl, NT, preferred_element_type=F32)
              + lax.dot_general(wl, xh, NT, preferred_element_type=F32))
    scores = jax.nn.sigmoid(logits)
    biased = scores + eb_ref[...]
    s3 = scores.reshape(G, per, tm)
    b3 = biased.reshape(G, per, tm)
    j_io = lax.broadcasted_iota(jnp.int32, (G, per, tm), 1).astype(F32)
    g_io3 = lax.broadcasted_iota(jnp.int32, (G, per, tm), 0).astype(F32)
    e_io = g_io3 * per + j_io
    g_io = lax.broadcasted_iota(jnp.int32, (G, 1, tm), 0).astype(F32)
    ninf = -jnp.inf

    m1 = jnp.max(b3, axis=1, keepdims=True)
    i1 = jnp.min(jnp.where(b3 == m1, j_io, float(per)), axis=1, keepdims=True)
    m2 = jnp.max(jnp.where(j_io == i1, ninf, b3), axis=1, keepdims=True)
    gs = m1 + m2
    gsel = jnp.zeros((G, 1, tm), F32)
    for _ in range(TOPK_GROUPS):
        gm = jnp.max(gs, axis=0, keepdims=True)
        gi = jnp.min(jnp.where(gs == gm, g_io, float(G)), axis=0, keepdims=True)
        hit = g_io == gi
        gsel = jnp.where(hit, 1.0, gsel)
        gs = jnp.where(hit, ninf, gs)
    masked = jnp.where(gsel > 0.0, b3, ninf)

    def red2(fn, a):
        return fn(fn(a, axis=1, keepdims=True), axis=0, keepdims=True)

    sel = jnp.zeros((G, per, tm), F32)
    idxs, scs = [], []
    for _ in range(TOP_K):
        mx = red2(jnp.max, masked)
        ei = red2(jnp.min, jnp.where(masked == mx, e_io, float(E)))
        hit = e_io == ei
        scs.append(red2(jnp.sum, jnp.where(hit, s3, 0.0)))
        idxs.append(ei)
        sel = jnp.where(hit, 1.0, sel)
        masked = jnp.where(hit, ninf, masked)
    wsum = scs[0]
    for k in range(1, TOP_K):
        wsum = wsum + scs[k]

    sel2 = sel.reshape(E, tm)
    upper = (lax.broadcasted_iota(jnp.int32, (tm, tm), 0) < lax.broadcasted_iota(jnp.int32, (tm, tm), 1))
    excl = jnp.dot(sel2.astype(BF16), upper.astype(BF16), preferred_element_type=F32)
    cnt3 = (excl + base_sc[...]).reshape(G, per, tm)
    for k in range(TOP_K):
        hit = e_io == idxs[k]
        rank = red2(jnp.sum, jnp.where(hit, cnt3, 0.0))
        idx_ref[k:k + 1, :] = idxs[k].reshape(1, tm).astype(jnp.int32)
        wt_ref[k:k + 1, :] = (scs[k] / wsum * ROUTED_SCALE).reshape(1, tm)
        rank_ref[k:k + 1, :] = rank.reshape(1, tm).astype(jnp.int32)
    base_sc[...] = base_sc[...] + jnp.sum(sel2, axis=1, keepdims=True)
    cnt_ref[...] = jnp.broadcast_to(base_sc[...], cnt_ref.shape).astype(jnp.int32)


def _router(h, wr_hi, wr_lo, e_bias):
    n, d = h.shape
    tm = _pick(n, (512, 256, 128))
    outk = pl.BlockSpec((TOP_K, tm), lambda i: (0, i))
    return pl.pallas_call(
        functools.partial(_router_kernel, tm=tm),
        grid=(n // tm,),
        in_specs=[pl.BlockSpec((tm, d), lambda i: (i, 0)),
                  pl.BlockSpec((N_EXPERTS, d), lambda i: (0, 0)),
                  pl.BlockSpec((N_EXPERTS, d), lambda i: (0, 0)),
                  pl.BlockSpec((N_EXPERTS, 1), lambda i: (0, 0))],
        out_specs=[outk, outk, outk, pl.BlockSpec((N_EXPERTS, LANES), lambda i: (0, 0))],
        out_shape=[jax.ShapeDtypeStruct((TOP_K, n), jnp.int32),
                   jax.ShapeDtypeStruct((TOP_K, n), F32),
                   jax.ShapeDtypeStruct((TOP_K, n), jnp.int32),
                   jax.ShapeDtypeStruct((N_EXPERTS, LANES), jnp.int32)],
        scratch_shapes=[pltpu.VMEM((N_EXPERTS, 1), F32)],
        compiler_params=_cp(("arbitrary",)),
        name="moe_router",
    )(h, wr_hi, wr_lo, e_bias.reshape(N_EXPERTS, 1))


def _row_copy(src, dst, sem):
    return pltpu.make_async_copy(src, dst, sem)


TOK_ROWS = 8


def _expert_kernel(be_ref, nu_ref, tok_cur, tok_nxt, h_ref, w1_ref, w3_ref, w2_ref, o_ref,
                   xbuf, w1b, w3b, w2b, sem, *, tm):
    j = pl.program_id(0)
    n_used = nu_ref[0]
    slot = lax.rem(j, 2)

    def gather(tok_ref, blk, dst_slot, wait):
        row = lax.rem(blk, TOK_ROWS)

        def body(r, carry):
            t = tok_ref[row, r]
            cp = _row_copy(h_ref.at[pl.ds(t, 1), :], xbuf.at[dst_slot, pl.ds(r, 1), :], sem.at[dst_slot])
            if wait:
                cp.wait()
            else:
                cp.start()
            return carry

        lax.fori_loop(0, tm, body, 0, unroll=8)

    @pl.when(j == 0)
    def _():
        gather(tok_cur, j, slot, False)

    @pl.when(j + 1 < n_used)
    def _():
        gather(tok_nxt, j + 1, 1 - slot, False)

    @pl.when(j < n_used)
    def _():
        @pl.when((j == 0) | (be_ref[j] != be_ref[jnp.maximum(j - 1, 0)]))
        def _():
            w1b[...] = w1_ref[...].astype(BF16)
            w3b[...] = w3_ref[...].astype(BF16)
            w2b[...] = w2_ref[...].astype(BF16)

        gather(tok_cur, j, slot, True)
        x = xbuf[slot].astype(BF16)
        a = jnp.dot(x, w1b[...], preferred_element_type=F32)
        b = jnp.dot(x, w3b[...], preferred_element_type=F32)
        hb = (_silu(a) * b).astype(BF16)
        o_ref[...] = jnp.dot(hb, w2b[...], preferred_element_type=F32)

    @pl.when(j >= n_used)
    def _():
        o_ref[...] = jnp.zeros(o_ref.shape, F32)


def _experts(h, tok_of_slot, blk_e, n_used, w1, w3, w2):
    n, d = h.shape
    nb, tm = tok_of_slot.shape
    f = w1.shape[2]
    assert nb % TOK_ROWS == 0
    grid_spec = pltpu.PrefetchScalarGridSpec(
        num_scalar_prefetch=2,
        grid=(nb,),
        in_specs=[pl.BlockSpec((TOK_ROWS, tm), lambda j, be, nu: (j // TOK_ROWS, 0), memory_space=pltpu.SMEM),
                  pl.BlockSpec((TOK_ROWS, tm), lambda j, be, nu: (jnp.minimum(j + 1, nb - 1) // TOK_ROWS, 0),
                               memory_space=pltpu.SMEM),
                  pl.BlockSpec(memory_space=pl.ANY),
                  pl.BlockSpec((None, d, f), lambda j, be, nu: (be[j], 0, 0)),
                  pl.BlockSpec((None, d, f), lambda j, be, nu: (be[j], 0, 0)),
                  pl.BlockSpec((None, f, d), lambda j, be, nu: (be[j], 0, 0))],
        out_specs=pl.BlockSpec((tm, d), lambda j, be, nu: (j, 0)),
        scratch_shapes=[pltpu.VMEM((2, tm, d), F32),
                        pltpu.VMEM((d, f), BF16), pltpu.VMEM((d, f), BF16), pltpu.VMEM((f, d), BF16),
                        pltpu.SemaphoreType.DMA((2,))],
    )
    return pl.pallas_call(
        functools.partial(_expert_kernel, tm=tm),
        grid_spec=grid_spec,
        out_shape=jax.ShapeDtypeStruct((nb * tm, d), F32),
        compiler_params=_cp(("arbitrary",)),
        name="moe_experts",
    )(blk_e, n_used, tok_of_slot, tok_of_slot, h, w1, w3, w2)


def _combine_kernel(dest_ref, h_ref, wt_ref, ws1_ref, ws3_ref, ws2_ref, g_ref, b_ref, ys_ref,
                    of_ref, *rest, tm):
    ob_ref = rest[0] if len(rest) == 3 else None
    gbuf, sem = rest[-2:]

    def start(r, carry):
        for k in range(TOP_K):
            d = dest_ref[k, r]
            _row_copy(ys_ref.at[pl.ds(d, 1), :], gbuf.at[k, pl.ds(r, 1), :], sem.at[0]).start()
        return carry

    def wait(r, carry):
        for k in range(TOP_K):
            d = dest_ref[k, r]
            _row_copy(ys_ref.at[pl.ds(d, 1), :], gbuf.at[k, pl.ds(r, 1), :], sem.at[0]).wait()
        return carry

    lax.fori_loop(0, tm, start, 0)
    h = h_ref[...]
    xb = h.astype(BF16)
    a = jnp.dot(xb, ws1_ref[...], preferred_element_type=F32)
    b = jnp.dot(xb, ws3_ref[...], preferred_element_type=F32)
    acc = jnp.dot((_silu(a) * b).astype(BF16), ws2_ref[...], preferred_element_type=F32)
    lax.fori_loop(0, tm, wait, 0)
    wt = wt_ref[...]
    for k in range(TOP_K):
        acc = acc + wt[:, k:k + 1] * gbuf[k]
    r = _ln_rows(ALPHA * h + acc, g_ref[...], b_ref[...])
    of_ref[...] = r
    if ob_ref is not None:
        ob_ref[...] = r.astype(BF16)


def _combine(h, dest, wt_tok, ws1, ws3, ws2, g, b, ys, final_shape=None):
    n, d = h.shape
    f = ws1.shape[1]
    tm = LANES
    row = pl.BlockSpec((tm, d), lambda i: (i, 0))
    vec = pl.BlockSpec((1, d), lambda i: (0, 0))
    if final_shape is None:
        out_specs = [row, row]
        out_shape = [jax.ShapeDtypeStruct((n, d), F32), jax.ShapeDtypeStruct((n, d), BF16)]
    else:
        batch, seq = final_shape
        nt = seq // tm + 1
        out_specs = [pl.BlockSpec((None, tm, d), lambda i: (i // nt, jnp.maximum(i % nt - 1, 0), 0))]
        out_shape = [jax.ShapeDtypeStruct((batch, seq, d), F32)]
    return pl.pallas_call(
        functools.partial(_combine_kernel, tm=tm),
        grid=(n // tm,),
        in_specs=[pl.BlockSpec((TOP_K, tm), lambda i: (0, i), memory_space=pltpu.SMEM),
                  row,
                  pl.BlockSpec((tm, TOP_K), lambda i: (i, 0)),
                  pl.BlockSpec((d, f), lambda i: (0, 0)),
                  pl.BlockSpec((d, f), lambda i: (0, 0)),
                  pl.BlockSpec((f, d), lambda i: (0, 0)),
                  vec, vec,
                  pl.BlockSpec(memory_space=pl.ANY)],
        out_specs=out_specs,
        out_shape=out_shape,
        scratch_shapes=[pltpu.VMEM((TOP_K, tm, d), F32), pltpu.SemaphoreType.DMA((1,))],
        compiler_params=_cp(("arbitrary",)),
        name="moe_combine",
    )(dest, h, wt_tok, ws1, ws3, ws2, g.reshape(1, d), b.reshape(1, d), ys)


def _moe(h_f32, w_router, e_bias, w1, w3, w2, ws1, ws3, ws2, g, b, final_shape=None):
    n = h_f32.shape[0]
    wr_t = w_router.T
    wr_hi = wr_t.astype(BF16)
    wr_lo = (wr_t - wr_hi.astype(F32)).astype(BF16)
    idx, wts, rank, cnt = _router(h_f32, wr_hi, wr_lo, e_bias)
    counts = cnt[:, 0]
    blk = EXPERT_ROWS
    pcounts = (counts + blk - 1) // blk * blk
    pends = jnp.cumsum(pcounts)
    pstarts = pends - pcounts
    e_ids = jnp.arange(N_EXPERTS, dtype=jnp.int32)
    start_of = jnp.sum(jnp.where(idx[None] == e_ids[:, None, None], pstarts[:, None, None], 0), axis=0)
    dest = (start_of + rank).astype(jnp.int32)
    nb = n * TOP_K // blk + N_EXPERTS
    blk_e = jnp.minimum(jnp.sum(pends[None, :] <= (jnp.arange(nb, dtype=jnp.int32) * blk)[:, None], axis=1),
                        N_EXPERTS - 1).astype(jnp.int32)
    n_used = (pends[-1:] // blk).astype(jnp.int32)
    tok = jnp.broadcast_to(jnp.arange(n, dtype=jnp.int32)[None], (TOP_K, n))
    tok_of_slot = jnp.zeros((nb * blk,), jnp.int32).at[dest.reshape(-1)].set(
        tok.reshape(-1), unique_indices=True).reshape(nb, blk)
    ys = _experts(h_f32, tok_of_slot, blk_e, n_used, w1, w3, w2)
    return _combine(h_f32, dest, wts.T, ws1.astype(BF16), ws3.astype(BF16), ws2.astype(BF16), g, b, ys,
                    final_shape)


def _in_proj_layout(w_in, b_in, d, ql, kvl):
    hq, hv = ML_HEADS * ML_QK, ML_HEADS * ML_V
    sizes = (ql, kvl, QK_ROPE, hq, hq, hv, hv, ML_HEADS, ML_HEADS, d, d)
    offs = np.concatenate([[0], np.cumsum(sizes)])
    seg = lambda a, i: a[..., int(offs[i]):int(offs[i + 1])]
    order = (0, 1, 3, 4, 5, 6, 9, 10)
    w_main = jnp.concatenate([seg(w_in, i) for i in order], axis=-1)
    b_main = jnp.concatenate([seg(b_in, i) for i in order], axis=-1)
    half = QK_ROPE // 2

    def small(a):
        kr = seg(a, 2)
        x1, x2 = kr[..., :half], kr[..., half:]
        pad = jnp.zeros(a.shape[:-1] + (LANES - 2 * ML_HEADS,), a.dtype)
        return jnp.concatenate([x1, x2, x2, x1, seg(a, 7), seg(a, 8), pad], axis=-1)

    main_offs = np.concatenate([[0], np.cumsum([sizes[i] for i in order])])
    return w_main, b_main, small(w_in), small(b_in), [int(o) for o in main_offs]


def _uq_layout(w_uq):
    kq = w_uq.shape[0]
    w = w_uq.reshape(kq, MLA_HEADS, QK_NOPE + QK_ROPE)
    half = QK_ROPE // 2
    nope, x1, x2 = w[..., :QK_NOPE], w[..., QK_NOPE:QK_NOPE + half], w[..., QK_NOPE + half:]
    return jnp.concatenate([nope, x1, x2, x2, x1], axis=-1).reshape(kq, MLA_HEADS * HEAD_SLAB)


def _ukv_layout(w_ukv):
    kk = w_ukv.shape[0]
    w = w_ukv.reshape(kk, MLA_HEADS, QK_NOPE + V_HEAD)
    return jnp.concatenate([w[..., :QK_NOPE].reshape(kk, -1), w[..., QK_NOPE:].reshape(kk, -1)], axis=-1)


def _rope_tables(tp):
    half = QK_ROPE // 2
    pos = jnp.arange(tp, dtype=F32) - PADL
    inv_freq = 1.0 / (ROPE_THETA ** (jnp.arange(0, QK_ROPE, 2, dtype=F32) / QK_ROPE))
    ang = pos[:, None] * inv_freq[None, :]
    cos, sin = jnp.cos(ang), jnp.sin(ang)
    zero = jnp.zeros((tp, LANES - 2 * half), F32)
    return jnp.concatenate([cos, cos, zero], axis=1), jnp.concatenate([-sin, sin, zero], axis=1)


def kernel(x, meta, ln_in_g, ln_in_b, w_in, b_in, q_norm_g, kv_norm_g, w_uq, w_ukv, conv_w, conv_b, ml_norm_g, w_br_mla, w_br_mlstm, w_out, ln1_g, ln1_b, w_router, e_bias, w1, w3, w2, ws1, ws3, ws2, ln2_g, ln2_b):
    batch, seq, d = x.shape
    depth = w_in.shape[0]
    ql, kvl = q_norm_g.shape[1], kv_norm_g.shape[1]
    assert ql == kvl and seq % LANES == 0
    tp = LANES + seq
    n = batch * tp
    hq = ML_HEADS * ML_QK

    head = jnp.concatenate([jnp.zeros((PADL, d), x.dtype), meta.astype(x.dtype)], axis=0)
    h_f32, h_bf = _layer_norm_in(x, head, ln_in_g, ln_in_b)
    cos_t, sin_t = _rope_tables(tp)
    col_scale = jnp.concatenate([jnp.ones((1, hq), F32), jnp.full((1, hq), ML_QK ** -0.5, F32)], axis=1)

    for l in range(depth):
        w_main, b_main, w_small, b_small, offs = _in_proj_layout(w_in[l], b_in[l], d, ql, kvl)
        o_mq, o_mv, o_mo, o_ga, o_gb = offs[2], offs[4], offs[5], offs[6], offs[7]
        z = _matmul_bias(h_bf, w_main.astype(BF16), b_main, BF16, sig_col=o_mo, name="in_proj")
        zs = _matmul_bias(h_bf, w_small.astype(BF16), b_small, F32, name="in_proj_small")

        q, k, v = _mla_qkv(z, zs, q_norm_g[l], kv_norm_g[l], _uq_layout(w_uq[l]).astype(BF16),
                           _ukv_layout(w_ukv[l]).astype(BF16), cos_t, sin_t, tp, ql)
        attn = _attention(q, k, v, batch, tp)

        assert o_mq % hq == 0 and o_mv % ML_V == 0 and o_mo % ML_V == 0
        qk = _qk_conv(z, conv_w[l], conv_b[l], col_scale, tp, o_mq // hq)
        hm = _mlstm(qk, z, zs, ml_norm_g[l], batch, tp, o_mv // ML_V, o_mo // ML_V)

        y = _merge(attn, hm, w_br_mla[l].astype(BF16), w_br_mlstm[l].astype(BF16), z, o_ga, o_gb)
        h_f32, h_bf = _outproj_ln(y, w_out[l].astype(BF16), h_f32, ln1_g[l], ln1_b[l])
        outs = _moe(h_f32, w_router[l], e_bias[l], w1[l], w3[l], w2[l], ws1[l], ws3[l], ws2[l],
                    ln2_g[l], ln2_b[l], final_shape=(batch, seq) if l == depth - 1 else None)
        if l == depth - 1:
            return outs[0]
        h_f32, h_bf = outs
```

```python
import functools

import numpy as np
import jax
import jax.numpy as jnp
from jax import lax
from jax.experimental import pallas as pl
from jax.experimental.pallas import tpu as pltpu

N_META = 16
MLA_HEADS = 8
QK_NOPE = 128
QK_ROPE = 64
V_HEAD = 128
ROPE_THETA = 10000.0
ML_HEADS = 8
ML_QK = 128
ML_V = 256
CONV_K = 4
CHUNK = 64
N_EXPERTS = 64
TOP_K = 8
N_GROUPS = 8
TOPK_GROUPS = 4
ROUTED_SCALE = 2.5
DEPTH = 2
ALPHA = (2 * DEPTH) ** 0.25
LN_EPS = 1e-5
RMS_EPS = 1e-6
NEG = -1e30

LANES = 128
HEAD_SLAB = 2 * LANES
PADL = LANES - N_META
EXPERT_ROWS = 256
VMEM_LIMIT = 56 * 1024 * 1024

F32 = jnp.float32
BF16 = jnp.bfloat16
U32 = jnp.uint32
NT = (((1,), (1,)), ((), ()))
TN = (((0,), (0,)), ((), ()))


def _pick(n, cands):
    for c in cands:
        if n % c == 0:
            return c
    raise ValueError(f"no tile in {cands} divides {n}")


def _cp(sem, vmem=None):
    return pltpu.CompilerParams(dimension_semantics=sem, vmem_limit_bytes=vmem or VMEM_LIMIT)


def _ln_rows(x, g, b):
    mu = jnp.mean(x, axis=-1, keepdims=True)
    xc = x - mu
    var = jnp.mean(xc * xc, axis=-1, keepdims=True)
    return xc * lax.rsqrt(var + LN_EPS) * g + b


def _rms_rows(x, g):
    return x * lax.rsqrt(jnp.mean(x * x, axis=-1, keepdims=True) + RMS_EPS) * g


def _silu(x):
    return x * jax.nn.sigmoid(x)


def _store_slab(ref, x):
    for s in range(x.shape[1] // HEAD_SLAB):
        lo = lax.bitcast_convert_type(x[:, s * HEAD_SLAB:s * HEAD_SLAB + LANES].astype(BF16).astype(F32), U32)
        hi = lax.bitcast_convert_type(x[:, s * HEAD_SLAB + LANES:(s + 1) * HEAD_SLAB].astype(BF16).astype(F32), U32)
        ref[:, s, :] = (lo >> 16) | hi


def _load_slab_pairs(u):
    return (lax.bitcast_convert_type(u << 16, F32),
            lax.bitcast_convert_type(u & jnp.uint32(0xFFFF0000), F32))


def _ln_in_kernel(head_ref, x_ref, g_ref, b_ref, of_ref, ob_ref):
    i = pl.program_id(1)

    def emit(src):
        y = _ln_rows(src, g_ref[...], b_ref[...])
        of_ref[...] = y
        ob_ref[...] = y.astype(BF16)

    @pl.when(i == 0)
    def _():
        emit(head_ref[...])

    @pl.when(i > 0)
    def _():
        emit(x_ref[...])


def _layer_norm_in(x, head, g, b):
    batch, seq, d = x.shape
    nt = seq // LANES + 1
    row = pl.BlockSpec((LANES, d), lambda bb, i: (bb * nt + i, 0))
    vec = pl.BlockSpec((1, d), lambda bb, i: (0, 0))
    n = batch * nt * LANES
    return pl.pallas_call(
        _ln_in_kernel,
        grid=(batch, nt),
        in_specs=[pl.BlockSpec((LANES, d), lambda bb, i: (0, 0)),
                  pl.BlockSpec((None, LANES, d), lambda bb, i: (bb, jnp.maximum(i - 1, 0), 0)),
                  vec, vec],
        out_specs=[row, row],
        out_shape=[jax.ShapeDtypeStruct((n, d), F32), jax.ShapeDtypeStruct((n, d), BF16)],
        compiler_params=_cp(("parallel", "arbitrary")),
        name="ln_in",
    )(head, x, g.reshape(1, d), b.reshape(1, d))


def _mm_kernel(x_ref, w_ref, b_ref, o_ref, *, sig_tile):
    acc = jnp.dot(x_ref[...], w_ref[...], preferred_element_type=F32) + b_ref[...]
    if sig_tile is None:
        o_ref[...] = acc.astype(o_ref.dtype)
        return
    j = pl.program_id(0)

    @pl.when(j < sig_tile)
    def _():
        o_ref[...] = acc.astype(o_ref.dtype)

    @pl.when(j >= sig_tile)
    def _():
        o_ref[...] = jax.nn.sigmoid(acc).astype(o_ref.dtype)


def _matmul_bias(x, w, b, out_dtype, sig_col=None, name="mm"):
    n, kd = x.shape
    nc = w.shape[1]
    tm = _pick(n, (1024, 512, 256, 128))
    tn = _pick(nc, (1024, 512, 256, 128))
    sig_tile = None
    if sig_col is not None:
        assert sig_col % tn == 0
        sig_tile = sig_col // tn
    return pl.pallas_call(
        functools.partial(_mm_kernel, sig_tile=sig_tile),
        grid=(nc // tn, n // tm),
        in_specs=[pl.BlockSpec((tm, kd), lambda j, i: (i, 0)),
                  pl.BlockSpec((kd, tn), lambda j, i: (0, j)),
                  pl.BlockSpec((1, tn), lambda j, i: (0, j))],
        out_specs=pl.BlockSpec((tm, tn), lambda j, i: (i, j)),
        out_shape=jax.ShapeDtypeStruct((n, nc), out_dtype),
        compiler_params=_cp(("parallel", "parallel")),
        name=name,
    )(x, w, b.reshape(1, nc))


def _rope_slab(r, cos_t, sin_t):
    return r * cos_t + pltpu.roll(r, 2 * (QK_ROPE // 2), 1) * sin_t


def _qproj_kernel(c_ref, g_ref, w_ref, cos_ref, sin_ref, o_ref, *, heads, scale):
    xn = _rms_rows(c_ref[...].astype(F32), g_ref[...])
    q = jnp.dot(xn.astype(BF16), w_ref[...], preferred_element_type=F32) * scale
    cos_t, sin_t = cos_ref[...], sin_ref[...]
    for h in range(heads):
        lo = h * HEAD_SLAB
        o_ref[:, lo:lo + LANES] = q[:, lo:lo + LANES].astype(BF16)
        o_ref[:, lo + LANES:lo + HEAD_SLAB] = _rope_slab(q[:, lo + LANES:lo + HEAD_SLAB], cos_t, sin_t).astype(BF16)


def _kproj_kernel(c_ref, g_ref, w_ref, kr_ref, cos_ref, sin_ref, k_ref, v_ref, *, heads):
    xn = _rms_rows(c_ref[...].astype(F32), g_ref[...])
    kv = jnp.dot(xn.astype(BF16), w_ref[...], preferred_element_type=F32)
    rr = _rope_slab(kr_ref[...], cos_ref[...], sin_ref[...]).astype(BF16)
    for h in range(heads):
        lo = h * HEAD_SLAB
        k_ref[:, lo:lo + LANES] = kv[:, h * LANES:(h + 1) * LANES].astype(BF16)
        k_ref[:, lo + LANES:lo + HEAD_SLAB] = rr
    v_ref[...] = kv[:, heads * LANES:].astype(BF16)


def _mla_qkv(z, zs, qg, kvg, wq, wkv, cos_t, sin_t, tp, ql):
    n = z.shape[0]
    heads = MLA_HEADS
    tm = _pick(tp, (384, 256, 128))
    nt = tp // tm
    scale = (QK_NOPE + QK_ROPE) ** -0.5
    tab = pl.BlockSpec((tm, LANES), lambda i: (i % nt, 0))
    q = pl.pallas_call(
        functools.partial(_qproj_kernel, heads=heads, scale=scale),
        grid=(n // tm,),
        in_specs=[pl.BlockSpec((tm, ql), lambda i: (i, 0)),
                  pl.BlockSpec((1, ql), lambda i: (0, 0)),
                  pl.BlockSpec((ql, heads * HEAD_SLAB), lambda i: (0, 0)),
                  tab, tab],
        out_specs=pl.BlockSpec((tm, heads * HEAD_SLAB), lambda i: (i, 0)),
        out_shape=jax.ShapeDtypeStruct((n, heads * HEAD_SLAB), BF16),
        compiler_params=_cp(("parallel",)),
        name="mla_qproj",
    )(z, qg.reshape(1, ql), wq, cos_t, sin_t)
    k, v = pl.pallas_call(
        functools.partial(_kproj_kernel, heads=heads),
        grid=(n // tm,),
        in_specs=[pl.BlockSpec((tm, ql), lambda i: (i, 1)),
                  pl.BlockSpec((1, ql), lambda i: (0, 0)),
                  pl.BlockSpec((ql, 2 * heads * LANES), lambda i: (0, 0)),
                  pl.BlockSpec((tm, LANES), lambda i: (i, 0)),
                  tab, tab],
        out_specs=[pl.BlockSpec((tm, heads * HEAD_SLAB), lambda i: (i, 0)),
                   pl.BlockSpec((tm, heads * LANES), lambda i: (i, 0))],
        out_shape=[jax.ShapeDtypeStruct((n, heads * HEAD_SLAB), BF16),
                   jax.ShapeDtypeStruct((n, heads * LANES), BF16)],
        compiler_params=_cp(("parallel",)),
        name="mla_kvproj",
    )(z, kvg.reshape(1, ql), wkv, zs, cos_t, sin_t)
    return q, k, v


ATTN_HEADS_PER_STEP = 2


def _attn_kernel(q_ref, k_ref, v_ref, o_ref, m_sc, l_sc, acc_sc, *, t, hps):
    i = pl.program_id(2)
    m_sc[...] = jnp.full(m_sc.shape, NEG, F32)
    l_sc[...] = jnp.zeros(l_sc.shape, F32)
    acc_sc[...] = jnp.zeros(acc_sc.shape, F32)

    def step(j, masked):
        rows = pl.ds(pl.multiple_of(j * t, t), t)
        if masked:
            qpos = i * t + lax.broadcasted_iota(jnp.int32, (t, t), 0)
            kpos = j * t + lax.broadcasted_iota(jnp.int32, (t, t), 1)
            keep = (kpos <= qpos) & (kpos >= PADL)
        for h in range(hps):
            q = q_ref[:, h * HEAD_SLAB:(h + 1) * HEAD_SLAB]
            s = lax.dot_general(q, k_ref[rows, h * HEAD_SLAB:(h + 1) * HEAD_SLAB], NT, preferred_element_type=F32)
            if masked:
                s = jnp.where(keep, s, NEG)
            m_prev = m_sc[h]
            m_new = jnp.maximum(m_prev, jnp.max(s, axis=1, keepdims=True))
            alpha = jnp.exp(m_prev - m_new)
            p = jnp.exp(s - m_new)
            l_sc[h] = alpha * l_sc[h] + jnp.sum(p, axis=1, keepdims=True)
            acc_sc[h] = alpha * acc_sc[h] + jnp.dot(p.astype(BF16), v_ref[rows, h * V_HEAD:(h + 1) * V_HEAD],
                                                    preferred_element_type=F32)
            m_sc[h] = m_new

    step(0, True)

    def body(j, carry):
        step(j, False)
        return carry

    lax.fori_loop(1, i, body, 0)

    @pl.when(i > 0)
    def _():
        step(i, True)

    for h in range(hps):
        o_ref[:, h * V_HEAD:(h + 1) * V_HEAD] = (acc_sc[h] / l_sc[h]).astype(o_ref.dtype)


def _attention(q, k, v, batch, tp):
    n = q.shape[0]
    heads = MLA_HEADS
    hps = ATTN_HEADS_PER_STEP
    assert heads % hps == 0
    t = _pick(tp, (384, 256, 128))
    nt = tp // t
    return pl.pallas_call(
        functools.partial(_attn_kernel, t=t, hps=hps),
        grid=(batch, heads // hps, nt),
        in_specs=[pl.BlockSpec((t, hps * HEAD_SLAB), lambda b, h, i: (b * nt + i, h)),
                  pl.BlockSpec((tp, hps * HEAD_SLAB), lambda b, h, i: (b, h)),
                  pl.BlockSpec((tp, hps * V_HEAD), lambda b, h, i: (b, h))],
        out_specs=pl.BlockSpec((t, hps * V_HEAD), lambda b, h, i: (b * nt + i, h)),
        out_shape=jax.ShapeDtypeStruct((n, heads * V_HEAD), BF16),
        scratch_shapes=[pltpu.VMEM((hps, t, 1), F32), pltpu.VMEM((hps, t, 1), F32),
                        pltpu.VMEM((hps, t, V_HEAD), F32)],
        compiler_params=_cp(("parallel", "parallel", "parallel")),
        name="mla_attention",
    )(q, k, v)


HALO = 16


def _conv_kernel(x_ref, halo_ref, w_ref, b_ref, s_ref, o_ref, *, tm, tp):
    i = pl.program_id(0)
    start = lax.rem(i * tm, tp)
    pos = start + lax.broadcasted_iota(jnp.int32, (tm, 1), 0)
    x = jnp.where(pos >= PADL, x_ref[...].astype(F32), 0.0)
    hpos = start - HALO + lax.broadcasted_iota(jnp.int32, (HALO, 1), 0)
    halo = jnp.where(hpos >= PADL, halo_ref[...].astype(F32), 0.0)
    ext = jnp.concatenate([halo, x], axis=0)
    w = w_ref[...]
    y = b_ref[...] + w[CONV_K - 1:CONV_K, :] * x
    for j in range(1, CONV_K):
        y = y + w[CONV_K - 1 - j:CONV_K - j, :] * pltpu.roll(ext, j, 0)[HALO:, :]
    o_ref[...] = (_silu(y) * s_ref[...]).astype(BF16)


def _qk_conv(z, conv_w, conv_b, col_scale, tp, off_blk):
    n = z.shape[0]
    cw = conv_w.shape[1] // 2
    tm = _pick(tp, (384, 256, 128))
    return pl.pallas_call(
        functools.partial(_conv_kernel, tm=tm, tp=tp),
        grid=(n // tm, 2),
        in_specs=[pl.BlockSpec((tm, cw), lambda i, c: (i, off_blk + c)),
                  pl.BlockSpec((HALO, cw), lambda i, c: (jnp.maximum(i * (tm // HALO) - 1, 0), off_blk + c)),
                  pl.BlockSpec((CONV_K, cw), lambda i, c: (0, c)),
                  pl.BlockSpec((1, cw), lambda i, c: (0, c)),
                  pl.BlockSpec((1, cw), lambda i, c: (0, c))],
        out_specs=pl.BlockSpec((tm, cw), lambda i, c: (i, c)),
        out_shape=jax.ShapeDtypeStruct((n, 2 * cw), BF16),
        compiler_params=_cp(("parallel", "parallel")),
        name="mlstm_qk_conv",
    )(z, z, conv_w, conv_b.reshape(1, -1), col_scale)


def _mlstm_kernel(q_ref, k_ref, v_ref, gt_ref, og_ref, ng_ref, o_ref, c_sc, n_sc, m_sc, *, group, heads):
    h = pl.program_id(1)
    g = pl.program_id(2)
    L = CHUNK

    @pl.when(g == 0)
    def _():
        c_sc[...] = jnp.zeros(c_sc.shape, F32)
        n_sc[...] = jnp.zeros(n_sc.shape, F32)
        m_sc[...] = jnp.zeros(m_sc.shape, F32)

    lane = lax.broadcasted_iota(jnp.int32, (L, LANES), 1)
    r_i = lax.broadcasted_iota(jnp.int32, (L, L), 0)
    c_i = lax.broadcasted_iota(jnp.int32, (L, L), 1)
    eye = r_i == c_i
    causal = c_i <= r_i

    def to_row(col):
        return jnp.sum(jnp.where(eye, col, 0.0), axis=0, keepdims=True)

    def to_col(row):
        return jnp.sum(jnp.where(eye, row, 0.0), axis=1, keepdims=True)

    for c in range(group):
        rows = slice(c * L, (c + 1) * L)
        gt = gt_ref[rows, :]
        i_col = jnp.sum(jnp.where(lane == h, gt, 0.0), axis=1, keepdims=True)
        f_col = jnp.sum(jnp.where(lane == heads + h, gt, 0.0), axis=1, keepdims=True)
        pos = (g * group + c) * L + lax.broadcasted_iota(jnp.int32, (L, 1), 0)
        pad = pos < PADL
        i_col = jnp.where(pad, NEG, i_col)
        lf_col = jnp.where(pad, 0.0, -(jnp.maximum(-f_col, 0.0) + jnp.log1p(jnp.exp(-jnp.abs(f_col)))))
        b_row = jnp.sum(jnp.where(r_i <= c_i, lf_col, 0.0), axis=0, keepdims=True)
        b_col = to_col(b_row)
        i_row = to_row(i_col)
        gtot = jnp.sum(lf_col, axis=0, keepdims=True)
        m_prev = m_sc[...]
        dmat = jnp.where(causal, b_col - b_row + i_row, NEG)
        inter = b_col + m_prev
        mrow = jnp.maximum(jnp.max(dmat, axis=1, keepdims=True), inter)
        dexp = jnp.exp(dmat - mrow)
        q = q_ref[rows, :]
        k = k_ref[rows, :]
        v = v_ref[rows, :]
        s = lax.dot_general(q, k, NT, preferred_element_type=F32) * dexp
        e_in = jnp.exp(inter - mrow)
        num = (jnp.dot(s.astype(BF16), v, preferred_element_type=F32)
               + e_in * jnp.dot(q, c_sc[...].astype(BF16), preferred_element_type=F32))
        den = (jnp.sum(s, axis=1, keepdims=True)
               + e_in * jnp.sum(q.astype(F32) * n_sc[...], axis=1, keepdims=True))
        hc = num / jnp.maximum(jnp.abs(den), jnp.exp(-mrow))
        a_row = gtot - b_row + i_row
        m_new = jnp.maximum(gtot + m_prev, jnp.max(a_row, axis=1, keepdims=True))
        ea_col = to_col(jnp.exp(a_row - m_new))
        decay = jnp.exp(gtot + m_prev - m_new)
        ks = k.astype(F32) * ea_col
        c_sc[...] = decay * c_sc[...] + lax.dot_general(ks.astype(BF16), v, TN, preferred_element_type=F32)
        n_sc[...] = decay * n_sc[...] + jnp.sum(ks, axis=0, keepdims=True)
        m_sc[...] = m_new
        hn = _rms_rows(hc, ng_ref[...])
        o_ref[rows, :] = (hn * og_ref[rows, :].astype(F32)).astype(BF16)


def _mlstm(qk, z, zs, norm_g, batch, tp, v_blk, og_blk):
    n = qk.shape[0]
    heads = ML_HEADS
    nchunks = tp // CHUNK
    group = _pick(nchunks, (6, 5, 4, 3, 2, 1))
    rows = group * CHUNK
    ng = nchunks // group
    return pl.pallas_call(
        functools.partial(_mlstm_kernel, group=group, heads=heads),
        grid=(batch, heads, ng),
        in_specs=[pl.BlockSpec((rows, ML_QK), lambda b, h, g: (b * ng + g, h)),
                  pl.BlockSpec((rows, ML_QK), lambda b, h, g: (b * ng + g, heads + h)),
                  pl.BlockSpec((rows, ML_V), lambda b, h, g: (b * ng + g, v_blk + h)),
                  pl.BlockSpec((rows, LANES), lambda b, h, g: (b * ng + g, 1)),
                  pl.BlockSpec((rows, ML_V), lambda b, h, g: (b * ng + g, og_blk + h)),
                  pl.BlockSpec((1, ML_V), lambda b, h, g: (0, h))],
        out_specs=pl.BlockSpec((rows, ML_V), lambda b, h, g: (b * ng + g, h)),
        out_shape=jax.ShapeDtypeStruct((n, heads * ML_V), BF16),
        scratch_shapes=[pltpu.VMEM((ML_QK, ML_V), F32), pltpu.VMEM((1, ML_QK), F32), pltpu.VMEM((1, 1), F32)],
        compiler_params=_cp(("parallel", "parallel", "arbitrary")),
        name="mlstm_scan",
    )(qk, qk, z, zs, z, norm_g.reshape(1, -1))


def _merge_kernel(a_ref, hm_ref, wa_ref, wb_ref, ga_ref, gb_ref, o_ref):
    ya = jnp.dot(a_ref[...], wa_ref[...], preferred_element_type=F32)
    yb = jnp.dot(hm_ref[...], wb_ref[...], preferred_element_type=F32)
    o_ref[...] = (ga_ref[...].astype(F32) * ya + gb_ref[...].astype(F32) * yb).astype(BF16)


def _merge(attn, hm, wa, wb, z, ga_off, gb_off):
    n = attn.shape[0]
    d = wa.shape[1]
    tm = _pick(n, (512, 256, 128))
    tn = _pick(d, (1024, 512, 256, 128))
    assert ga_off % tn == 0 and gb_off % tn == 0
    ga_blk, gb_blk = ga_off // tn, gb_off // tn
    return pl.pallas_call(
        _merge_kernel,
        grid=(d // tn, n // tm),
        in_specs=[pl.BlockSpec((tm, attn.shape[1]), lambda j, i: (i, 0)),
                  pl.BlockSpec((tm, hm.shape[1]), lambda j, i: (i, 0)),
                  pl.BlockSpec((wa.shape[0], tn), lambda j, i: (0, j)),
                  pl.BlockSpec((wb.shape[0], tn), lambda j, i: (0, j)),
                  pl.BlockSpec((tm, tn), lambda j, i: (i, ga_blk + j)),
                  pl.BlockSpec((tm, tn), lambda j, i: (i, gb_blk + j))],
        out_specs=pl.BlockSpec((tm, tn), lambda j, i: (i, j)),
        out_shape=jax.ShapeDtypeStruct((n, d), BF16),
        compiler_params=_cp(("parallel", "parallel")),
        name="branch_merge",
    )(attn, hm, wa, wb, z, z)


def _outln_kernel(y_ref, w_ref, h_ref, g_ref, b_ref, of_ref, os_ref):
    o = jnp.dot(y_ref[...], w_ref[...], preferred_element_type=F32)
    r = _ln_rows(ALPHA * h_ref[...] + o, g_ref[...], b_ref[...])
    of_ref[...] = r
    _store_slab(os_ref, r)


def _outproj_ln(y, w, h, g, b):
    n, d = h.shape
    tm = _pick(n, (256, 128))
    ns = d // HEAD_SLAB
    row = pl.BlockSpec((tm, d), lambda i: (i, 0))
    vec = pl.BlockSpec((1, d), lambda i: (0, 0))
    return pl.pallas_call(
        _outln_kernel,
        grid=(n // tm,),
        in_specs=[row, pl.BlockSpec((d, d), lambda i: (0, 0)), row, vec, vec],
        out_specs=[row, pl.BlockSpec((tm, ns, LANES), lambda i: (i, 0, 0))],
        out_shape=[jax.ShapeDtypeStruct((n, d), F32), jax.ShapeDtypeStruct((n, ns, LANES), U32)],
        compiler_params=_cp(("parallel",)),
        name="outproj_ln",
    )(y, w, h, g.reshape(1, d), b.reshape(1, d))


def _router_kernel(x_ref, wh_ref, wl_ref, eb_ref, idx_ref, wt_ref, rank_ref, cnt_ref, base_sc, *, tm):
    i = pl.program_id(0)
    E, G = N_EXPERTS, N_GROUPS
    per = E // G

    @pl.when(i == 0)
    def _():
        base_sc[...] = jnp.zeros(base_sc.shape, F32)

    x = x_ref[...]
    xh = x.astype(BF16)
    xl = (x - xh.astype(F32)).astype(BF16)
    wh, wl = wh_ref[...], wl_ref[...]
    logits = (lax.dot_general(wh, xh, NT, preferred_element_type=F32)
              + lax.dot_general(wh, xl, NT, preferred_element_type=F32)
              + lax.dot_general(wl, xh, NT, preferred_element_type=F32))
    scores = jax.nn.sigmoid(logits)
    biased = scores + eb_ref[...]
    s3 = scores.reshape(G, per, tm)
    b3 = biased.reshape(G, per, tm)
    j_io = lax.broadcasted_iota(jnp.int32, (G, per, tm), 1).astype(F32)
    g_io3 = lax.broadcasted_iota(jnp.int32, (G, per, tm), 0).astype(F32)
    e_io = g_io3 * per + j_io
    g_io = lax.broadcasted_iota(jnp.int32, (G, 1, tm), 0).astype(F32)
    ninf = -jnp.inf

    m1 = jnp.max(b3, axis=1, keepdims=True)
    i1 = jnp.min(jnp.where(b3 == m1, j_io, float(per)), axis=1, keepdims=True)
    m2 = jnp.max(jnp.where(j_io == i1, ninf, b3), axis=1, keepdims=True)
    gs = m1 + m2
    gsel = jnp.zeros((G, 1, tm), F32)
    for _ in range(TOPK_GROUPS):
        gm = jnp.max(gs, axis=0, keepdims=True)
        gi = jnp.min(jnp.where(gs == gm, g_io, float(G)), axis=0, keepdims=True)
        hit = g_io == gi
        gsel = jnp.where(hit, 1.0, gsel)
        gs = jnp.where(hit, ninf, gs)
    masked = jnp.where(gsel > 0.0, b3, ninf)

    def red2(fn, a):
        return fn(fn(a, axis=1, keepdims=True), axis=0, keepdims=True)

    sel = jnp.zeros((G, per, tm), F32)
    idxs, scs = [], []
    for _ in range(TOP_K):
        mx = red2(jnp.max, masked)
        ei = red2(jnp.min, jnp.where(masked == mx, e_io, float(E)))
        hit = e_io == ei
        scs.append(red2(jnp.sum, jnp.where(hit, s3, 0.0)))
        idxs.append(ei)
        sel = jnp.where(hit, 1.0, sel)
        masked = jnp.where(hit, ninf, masked)
    wsum = scs[0]
    for k in range(1, TOP_K):
        wsum = wsum + scs[k]

    sel2 = sel.reshape(E, tm)
    upper = (lax.broadcasted_iota(jnp.int32, (tm, tm), 0) < lax.broadcasted_iota(jnp.int32, (tm, tm), 1))
    excl = jnp.dot(sel2.astype(BF16), upper.astype(BF16), preferred_element_type=F32)
    cnt3 = (excl + base_sc[...]).reshape(G, per, tm)
    for k in range(TOP_K):
        hit = e_io == idxs[k]
        rank = red2(jnp.sum, jnp.where(hit, cnt3, 0.0))
        idx_ref[k:k + 1, :] = idxs[k].reshape(1, tm).astype(jnp.int32)
        wt_ref[k:k + 1, :] = (scs[k] / wsum * ROUTED_SCALE).reshape(1, tm)
        rank_ref[k:k + 1, :] = rank.reshape(1, tm).astype(jnp.int32)
    base_sc[...] = base_sc[...] + jnp.sum(sel2, axis=1, keepdims=True)
    cnt_ref[...] = jnp.broadcast_to(base_sc[...], cnt_ref.shape).astype(jnp.int32)


def _router(h, wr_hi, wr_lo, e_bias):
    n, d = h.shape
    tm = _pick(n, (512, 256, 128))
    outk = pl.BlockSpec((TOP_K, tm), lambda i: (0, i))
    return pl.pallas_call(
        functools.partial(_router_kernel, tm=tm),
        grid=(n // tm,),
        in_specs=[pl.BlockSpec((tm, d), lambda i: (i, 0)),
                  pl.BlockSpec((N_EXPERTS, d), lambda i: (0, 0)),
                  pl.BlockSpec((N_EXPERTS, d), lambda i: (0, 0)),
                  pl.BlockSpec((N_EXPERTS, 1), lambda i: (0, 0))],
        out_specs=[outk, outk, outk, pl.BlockSpec((N_EXPERTS, LANES), lambda i: (0, 0))],
        out_shape=[jax.ShapeDtypeStruct((TOP_K, n), jnp.int32),
                   jax.ShapeDtypeStruct((TOP_K, n), F32),
                   jax.ShapeDtypeStruct((TOP_K, n), jnp.int32),
                   jax.ShapeDtypeStruct((N_EXPERTS, LANES), jnp.int32)],
        scratch_shapes=[pltpu.VMEM((N_EXPERTS, 1), F32)],
        compiler_params=_cp(("arbitrary",)),
        name="moe_router",
    )(h, wr_hi, wr_lo, e_bias.reshape(N_EXPERTS, 1))


def _row_copy(src, dst, sem):
    return pltpu.make_async_copy(src, dst, sem)


TOK_ROWS = 8


def _expert_kernel(be_ref, nu_ref, tok_cur, tok_nxt, h_ref, w1_ref, w3_ref, w2_ref, o_ref,
                   xbuf, w1b, w3b, w2b, sem, *, tm):
    j = pl.program_id(0)
    n_used = nu_ref[0]
    slot = lax.rem(j, 2)

    def gather(tok_ref, blk, dst_slot, wait):
        row = lax.rem(blk, TOK_ROWS)

        def body(r, carry):
            t = tok_ref[row, r]
            cp = _row_copy(h_ref.at[t], xbuf.at[dst_slot, r], sem.at[dst_slot])
            if wait:
                cp.wait()
            else:
                cp.start()
            return carry

        lax.fori_loop(0, tm, body, 0, unroll=8)

    @pl.when(j == 0)
    def _():
        gather(tok_cur, j, slot, False)

    @pl.when(j + 1 < n_used)
    def _():
        gather(tok_nxt, j + 1, 1 - slot, False)

    @pl.when(j < n_used)
    def _():
        @pl.when((j == 0) | (be_ref[j] != be_ref[jnp.maximum(j - 1, 0)]))
        def _():
            w1b[...] = w1_ref[...].astype(BF16)
            w3b[...] = w3_ref[...].astype(BF16)
            w2b[...] = w2_ref[...].astype(BF16)

        gather(tok_cur, j, slot, True)
        parts = []
        for s in range(xbuf.shape[2]):
            lo, hi = _load_slab_pairs(xbuf[slot, :, s, :])
            parts += [lo.astype(BF16), hi.astype(BF16)]
        x = jnp.concatenate(parts, axis=1)
        a = jnp.dot(x, w1b[...], preferred_element_type=F32)
        b = jnp.dot(x, w3b[...], preferred_element_type=F32)
        hb = (_silu(a) * b).astype(BF16)
        _store_slab(o_ref, jnp.dot(hb, w2b[...], preferred_element_type=F32))

    @pl.when(j >= n_used)
    def _():
        o_ref[...] = jnp.zeros(o_ref.shape, U32)


def _experts(hs, tok_of_slot, blk_e, n_used, w1, w3, w2, layer):
    n, ns, _ = hs.shape
    d = ns * HEAD_SLAB
    nb, tm = tok_of_slot.shape
    f = w1.shape[3]
    assert nb % TOK_ROWS == 0
    grid_spec = pltpu.PrefetchScalarGridSpec(
        num_scalar_prefetch=2,
        grid=(nb,),
        in_specs=[pl.BlockSpec((TOK_ROWS, tm), lambda j, be, nu: (j // TOK_ROWS, 0), memory_space=pltpu.SMEM),
                  pl.BlockSpec((TOK_ROWS, tm), lambda j, be, nu: (jnp.minimum(j + 1, nb - 1) // TOK_ROWS, 0),
                               memory_space=pltpu.SMEM),
                  pl.BlockSpec(memory_space=pl.ANY),
                  pl.BlockSpec((None, None, d, f), lambda j, be, nu: (layer, be[j], 0, 0)),
                  pl.BlockSpec((None, None, d, f), lambda j, be, nu: (layer, be[j], 0, 0)),
                  pl.BlockSpec((None, None, f, d), lambda j, be, nu: (layer, be[j], 0, 0))],
        out_specs=pl.BlockSpec((tm, ns, LANES), lambda j, be, nu: (j, 0, 0)),
        scratch_shapes=[pltpu.VMEM((2, tm, ns, LANES), U32),
                        pltpu.VMEM((d, f), BF16), pltpu.VMEM((d, f), BF16), pltpu.VMEM((f, d), BF16),
                        pltpu.SemaphoreType.DMA((2,))],
    )
    return pl.pallas_call(
        functools.partial(_expert_kernel, tm=tm),
        grid_spec=grid_spec,
        out_shape=jax.ShapeDtypeStruct((nb * tm, ns, LANES), U32),
        compiler_params=_cp(("arbitrary",)),
        name="moe_experts",
    )(blk_e, n_used, tok_of_slot, tok_of_slot, hs, w1, w3, w2)


def _combine_kernel(dest_ref, h_ref, wt_ref, ws1_ref, ws3_ref, ws2_ref, g_ref, b_ref, ys_ref,
                    of_ref, *rest, tm):
    ob_ref = rest[0] if len(rest) == 3 else None
    gbuf, sem = rest[-2:]

    def start(r, carry):
        for k in range(TOP_K):
            d = dest_ref[k, r]
            _row_copy(ys_ref.at[d], gbuf.at[k, r], sem.at[0]).start()
        return carry

    def wait(r, carry):
        for k in range(TOP_K):
            d = dest_ref[k, r]
            _row_copy(ys_ref.at[d], gbuf.at[k, r], sem.at[0]).wait()
        return carry

    lax.fori_loop(0, tm, start, 0)
    h = h_ref[...]
    xb = h.astype(BF16)
    a = jnp.dot(xb, ws1_ref[...], preferred_element_type=F32)
    b = jnp.dot(xb, ws3_ref[...], preferred_element_type=F32)
    shared = jnp.dot((_silu(a) * b).astype(BF16), ws2_ref[...], preferred_element_type=F32)
    lax.fori_loop(0, tm, wait, 0)
    wt = wt_ref[...]
    parts = []
    for s in range(gbuf.shape[2]):
        lo = hi = None
        for k in range(TOP_K):
            plo, phi = _load_slab_pairs(gbuf[k, :, s, :])
            w = wt[:, k:k + 1]
            lo = w * plo if lo is None else lo + w * plo
            hi = w * phi if hi is None else hi + w * phi
        parts += [lo, hi]
    acc = shared + jnp.concatenate(parts, axis=1)
    r = _ln_rows(ALPHA * h + acc, g_ref[...], b_ref[...])
    of_ref[...] = r
    if ob_ref is not None:
        ob_ref[...] = r.astype(BF16)


def _combine(h, dest, wt_tok, ws1, ws3, ws2, g, b, ys, final_shape=None):
    n, d = h.shape
    f = ws1.shape[1]
    tm = LANES
    row = pl.BlockSpec((tm, d), lambda i: (i, 0))
    vec = pl.BlockSpec((1, d), lambda i: (0, 0))
    if final_shape is None:
        out_specs = [row, row]
        out_shape = [jax.ShapeDtypeStruct((n, d), F32), jax.ShapeDtypeStruct((n, d), BF16)]
    else:
        batch, seq = final_shape
        nt = seq // tm + 1
        out_specs = [pl.BlockSpec((None, tm, d), lambda i: (i // nt, jnp.maximum(i % nt - 1, 0), 0))]
        out_shape = [jax.ShapeDtypeStruct((batch, seq, d), F32)]
    return pl.pallas_call(
        functools.partial(_combine_kernel, tm=tm),
        grid=(n // tm,),
        in_specs=[pl.BlockSpec((TOP_K, tm), lambda i: (0, i), memory_space=pltpu.SMEM),
                  row,
                  pl.BlockSpec((tm, TOP_K), lambda i: (i, 0)),
                  pl.BlockSpec((d, f), lambda i: (0, 0)),
                  pl.BlockSpec((d, f), lambda i: (0, 0)),
                  pl.BlockSpec((f, d), lambda i: (0, 0)),
                  vec, vec,
                  pl.BlockSpec(memory_space=pl.ANY)],
        out_specs=out_specs,
        out_shape=out_shape,
        scratch_shapes=[pltpu.VMEM((TOP_K, tm) + ys.shape[1:], U32), pltpu.SemaphoreType.DMA((1,))],
        compiler_params=_cp(("arbitrary",)),
        name="moe_combine",
    )(dest, h, wt_tok, ws1, ws3, ws2, g.reshape(1, d), b.reshape(1, d), ys)


def _moe(h_f32, hs, w_router, e_bias, w1, w3, w2, layer, ws1, ws3, ws2, g, b, final_shape=None):
    n = h_f32.shape[0]
    wr_t = w_router.T
    wr_hi = wr_t.astype(BF16)
    wr_lo = (wr_t - wr_hi.astype(F32)).astype(BF16)
    idx, wts, rank, cnt = _router(h_f32, wr_hi, wr_lo, e_bias)
    counts = cnt[:, 0]
    blk = EXPERT_ROWS
    pcounts = (counts + blk - 1) // blk * blk
    pends = jnp.cumsum(pcounts)
    pstarts = pends - pcounts
    e_ids = jnp.arange(N_EXPERTS, dtype=jnp.int32)
    start_of = jnp.sum(jnp.where(idx[None] == e_ids[:, None, None], pstarts[:, None, None], 0), axis=0)
    dest = (start_of + rank).astype(jnp.int32)
    nb = n * TOP_K // blk + N_EXPERTS
    blk_e = jnp.minimum(jnp.sum(pends[None, :] <= (jnp.arange(nb, dtype=jnp.int32) * blk)[:, None], axis=1),
                        N_EXPERTS - 1).astype(jnp.int32)
    n_used = (pends[-1:] // blk).astype(jnp.int32)
    tok = jnp.broadcast_to(jnp.arange(n, dtype=jnp.int32)[None], (TOP_K, n))
    tok_of_slot = jnp.zeros((nb * blk,), jnp.int32).at[dest.reshape(-1)].set(
        tok.reshape(-1), unique_indices=True).reshape(nb, blk)
    ys = _experts(hs, tok_of_slot, blk_e, n_used, w1, w3, w2, layer)
    return _combine(h_f32, dest, wts.T, ws1.astype(BF16), ws3.astype(BF16), ws2.astype(BF16), g, b, ys,
                    final_shape)


def _in_proj_layout(w_in, b_in, d, ql, kvl):
    hq, hv = ML_HEADS * ML_QK, ML_HEADS * ML_V
    sizes = (ql, kvl, QK_ROPE, hq, hq, hv, hv, ML_HEADS, ML_HEADS, d, d)
    offs = np.concatenate([[0], np.cumsum(sizes)])
    seg = lambda a, i: a[..., int(offs[i]):int(offs[i + 1])]
    order = (0, 1, 3, 4, 5, 6, 9, 10)
    w_main = jnp.concatenate([seg(w_in, i) for i in order], axis=-1)
    b_main = jnp.concatenate([seg(b_in, i) for i in order], axis=-1)
    half = QK_ROPE // 2

    def small(a):
        kr = seg(a, 2)
        x1, x2 = kr[..., :half], kr[..., half:]
        pad = jnp.zeros(a.shape[:-1] + (LANES - 2 * ML_HEADS,), a.dtype)
        return jnp.concatenate([x1, x2, x2, x1, seg(a, 7), seg(a, 8), pad], axis=-1)

    main_offs = np.concatenate([[0], np.cumsum([sizes[i] for i in order])])
    return w_main, b_main, small(w_in), small(b_in), [int(o) for o in main_offs]


def _uq_layout(w_uq):
    kq = w_uq.shape[0]
    w = w_uq.reshape(kq, MLA_HEADS, QK_NOPE + QK_ROPE)
    half = QK_ROPE // 2
    nope, x1, x2 = w[..., :QK_NOPE], w[..., QK_NOPE:QK_NOPE + half], w[..., QK_NOPE + half:]
    return jnp.concatenate([nope, x1, x2, x2, x1], axis=-1).reshape(kq, MLA_HEADS * HEAD_SLAB)


def _ukv_layout(w_ukv):
    kk = w_ukv.shape[0]
    w = w_ukv.reshape(kk, MLA_HEADS, QK_NOPE + V_HEAD)
    return jnp.concatenate([w[..., :QK_NOPE].reshape(kk, -1), w[..., QK_NOPE:].reshape(kk, -1)], axis=-1)


def _rope_tables(tp):
    half = QK_ROPE // 2
    pos = jnp.arange(tp, dtype=F32) - PADL
    inv_freq = 1.0 / (ROPE_THETA ** (jnp.arange(0, QK_ROPE, 2, dtype=F32) / QK_ROPE))
    ang = pos[:, None] * inv_freq[None, :]
    cos, sin = jnp.cos(ang), jnp.sin(ang)
    zero = jnp.zeros((tp, LANES - 2 * half), F32)
    return jnp.concatenate([cos, cos, zero], axis=1), jnp.concatenate([-sin, sin, zero], axis=1)


def kernel(x, meta, ln_in_g, ln_in_b, w_in, b_in, q_norm_g, kv_norm_g, w_uq, w_ukv, conv_w, conv_b, ml_norm_g, w_br_mla, w_br_mlstm, w_out, ln1_g, ln1_b, w_router, e_bias, w1, w3, w2, ws1, ws3, ws2, ln2_g, ln2_b):
    batch, seq, d = x.shape
    depth = w_in.shape[0]
    ql, kvl = q_norm_g.shape[1], kv_norm_g.shape[1]
    assert ql == kvl and seq % LANES == 0
    tp = LANES + seq
    n = batch * tp
    hq = ML_HEADS * ML_QK

    head = jnp.concatenate([jnp.zeros((PADL, d), x.dtype), meta.astype(x.dtype)], axis=0)
    h_f32, h_bf = _layer_norm_in(x, head, ln_in_g, ln_in_b)
    cos_t, sin_t = _rope_tables(tp)
    col_scale = jnp.concatenate([jnp.ones((1, hq), F32), jnp.full((1, hq), ML_QK ** -0.5, F32)], axis=1)

    for l in range(depth):
        w_main, b_main, w_small, b_small, offs = _in_proj_layout(w_in[l], b_in[l], d, ql, kvl)
        o_mq, o_mv, o_mo, o_ga, o_gb = offs[2], offs[4], offs[5], offs[6], offs[7]
        z = _matmul_bias(h_bf, w_main.astype(BF16), b_main, BF16, sig_col=o_mo, name="in_proj")
        zs = _matmul_bias(h_bf, w_small.astype(BF16), b_small, F32, name="in_proj_small")

        q, k, v = _mla_qkv(z, zs, q_norm_g[l], kv_norm_g[l], _uq_layout(w_uq[l]).astype(BF16),
                           _ukv_layout(w_ukv[l]).astype(BF16), cos_t, sin_t, tp, ql)
        attn = _attention(q, k, v, batch, tp)

        assert o_mq % hq == 0 and o_mv % ML_V == 0 and o_mo % ML_V == 0
        qk = _qk_conv(z, conv_w[l], conv_b[l], col_scale, tp, o_mq // hq)
        hm = _mlstm(qk, z, zs, ml_norm_g[l], batch, tp, o_mv // ML_V, o_mo // ML_V)

        y = _merge(attn, hm, w_br_mla[l].astype(BF16), w_br_mlstm[l].astype(BF16), z, o_ga, o_gb)
        h_f32, hs = _outproj_ln(y, w_out[l].astype(BF16), h_f32, ln1_g[l], ln1_b[l])
        outs = _moe(h_f32, hs, w_router[l], e_bias[l], w1, w3, w2, l, ws1[l], ws3[l], ws2[l],
                    ln2_g[l], ln2_b[l], final_shape=(batch, seq) if l == depth - 1 else None)
        if l == depth - 1:
            return outs[0]
        h_f32, h_bf = outs
```

```python
import functools

import numpy as np
import jax
import jax.numpy as jnp
from jax import lax
from jax.experimental import pallas as pl
from jax.experimental.pallas import tpu as pltpu

N_META = 16
MLA_HEADS = 8
QK_NOPE = 128
QK_ROPE = 64
V_HEAD = 128
ROPE_THETA = 10000.0
ML_HEADS = 8
ML_QK = 128
ML_V = 256
CONV_K = 4
CHUNK = 64
N_EXPERTS = 64
TOP_K = 8
N_GROUPS = 8
TOPK_GROUPS = 4
ROUTED_SCALE = 2.5
DEPTH = 2
ALPHA = (2 * DEPTH) ** 0.25
LN_EPS = 1e-5
RMS_EPS = 1e-6
NEG = -1e30

LANES = 128
HEAD_SLAB = 2 * LANES
PADL = LANES - N_META
EXPERT_ROWS = 256
VMEM_LIMIT = 56 * 1024 * 1024

F32 = jnp.float32
BF16 = jnp.bfloat16
U32 = jnp.uint32
NT = (((1,), (1,)), ((), ()))
TN = (((0,), (0,)), ((), ()))


def _pick(n, cands):
    for c in cands:
        if n % c == 0:
            return c
    raise ValueError(f"no tile in {cands} divides {n}")


def _cp(sem, vmem=None):
    return pltpu.CompilerParams(dimension_semantics=sem, vmem_limit_bytes=vmem or VMEM_LIMIT)


def _ln_rows(x, g, b):
    mu = jnp.mean(x, axis=-1, keepdims=True)
    xc = x - mu
    var = jnp.mean(xc * xc, axis=-1, keepdims=True)
    return xc * lax.rsqrt(var + LN_EPS) * g + b


def _rms_rows(x, g):
    return x * lax.rsqrt(jnp.mean(x * x, axis=-1, keepdims=True) + RMS_EPS) * g


def _silu(x):
    return x * jax.nn.sigmoid(x)


def _store_slab(ref, x):
    rows, ns = x.shape[0], x.shape[1] // HEAD_SLAB
    for s in range(ns):
        lo = lax.bitcast_convert_type(x[:, s * HEAD_SLAB:s * HEAD_SLAB + LANES].astype(BF16).astype(F32), U32)
        hi = lax.bitcast_convert_type(x[:, s * HEAD_SLAB + LANES:(s + 1) * HEAD_SLAB].astype(BF16).astype(F32), U32)
        ref[pl.ds(s, rows, stride=ns), :] = (lo >> 16) | hi


def _load_slab_pairs(ref, lead, s, rows, ns):
    u = ref[lead + (pl.ds(s, rows, stride=ns), slice(None))]
    return (lax.bitcast_convert_type(u << 16, F32),
            lax.bitcast_convert_type(u & jnp.uint32(0xFFFF0000), F32))


def _ln_in_kernel(head_ref, x_ref, g_ref, b_ref, of_ref, ob_ref):
    i = pl.program_id(1)

    def emit(src):
        y = _ln_rows(src, g_ref[...], b_ref[...])
        of_ref[...] = y
        ob_ref[...] = y.astype(BF16)

    @pl.when(i == 0)
    def _():
        emit(head_ref[...])

    @pl.when(i > 0)
    def _():
        emit(x_ref[...])


def _layer_norm_in(x, head, g, b):
    batch, seq, d = x.shape
    nt = seq // LANES + 1
    row = pl.BlockSpec((LANES, d), lambda bb, i: (bb * nt + i, 0))
    vec = pl.BlockSpec((1, d), lambda bb, i: (0, 0))
    n = batch * nt * LANES
    return pl.pallas_call(
        _ln_in_kernel,
        grid=(batch, nt),
        in_specs=[pl.BlockSpec((LANES, d), lambda bb, i: (0, 0)),
                  pl.BlockSpec((None, LANES, d), lambda bb, i: (bb, jnp.maximum(i - 1, 0), 0)),
                  vec, vec],
        out_specs=[row, row],
        out_shape=[jax.ShapeDtypeStruct((n, d), F32), jax.ShapeDtypeStruct((n, d), BF16)],
        compiler_params=_cp(("parallel", "arbitrary")),
        name="ln_in",
    )(head, x, g.reshape(1, d), b.reshape(1, d))


def _mm_kernel(x_ref, w_ref, b_ref, o_ref, *, sig_tile):
    acc = jnp.dot(x_ref[...], w_ref[...], preferred_element_type=F32) + b_ref[...]
    if sig_tile is None:
        o_ref[...] = acc.astype(o_ref.dtype)
        return
    j = pl.program_id(0)

    @pl.when(j < sig_tile)
    def _():
        o_ref[...] = acc.astype(o_ref.dtype)

    @pl.when(j >= sig_tile)
    def _():
        o_ref[...] = jax.nn.sigmoid(acc).astype(o_ref.dtype)


def _matmul_bias(x, w, b, out_dtype, sig_col=None, name="mm"):
    n, kd = x.shape
    nc = w.shape[1]
    tm = _pick(n, (1024, 512, 256, 128))
    tn = _pick(nc, (1024, 512, 256, 128))
    sig_tile = None
    if sig_col is not None:
        assert sig_col % tn == 0
        sig_tile = sig_col // tn
    return pl.pallas_call(
        functools.partial(_mm_kernel, sig_tile=sig_tile),
        grid=(nc // tn, n // tm),
        in_specs=[pl.BlockSpec((tm, kd), lambda j, i: (i, 0)),
                  pl.BlockSpec((kd, tn), lambda j, i: (0, j)),
                  pl.BlockSpec((1, tn), lambda j, i: (0, j))],
        out_specs=pl.BlockSpec((tm, tn), lambda j, i: (i, j)),
        out_shape=jax.ShapeDtypeStruct((n, nc), out_dtype),
        compiler_params=_cp(("parallel", "parallel")),
        name=name,
    )(x, w, b.reshape(1, nc))


def _rope_slab(r, cos_t, sin_t):
    return r * cos_t + pltpu.roll(r, 2 * (QK_ROPE // 2), 1) * sin_t


def _qproj_kernel(c_ref, g_ref, w_ref, cos_ref, sin_ref, o_ref, *, heads, scale):
    xn = _rms_rows(c_ref[...].astype(F32), g_ref[...])
    q = jnp.dot(xn.astype(BF16), w_ref[...], preferred_element_type=F32) * scale
    cos_t, sin_t = cos_ref[...], sin_ref[...]
    for h in range(heads):
        lo = h * HEAD_SLAB
        o_ref[:, lo:lo + LANES] = q[:, lo:lo + LANES].astype(BF16)
        o_ref[:, lo + LANES:lo + HEAD_SLAB] = _rope_slab(q[:, lo + LANES:lo + HEAD_SLAB], cos_t, sin_t).astype(BF16)


def _kproj_kernel(c_ref, g_ref, w_ref, kr_ref, cos_ref, sin_ref, k_ref, v_ref, *, heads):
    xn = _rms_rows(c_ref[...].astype(F32), g_ref[...])
    kv = jnp.dot(xn.astype(BF16), w_ref[...], preferred_element_type=F32)
    rr = _rope_slab(kr_ref[...], cos_ref[...], sin_ref[...]).astype(BF16)
    for h in range(heads):
        lo = h * HEAD_SLAB
        k_ref[:, lo:lo + LANES] = kv[:, h * LANES:(h + 1) * LANES].astype(BF16)
        k_ref[:, lo + LANES:lo + HEAD_SLAB] = rr
    v_ref[...] = kv[:, heads * LANES:].astype(BF16)


def _mla_qkv(z, zs, qg, kvg, wq, wkv, cos_t, sin_t, tp, ql):
    n = z.shape[0]
    heads = MLA_HEADS
    tm = _pick(tp, (384, 256, 128))
    nt = tp // tm
    scale = (QK_NOPE + QK_ROPE) ** -0.5
    tab = pl.BlockSpec((tm, LANES), lambda i: (i % nt, 0))
    q = pl.pallas_call(
        functools.partial(_qproj_kernel, heads=heads, scale=scale),
        grid=(n // tm,),
        in_specs=[pl.BlockSpec((tm, ql), lambda i: (i, 0)),
                  pl.BlockSpec((1, ql), lambda i: (0, 0)),
                  pl.BlockSpec((ql, heads * HEAD_SLAB), lambda i: (0, 0)),
                  tab, tab],
        out_specs=pl.BlockSpec((tm, heads * HEAD_SLAB), lambda i: (i, 0)),
        out_shape=jax.ShapeDtypeStruct((n, heads * HEAD_SLAB), BF16),
        compiler_params=_cp(("parallel",)),
        name="mla_qproj",
    )(z, qg.reshape(1, ql), wq, cos_t, sin_t)
    k, v = pl.pallas_call(
        functools.partial(_kproj_kernel, heads=heads),
        grid=(n // tm,),
        in_specs=[pl.BlockSpec((tm, ql), lambda i: (i, 1)),
                  pl.BlockSpec((1, ql), lambda i: (0, 0)),
                  pl.BlockSpec((ql, 2 * heads * LANES), lambda i: (0, 0)),
                  pl.BlockSpec((tm, LANES), lambda i: (i, 0)),
                  tab, tab],
        out_specs=[pl.BlockSpec((tm, heads * HEAD_SLAB), lambda i: (i, 0)),
                   pl.BlockSpec((tm, heads * LANES), lambda i: (i, 0))],
        out_shape=[jax.ShapeDtypeStruct((n, heads * HEAD_SLAB), BF16),
                   jax.ShapeDtypeStruct((n, heads * LANES), BF16)],
        compiler_params=_cp(("parallel",)),
        name="mla_kvproj",
    )(z, kvg.reshape(1, ql), wkv, zs, cos_t, sin_t)
    return q, k, v


ATTN_HEADS_PER_STEP = 2


def _attn_kernel(q_ref, k_ref, v_ref, o_ref, m_sc, l_sc, acc_sc, *, t, hps):
    i = pl.program_id(2)
    m_sc[...] = jnp.full(m_sc.shape, NEG, F32)
    l_sc[...] = jnp.zeros(l_sc.shape, F32)
    acc_sc[...] = jnp.zeros(acc_sc.shape, F32)

    def step(j, masked):
        rows = pl.ds(pl.multiple_of(j * t, t), t)
        if masked:
            qpos = i * t + lax.broadcasted_iota(jnp.int32, (t, t), 0)
            kpos = j * t + lax.broadcasted_iota(jnp.int32, (t, t), 1)
            keep = (kpos <= qpos) & (kpos >= PADL)
        for h in range(hps):
            q = q_ref[:, h * HEAD_SLAB:(h + 1) * HEAD_SLAB]
            s = lax.dot_general(q, k_ref[rows, h * HEAD_SLAB:(h + 1) * HEAD_SLAB], NT, preferred_element_type=F32)
            if masked:
                s = jnp.where(keep, s, NEG)
            m_prev = m_sc[h]
            m_new = jnp.maximum(m_prev, jnp.max(s, axis=1, keepdims=True))
            alpha = jnp.exp(m_prev - m_new)
            p = jnp.exp(s - m_new)
            l_sc[h] = alpha * l_sc[h] + jnp.sum(p, axis=1, keepdims=True)
            acc_sc[h] = alpha * acc_sc[h] + jnp.dot(p.astype(BF16), v_ref[rows, h * V_HEAD:(h + 1) * V_HEAD],
                                                    preferred_element_type=F32)
            m_sc[h] = m_new

    step(0, True)

    def body(j, carry):
        step(j, False)
        return carry

    lax.fori_loop(1, i, body, 0)

    @pl.when(i > 0)
    def _():
        step(i, True)

    for h in range(hps):
        o_ref[:, h * V_HEAD:(h + 1) * V_HEAD] = (acc_sc[h] / l_sc[h]).astype(o_ref.dtype)


def _attention(q, k, v, batch, tp):
    n = q.shape[0]
    heads = MLA_HEADS
    hps = ATTN_HEADS_PER_STEP
    assert heads % hps == 0
    t = _pick(tp, (384, 256, 128))
    nt = tp // t
    return pl.pallas_call(
        functools.partial(_attn_kernel, t=t, hps=hps),
        grid=(batch, heads // hps, nt),
        in_specs=[pl.BlockSpec((t, hps * HEAD_SLAB), lambda b, h, i: (b * nt + i, h)),
                  pl.BlockSpec((tp, hps * HEAD_SLAB), lambda b, h, i: (b, h)),
                  pl.BlockSpec((tp, hps * V_HEAD), lambda b, h, i: (b, h))],
        out_specs=pl.BlockSpec((t, hps * V_HEAD), lambda b, h, i: (b * nt + i, h)),
        out_shape=jax.ShapeDtypeStruct((n, heads * V_HEAD), BF16),
        scratch_shapes=[pltpu.VMEM((hps, t, 1), F32), pltpu.VMEM((hps, t, 1), F32),
                        pltpu.VMEM((hps, t, V_HEAD), F32)],
        compiler_params=_cp(("parallel", "parallel", "parallel")),
        name="mla_attention",
    )(q, k, v)


HALO = 16


def _conv_kernel(x_ref, halo_ref, w_ref, b_ref, s_ref, o_ref, *, tm, tp):
    i = pl.program_id(0)
    start = lax.rem(i * tm, tp)
    pos = start + lax.broadcasted_iota(jnp.int32, (tm, 1), 0)
    x = jnp.where(pos >= PADL, x_ref[...].astype(F32), 0.0)
    hpos = start - HALO + lax.broadcasted_iota(jnp.int32, (HALO, 1), 0)
    halo = jnp.where(hpos >= PADL, halo_ref[...].astype(F32), 0.0)
    ext = jnp.concatenate([halo, x], axis=0)
    w = w_ref[...]
    y = b_ref[...] + w[CONV_K - 1:CONV_K, :] * x
    for j in range(1, CONV_K):
        y = y + w[CONV_K - 1 - j:CONV_K - j, :] * pltpu.roll(ext, j, 0)[HALO:, :]
    o_ref[...] = (_silu(y) * s_ref[...]).astype(BF16)


def _qk_conv(z, conv_w, conv_b, col_scale, tp, off_blk):
    n = z.shape[0]
    cw = conv_w.shape[1] // 2
    tm = _pick(tp, (384, 256, 128))
    return pl.pallas_call(
        functools.partial(_conv_kernel, tm=tm, tp=tp),
        grid=(n // tm, 2),
        in_specs=[pl.BlockSpec((tm, cw), lambda i, c: (i, off_blk + c)),
                  pl.BlockSpec((HALO, cw), lambda i, c: (jnp.maximum(i * (tm // HALO) - 1, 0), off_blk + c)),
                  pl.BlockSpec((CONV_K, cw), lambda i, c: (0, c)),
                  pl.BlockSpec((1, cw), lambda i, c: (0, c)),
                  pl.BlockSpec((1, cw), lambda i, c: (0, c))],
        out_specs=pl.BlockSpec((tm, cw), lambda i, c: (i, c)),
        out_shape=jax.ShapeDtypeStruct((n, 2 * cw), BF16),
        compiler_params=_cp(("parallel", "parallel")),
        name="mlstm_qk_conv",
    )(z, z, conv_w, conv_b.reshape(1, -1), col_scale)


def _mlstm_kernel(q_ref, k_ref, v_ref, gt_ref, og_ref, ng_ref, o_ref, c_sc, n_sc, m_sc, *, group, heads):
    h = pl.program_id(1)
    g = pl.program_id(2)
    L = CHUNK

    @pl.when(g == 0)
    def _():
        c_sc[...] = jnp.zeros(c_sc.shape, F32)
        n_sc[...] = jnp.zeros(n_sc.shape, F32)
        m_sc[...] = jnp.zeros(m_sc.shape, F32)

    lane = lax.broadcasted_iota(jnp.int32, (L, LANES), 1)
    r_i = lax.broadcasted_iota(jnp.int32, (L, L), 0)
    c_i = lax.broadcasted_iota(jnp.int32, (L, L), 1)
    eye = r_i == c_i
    causal = c_i <= r_i

    def to_row(col):
        return jnp.sum(jnp.where(eye, col, 0.0), axis=0, keepdims=True)

    def to_col(row):
        return jnp.sum(jnp.where(eye, row, 0.0), axis=1, keepdims=True)

    for c in range(group):
        rows = slice(c * L, (c + 1) * L)
        gt = gt_ref[rows, :]
        i_col = jnp.sum(jnp.where(lane == h, gt, 0.0), axis=1, keepdims=True)
        f_col = jnp.sum(jnp.where(lane == heads + h, gt, 0.0), axis=1, keepdims=True)
        pos = (g * group + c) * L + lax.broadcasted_iota(jnp.int32, (L, 1), 0)
        pad = pos < PADL
        i_col = jnp.where(pad, NEG, i_col)
        lf_col = jnp.where(pad, 0.0, -(jnp.maximum(-f_col, 0.0) + jnp.log1p(jnp.exp(-jnp.abs(f_col)))))
        b_row = jnp.sum(jnp.where(r_i <= c_i, lf_col, 0.0), axis=0, keepdims=True)
        b_col = to_col(b_row)
        i_row = to_row(i_col)
        gtot = jnp.sum(lf_col, axis=0, keepdims=True)
        m_prev = m_sc[...]
        dmat = jnp.where(causal, b_col - b_row + i_row, NEG)
        inter = b_col + m_prev
        mrow = jnp.maximum(jnp.max(dmat, axis=1, keepdims=True), inter)
        dexp = jnp.exp(dmat - mrow)
        q = q_ref[rows, :]
        k = k_ref[rows, :]
        v = v_ref[rows, :]
        s = lax.dot_general(q, k, NT, preferred_element_type=F32) * dexp
        e_in = jnp.exp(inter - mrow)
        num = (jnp.dot(s.astype(BF16), v, preferred_element_type=F32)
               + e_in * jnp.dot(q, c_sc[...].astype(BF16), preferred_element_type=F32))
        den = (jnp.sum(s, axis=1, keepdims=True)
               + e_in * jnp.sum(q.astype(F32) * n_sc[...], axis=1, keepdims=True))
        hc = num / jnp.maximum(jnp.abs(den), jnp.exp(-mrow))
        a_row = gtot - b_row + i_row
        m_new = jnp.maximum(gtot + m_prev, jnp.max(a_row, axis=1, keepdims=True))
        ea_col = to_col(jnp.exp(a_row - m_new))
        decay = jnp.exp(gtot + m_prev - m_new)
        ks = k.astype(F32) * ea_col
        c_sc[...] = decay * c_sc[...] + lax.dot_general(ks.astype(BF16), v, TN, preferred_element_type=F32)
        n_sc[...] = decay * n_sc[...] + jnp.sum(ks, axis=0, keepdims=True)
        m_sc[...] = m_new
        hn = _rms_rows(hc, ng_ref[...])
        o_ref[rows, :] = (hn * og_ref[rows, :].astype(F32)).astype(BF16)


def _mlstm(qk, z, zs, norm_g, batch, tp, v_blk, og_blk):
    n = qk.shape[0]
    heads = ML_HEADS
    nchunks = tp // CHUNK
    group = _pick(nchunks, (6, 5, 4, 3, 2, 1))
    rows = group * CHUNK
    ng = nchunks // group
    return pl.pallas_call(
        functools.partial(_mlstm_kernel, group=group, heads=heads),
        grid=(batch, heads, ng),
        in_specs=[pl.BlockSpec((rows, ML_QK), lambda b, h, g: (b * ng + g, h)),
                  pl.BlockSpec((rows, ML_QK), lambda b, h, g: (b * ng + g, heads + h)),
                  pl.BlockSpec((rows, ML_V), lambda b, h, g: (b * ng + g, v_blk + h)),
                  pl.BlockSpec((rows, LANES), lambda b, h, g: (b * ng + g, 1)),
                  pl.BlockSpec((rows, ML_V), lambda b, h, g: (b * ng + g, og_blk + h)),
                  pl.BlockSpec((1, ML_V), lambda b, h, g: (0, h))],
        out_specs=pl.BlockSpec((rows, ML_V), lambda b, h, g: (b * ng + g, h)),
        out_shape=jax.ShapeDtypeStruct((n, heads * ML_V), BF16),
        scratch_shapes=[pltpu.VMEM((ML_QK, ML_V), F32), pltpu.VMEM((1, ML_QK), F32), pltpu.VMEM((1, 1), F32)],
        compiler_params=_cp(("parallel", "parallel", "arbitrary")),
        name="mlstm_scan",
    )(qk, qk, z, zs, z, norm_g.reshape(1, -1))


def _merge_kernel(a_ref, hm_ref, wa_ref, wb_ref, ga_ref, gb_ref, o_ref):
    ya = jnp.dot(a_ref[...], wa_ref[...], preferred_element_type=F32)
    yb = jnp.dot(hm_ref[...], wb_ref[...], preferred_element_type=F32)
    o_ref[...] = (ga_ref[...].astype(F32) * ya + gb_ref[...].astype(F32) * yb).astype(BF16)


def _merge(attn, hm, wa, wb, z, ga_off, gb_off):
    n = attn.shape[0]
    d = wa.shape[1]
    tm = _pick(n, (512, 256, 128))
    tn = _pick(d, (1024, 512, 256, 128))
    assert ga_off % tn == 0 and gb_off % tn == 0
    ga_blk, gb_blk = ga_off // tn, gb_off // tn
    return pl.pallas_call(
        _merge_kernel,
        grid=(d // tn, n // tm),
        in_specs=[pl.BlockSpec((tm, attn.shape[1]), lambda j, i: (i, 0)),
                  pl.BlockSpec((tm, hm.shape[1]), lambda j, i: (i, 0)),
                  pl.BlockSpec((wa.shape[0], tn), lambda j, i: (0, j)),
                  pl.BlockSpec((wb.shape[0], tn), lambda j, i: (0, j)),
                  pl.BlockSpec((tm, tn), lambda j, i: (i, ga_blk + j)),
                  pl.BlockSpec((tm, tn), lambda j, i: (i, gb_blk + j))],
        out_specs=pl.BlockSpec((tm, tn), lambda j, i: (i, j)),
        out_shape=jax.ShapeDtypeStruct((n, d), BF16),
        compiler_params=_cp(("parallel", "parallel")),
        name="branch_merge",
    )(attn, hm, wa, wb, z, z)


def _outln_kernel(y_ref, w_ref, h_ref, g_ref, b_ref, of_ref, os_ref):
    o = jnp.dot(y_ref[...], w_ref[...], preferred_element_type=F32)
    r = _ln_rows(ALPHA * h_ref[...] + o, g_ref[...], b_ref[...])
    of_ref[...] = r
    _store_slab(os_ref, r)


def _outproj_ln(y, w, h, g, b):
    n, d = h.shape
    tm = _pick(n, (256, 128))
    ns = d // HEAD_SLAB
    row = pl.BlockSpec((tm, d), lambda i: (i, 0))
    vec = pl.BlockSpec((1, d), lambda i: (0, 0))
    return pl.pallas_call(
        _outln_kernel,
        grid=(n // tm,),
        in_specs=[row, pl.BlockSpec((d, d), lambda i: (0, 0)), row, vec, vec],
        out_specs=[row, pl.BlockSpec((tm * ns, LANES), lambda i: (i, 0))],
        out_shape=[jax.ShapeDtypeStruct((n, d), F32), jax.ShapeDtypeStruct((n * ns, LANES), U32)],
        compiler_params=_cp(("parallel",)),
        name="outproj_ln",
    )(y, w, h, g.reshape(1, d), b.reshape(1, d))


def _router_kernel(x_ref, wh_ref, wl_ref, eb_ref, idx_ref, wt_ref, rank_ref, cnt_ref, base_sc, *, tm):
    i = pl.program_id(0)
    E, G = N_EXPERTS, N_GROUPS
    per = E // G

    @pl.when(i == 0)
    def _():
        base_sc[...] = jnp.zeros(base_sc.shape, F32)

    x = x_ref[...]
    xh = x.astype(BF16)
    xl = (x - xh.astype(F32)).astype(BF16)
    wh, wl = wh_ref[...], wl_ref[...]
    logits = (lax.dot_general(wh, xh, NT, preferred_element_type=F32)
              + lax.dot_general(wh, xl, NT, preferred_element_type=F32)
              + lax.dot_general(wl, xh, NT, preferred_element_type=F32))
    scores = jax.nn.sigmoid(logits)
    biased = scores + eb_ref[...]
    s3 = scores.reshape(G, per, tm)
    b3 = biased.reshape(G, per, tm)
    j_io = lax.broadcasted_iota(jnp.int32, (G, per, tm), 1).astype(F32)
    g_io3 = lax.broadcasted_iota(jnp.int32, (G, per, tm), 0).astype(F32)
    e_io = g_io3 * per + j_io
    g_io = lax.broadcasted_iota(jnp.int32, (G, 1, tm), 0).astype(F32)
    ninf = -jnp.inf

    m1 = jnp.max(b3, axis=1, keepdims=True)
    i1 = jnp.min(jnp.where(b3 == m1, j_io, float(per)), axis=1, keepdims=True)
    m2 = jnp.max(jnp.where(j_io == i1, ninf, b3), axis=1, keepdims=True)
    gs = m1 + m2
    gsel = jnp.zeros((G, 1, tm), F32)
    for _ in range(TOPK_GROUPS):
        gm = jnp.max(gs, axis=0, keepdims=True)
        gi = jnp.min(jnp.where(gs == gm, g_io, float(G)), axis=0, keepdims=True)
        hit = g_io == gi
        gsel = jnp.where(hit, 1.0, gsel)
        gs = jnp.where(hit, ninf, gs)
    masked = jnp.where(gsel > 0.0, b3, ninf)

    def red2(fn, a):
        return fn(fn(a, axis=1, keepdims=True), axis=0, keepdims=True)

    sel = jnp.zeros((G, per, tm), F32)
    idxs, scs = [], []
    for _ in range(TOP_K):
        mx = red2(jnp.max, masked)
        ei = red2(jnp.min, jnp.where(masked == mx, e_io, float(E)))
        hit = e_io == ei
        scs.append(red2(jnp.sum, jnp.where(hit, s3, 0.0)))
        idxs.append(ei)
        sel = jnp.where(hit, 1.0, sel)
        masked = jnp.where(hit, ninf, masked)
    wsum = scs[0]
    for k in range(1, TOP_K):
        wsum = wsum + scs[k]

    sel2 = sel.reshape(E, tm)
    upper = (lax.broadcasted_iota(jnp.int32, (tm, tm), 0) < lax.broadcasted_iota(jnp.int32, (tm, tm), 1))
    excl = jnp.dot(sel2.astype(BF16), upper.astype(BF16), preferred_element_type=F32)
    cnt3 = (excl + base_sc[...]).reshape(G, per, tm)
    for k in range(TOP_K):
        hit = e_io == idxs[k]
        rank = red2(jnp.sum, jnp.where(hit, cnt3, 0.0))
        idx_ref[k:k + 1, :] = idxs[k].reshape(1, tm).astype(jnp.int32)
        wt_ref[k:k + 1, :] = (scs[k] / wsum * ROUTED_SCALE).reshape(1, tm)
        rank_ref[k:k + 1, :] = rank.reshape(1, tm).astype(jnp.int32)
    base_sc[...] = base_sc[...] + jnp.sum(sel2, axis=1, keepdims=True)
    cnt_ref[...] = jnp.broadcast_to(base_sc[...], cnt_ref.shape).astype(jnp.int32)


def _router(h, wr_hi, wr_lo, e_bias):
    n, d = h.shape
    tm = _pick(n, (512, 256, 128))
    outk = pl.BlockSpec((TOP_K, tm), lambda i: (0, i))
    return pl.pallas_call(
        functools.partial(_router_kernel, tm=tm),
        grid=(n // tm,),
        in_specs=[pl.BlockSpec((tm, d), lambda i: (i, 0)),
                  pl.BlockSpec((N_EXPERTS, d), lambda i: (0, 0)),
                  pl.BlockSpec((N_EXPERTS, d), lambda i: (0, 0)),
                  pl.BlockSpec((N_EXPERTS, 1), lambda i: (0, 0))],
        out_specs=[outk, outk, outk, pl.BlockSpec((N_EXPERTS, LANES), lambda i: (0, 0))],
        out_shape=[jax.ShapeDtypeStruct((TOP_K, n), jnp.int32),
                   jax.ShapeDtypeStruct((TOP_K, n), F32),
                   jax.ShapeDtypeStruct((TOP_K, n), jnp.int32),
                   jax.ShapeDtypeStruct((N_EXPERTS, LANES), jnp.int32)],
        scratch_shapes=[pltpu.VMEM((N_EXPERTS, 1), F32)],
        compiler_params=_cp(("arbitrary",)),
        name="moe_router",
    )(h, wr_hi, wr_lo, e_bias.reshape(N_EXPERTS, 1))


def _row_copy(src, dst, sem):
    return pltpu.make_async_copy(src, dst, sem)


TOK_ROWS = 8


def _slab_rows(i, ns):
    return pl.ds(pl.multiple_of(i * ns, ns), ns)


def _expert_kernel(be_ref, nu_ref, tok_cur, tok_nxt, h_ref, w1_ref, w3_ref, w2_ref, o_ref,
                   xbuf, w1b, w3b, w2b, sem, *, tm, ns):
    j = pl.program_id(0)
    n_used = nu_ref[0]
    slot = lax.rem(j, 2)

    def gather(tok_ref, blk, dst_slot, wait):
        row = lax.rem(blk, TOK_ROWS)

        def body(r, carry):
            t = tok_ref[row, r]
            cp = _row_copy(h_ref.at[_slab_rows(t, ns), :], xbuf.at[dst_slot, _slab_rows(r, ns), :],
                           sem.at[dst_slot])
            if wait:
                cp.wait()
            else:
                cp.start()
            return carry

        lax.fori_loop(0, tm, body, 0, unroll=8)

    @pl.when(j == 0)
    def _():
        gather(tok_cur, j, slot, False)

    @pl.when(j + 1 < n_used)
    def _():
        gather(tok_nxt, j + 1, 1 - slot, False)

    @pl.when(j < n_used)
    def _():
        @pl.when((j == 0) | (be_ref[j] != be_ref[jnp.maximum(j - 1, 0)]))
        def _():
            w1b[...] = w1_ref[...].astype(BF16)
            w3b[...] = w3_ref[...].astype(BF16)
            w2b[...] = w2_ref[...].astype(BF16)

        gather(tok_cur, j, slot, True)
        parts = []
        for s in range(ns):
            lo, hi = _load_slab_pairs(xbuf, (slot,), s, tm, ns)
            parts += [lo.astype(BF16), hi.astype(BF16)]
        x = jnp.concatenate(parts, axis=1)
        a = jnp.dot(x, w1b[...], preferred_element_type=F32)
        b = jnp.dot(x, w3b[...], preferred_element_type=F32)
        hb = (_silu(a) * b).astype(BF16)
        _store_slab(o_ref, jnp.dot(hb, w2b[...], preferred_element_type=F32))

    @pl.when(j >= n_used)
    def _():
        o_ref[...] = jnp.zeros(o_ref.shape, U32)


def _experts(hs, tok_of_slot, blk_e, n_used, w1, w3, w2, layer):
    d, f = w1.shape[2], w1.shape[3]
    ns = d // HEAD_SLAB
    nb, tm = tok_of_slot.shape
    assert nb % TOK_ROWS == 0
    grid_spec = pltpu.PrefetchScalarGridSpec(
        num_scalar_prefetch=2,
        grid=(nb,),
        in_specs=[pl.BlockSpec((TOK_ROWS, tm), lambda j, be, nu: (j // TOK_ROWS, 0), memory_space=pltpu.SMEM),
                  pl.BlockSpec((TOK_ROWS, tm), lambda j, be, nu: (jnp.minimum(j + 1, nb - 1) // TOK_ROWS, 0),
                               memory_space=pltpu.SMEM),
                  pl.BlockSpec(memory_space=pl.ANY),
                  pl.BlockSpec((None, None, d, f), lambda j, be, nu: (layer, be[j], 0, 0)),
                  pl.BlockSpec((None, None, d, f), lambda j, be, nu: (layer, be[j], 0, 0)),
                  pl.BlockSpec((None, None, f, d), lambda j, be, nu: (layer, be[j], 0, 0))],
        out_specs=pl.BlockSpec((tm * ns, LANES), lambda j, be, nu: (j, 0)),
        scratch_shapes=[pltpu.VMEM((2, tm * ns, LANES), U32),
                        pltpu.VMEM((d, f), BF16), pltpu.VMEM((d, f), BF16), pltpu.VMEM((f, d), BF16),
                        pltpu.SemaphoreType.DMA((2,))],
    )
    return pl.pallas_call(
        functools.partial(_expert_kernel, tm=tm, ns=ns),
        grid_spec=grid_spec,
        out_shape=jax.ShapeDtypeStruct((nb * tm * ns, LANES), U32),
        compiler_params=_cp(("arbitrary",)),
        name="moe_experts",
    )(blk_e, n_used, tok_of_slot, tok_of_slot, hs, w1, w3, w2)


def _combine_kernel(dest_ref, h_ref, wt_ref, ws1_ref, ws3_ref, ws2_ref, g_ref, b_ref, ys_ref,
                    of_ref, *rest, tm, ns):
    ob_ref = rest[0] if len(rest) == 3 else None
    gbuf, sem = rest[-2:]

    def copy(k, r):
        d = dest_ref[k, r]
        return _row_copy(ys_ref.at[_slab_rows(d, ns), :], gbuf.at[k, _slab_rows(r, ns), :], sem.at[0])

    def start(r, carry):
        for k in range(TOP_K):
            copy(k, r).start()
        return carry

    def wait(r, carry):
        for k in range(TOP_K):
            copy(k, r).wait()
        return carry

    lax.fori_loop(0, tm, start, 0)
    h = h_ref[...]
    xb = h.astype(BF16)
    a = jnp.dot(xb, ws1_ref[...], preferred_element_type=F32)
    b = jnp.dot(xb, ws3_ref[...], preferred_element_type=F32)
    shared = jnp.dot((_silu(a) * b).astype(BF16), ws2_ref[...], preferred_element_type=F32)
    lax.fori_loop(0, tm, wait, 0)
    wt = wt_ref[...]
    parts = []
    for s in range(ns):
        lo = hi = None
        for k in range(TOP_K):
            plo, phi = _load_slab_pairs(gbuf, (k,), s, tm, ns)
            w = wt[:, k:k + 1]
            lo = w * plo if lo is None else lo + w * plo
            hi = w * phi if hi is None else hi + w * phi
        parts += [lo, hi]
    acc = shared + jnp.concatenate(parts, axis=1)
    r = _ln_rows(ALPHA * h + acc, g_ref[...], b_ref[...])
    of_ref[...] = r
    if ob_ref is not None:
        ob_ref[...] = r.astype(BF16)


def _combine(h, dest, wt_tok, ws1, ws3, ws2, g, b, ys, final_shape=None):
    n, d = h.shape
    f = ws1.shape[1]
    tm = LANES
    row = pl.BlockSpec((tm, d), lambda i: (i, 0))
    vec = pl.BlockSpec((1, d), lambda i: (0, 0))
    if final_shape is None:
        out_specs = [row, row]
        out_shape = [jax.ShapeDtypeStruct((n, d), F32), jax.ShapeDtypeStruct((n, d), BF16)]
    else:
        batch, seq = final_shape
        nt = seq // tm + 1
        out_specs = [pl.BlockSpec((None, tm, d), lambda i: (i // nt, jnp.maximum(i % nt - 1, 0), 0))]
        out_shape = [jax.ShapeDtypeStruct((batch, seq, d), F32)]
    ns = d // HEAD_SLAB
    return pl.pallas_call(
        functools.partial(_combine_kernel, tm=tm, ns=ns),
        grid=(n // tm,),
        in_specs=[pl.BlockSpec((TOP_K, tm), lambda i: (0, i), memory_space=pltpu.SMEM),
                  row,
                  pl.BlockSpec((tm, TOP_K), lambda i: (i, 0)),
                  pl.BlockSpec((d, f), lambda i: (0, 0)),
                  pl.BlockSpec((d, f), lambda i: (0, 0)),
                  pl.BlockSpec((f, d), lambda i: (0, 0)),
                  vec, vec,
                  pl.BlockSpec(memory_space=pl.ANY)],
        out_specs=out_specs,
        out_shape=out_shape,
        scratch_shapes=[pltpu.VMEM((TOP_K, tm * ns, LANES), U32), pltpu.SemaphoreType.DMA((1,))],
        compiler_params=_cp(("arbitrary",)),
        name="moe_combine",
    )(dest, h, wt_tok, ws1, ws3, ws2, g.reshape(1, d), b.reshape(1, d), ys)


def _moe(h_f32, hs, w_router, e_bias, w1, w3, w2, layer, ws1, ws3, ws2, g, b, final_shape=None):
    n = h_f32.shape[0]
    wr_t = w_router.T
    wr_hi = wr_t.astype(BF16)
    wr_lo = (wr_t - wr_hi.astype(F32)).astype(BF16)
    idx, wts, rank, cnt = _router(h_f32, wr_hi, wr_lo, e_bias)
    counts = cnt[:, 0]
    blk = EXPERT_ROWS
    pcounts = (counts + blk - 1) // blk * blk
    pends = jnp.cumsum(pcounts)
    pstarts = pends - pcounts
    e_ids = jnp.arange(N_EXPERTS, dtype=jnp.int32)
    start_of = jnp.sum(jnp.where(idx[None] == e_ids[:, None, None], pstarts[:, None, None], 0), axis=0)
    dest = (start_of + rank).astype(jnp.int32)
    nb = n * TOP_K // blk + N_EXPERTS
    blk_e = jnp.minimum(jnp.sum(pends[None, :] <= (jnp.arange(nb, dtype=jnp.int32) * blk)[:, None], axis=1),
                        N_EXPERTS - 1).astype(jnp.int32)
    n_used = (pends[-1:] // blk).astype(jnp.int32)
    tok = jnp.broadcast_to(jnp.arange(n, dtype=jnp.int32)[None], (TOP_K, n))
    tok_of_slot = jnp.zeros((nb * blk,), jnp.int32).at[dest.reshape(-1)].set(
        tok.reshape(-1), unique_indices=True).reshape(nb, blk)
    ys = _experts(hs, tok_of_slot, blk_e, n_used, w1, w3, w2, layer)
    return _combine(h_f32, dest, wts.T, ws1.astype(BF16), ws3.astype(BF16), ws2.astype(BF16), g, b, ys,
                    final_shape)


def _in_proj_layout(w_in, b_in, d, ql, kvl):
    hq, hv = ML_HEADS * ML_QK, ML_HEADS * ML_V
    sizes = (ql, kvl, QK_ROPE, hq, hq, hv, hv, ML_HEADS, ML_HEADS, d, d)
    offs = np.concatenate([[0], np.cumsum(sizes)])
    seg = lambda a, i: a[..., int(offs[i]):int(offs[i + 1])]
    order = (0, 1, 3, 4, 5, 6, 9, 10)
    w_main = jnp.concatenate([seg(w_in, i) for i in order], axis=-1)
    b_main = jnp.concatenate([seg(b_in, i) for i in order], axis=-1)
    half = QK_ROPE // 2

    def small(a):
        kr = seg(a, 2)
        x1, x2 = kr[..., :half], kr[..., half:]
        pad = jnp.zeros(a.shape[:-1] + (LANES - 2 * ML_HEADS,), a.dtype)
        return jnp.concatenate([x1, x2, x2, x1, seg(a, 7), seg(a, 8), pad], axis=-1)

    main_offs = np.concatenate([[0], np.cumsum([sizes[i] for i in order])])
    return w_main, b_main, small(w_in), small(b_in), [int(o) for o in main_offs]


def _uq_layout(w_uq):
    kq = w_uq.shape[0]
    w = w_uq.reshape(kq, MLA_HEADS, QK_NOPE + QK_ROPE)
    half = QK_ROPE // 2
    nope, x1, x2 = w[..., :QK_NOPE], w[..., QK_NOPE:QK_NOPE + half], w[..., QK_NOPE + half:]
    return jnp.concatenate([nope, x1, x2, x2, x1], axis=-1).reshape(kq, MLA_HEADS * HEAD_SLAB)


def _ukv_layout(w_ukv):
    kk = w_ukv.shape[0]
    w = w_ukv.reshape(kk, MLA_HEADS, QK_NOPE + V_HEAD)
    return jnp.concatenate([w[..., :QK_NOPE].reshape(kk, -1), w[..., QK_NOPE:].reshape(kk, -1)], axis=-1)


def _rope_tables(tp):
    half = QK_ROPE // 2
    pos = jnp.arange(tp, dtype=F32) - PADL
    inv_freq = 1.0 / (ROPE_THETA ** (jnp.arange(0, QK_ROPE, 2, dtype=F32) / QK_ROPE))
    ang = pos[:, None] * inv_freq[None, :]
    cos, sin = jnp.cos(ang), jnp.sin(ang)
    zero = jnp.zeros((tp, LANES - 2 * half), F32)
    return jnp.concatenate([cos, cos, zero], axis=1), jnp.concatenate([-sin, sin, zero], axis=1)


def kernel(x, meta, ln_in_g, ln_in_b, w_in, b_in, q_norm_g, kv_norm_g, w_uq, w_ukv, conv_w, conv_b, ml_norm_g, w_br_mla, w_br_mlstm, w_out, ln1_g, ln1_b, w_router, e_bias, w1, w3, w2, ws1, ws3, ws2, ln2_g, ln2_b):
    batch, seq, d = x.shape
    depth = w_in.shape[0]
    ql, kvl = q_norm_g.shape[1], kv_norm_g.shape[1]
    assert ql == kvl and seq % LANES == 0
    tp = LANES + seq
    n = batch * tp
    hq = ML_HEADS * ML_QK

    head = jnp.concatenate([jnp.zeros((PADL, d), x.dtype), meta.astype(x.dtype)], axis=0)
    h_f32, h_bf = _layer_norm_in(x, head, ln_in_g, ln_in_b)
    cos_t, sin_t = _rope_tables(tp)
    col_scale = jnp.concatenate([jnp.ones((1, hq), F32), jnp.full((1, hq), ML_QK ** -0.5, F32)], axis=1)

    for l in range(depth):
        w_main, b_main, w_small, b_small, offs = _in_proj_layout(w_in[l], b_in[l], d, ql, kvl)
        o_mq, o_mv, o_mo, o_ga, o_gb = offs[2], offs[4], offs[5], offs[6], offs[7]
        z = _matmul_bias(h_bf, w_main.astype(BF16), b_main, BF16, sig_col=o_mo, name="in_proj")
        zs = _matmul_bias(h_bf, w_small.astype(BF16), b_small, F32, name="in_proj_small")

        q, k, v = _mla_qkv(z, zs, q_norm_g[l], kv_norm_g[l], _uq_layout(w_uq[l]).astype(BF16),
                           _ukv_layout(w_ukv[l]).astype(BF16), cos_t, sin_t, tp, ql)
        attn = _attention(q, k, v, batch, tp)

        assert o_mq % hq == 0 and o_mv % ML_V == 0 and o_mo % ML_V == 0
        qk = _qk_conv(z, conv_w[l], conv_b[l], col_scale, tp, o_mq // hq)
        hm = _mlstm(qk, z, zs, ml_norm_g[l], batch, tp, o_mv // ML_V, o_mo // ML_V)

        y = _merge(attn, hm, w_br_mla[l].astype(BF16), w_br_mlstm[l].astype(BF16), z, o_ga, o_gb)
        h_f32, hs = _outproj_ln(y, w_out[l].astype(BF16), h_f32, ln1_g[l], ln1_b[l])
        outs = _moe(h_f32, hs, w_router[l], e_bias[l], w1, w3, w2, l, ws1[l], ws3[l], ws2[l],
                    ln2_g[l], ln2_b[l], final_shape=(batch, seq) if l == depth - 1 else None)
        if l == depth - 1:
            return outs[0]
        h_f32, h_bf = outs
```

```python
import functools

import numpy as np
import jax
import jax.numpy as jnp
from jax import lax
from jax.experimental import pallas as pl
from jax.experimental.pallas import tpu as pltpu

N_META = 16
MLA_HEADS = 8
QK_NOPE = 128
QK_ROPE = 64
V_HEAD = 128
ROPE_THETA = 10000.0
ML_HEADS = 8
ML_QK = 128
ML_V = 256
CONV_K = 4
CHUNK = 64
N_EXPERTS = 64
TOP_K = 8
N_GROUPS = 8
TOPK_GROUPS = 4
ROUTED_SCALE = 2.5
DEPTH = 2
ALPHA = (2 * DEPTH) ** 0.25
LN_EPS = 1e-5
RMS_EPS = 1e-6
NEG = -1e30

LANES = 128
HEAD_SLAB = 2 * LANES
PADL = LANES - N_META
EXPERT_ROWS = 256
VMEM_LIMIT = 56 * 1024 * 1024

F32 = jnp.float32
BF16 = jnp.bfloat16
U32 = jnp.uint32
NT = (((1,), (1,)), ((), ()))
TN = (((0,), (0,)), ((), ()))


def _pick(n, cands):
    for c in cands:
        if n % c == 0:
            return c
    raise ValueError(f"no tile in {cands} divides {n}")


def _cp(sem, vmem=None):
    return pltpu.CompilerParams(dimension_semantics=sem, vmem_limit_bytes=vmem or VMEM_LIMIT)


def _ln_rows(x, g, b):
    mu = jnp.mean(x, axis=-1, keepdims=True)
    xc = x - mu
    var = jnp.mean(xc * xc, axis=-1, keepdims=True)
    return xc * lax.rsqrt(var + LN_EPS) * g + b


def _rms_rows(x, g):
    return x * lax.rsqrt(jnp.mean(x * x, axis=-1, keepdims=True) + RMS_EPS) * g


def _silu(x):
    return x * jax.nn.sigmoid(x)


def _store_slab(ref, x):
    rows, ns = x.shape[0], x.shape[1] // HEAD_SLAB
    for s in range(ns):
        lo = lax.bitcast_convert_type(x[:, s * HEAD_SLAB:s * HEAD_SLAB + LANES].astype(BF16).astype(F32), U32)
        hi = lax.bitcast_convert_type(x[:, s * HEAD_SLAB + LANES:(s + 1) * HEAD_SLAB].astype(BF16).astype(F32), U32)
        ref[pl.ds(s, rows, stride=ns), :] = (lo >> 16) | hi


def _load_slab_pairs(ref, lead, s, rows, ns):
    u = ref[lead + (pl.ds(s, rows, stride=ns), slice(None))]
    return (lax.bitcast_convert_type(u << 16, F32),
            lax.bitcast_convert_type(u & jnp.uint32(0xFFFF0000), F32))


def _ln_in_kernel(head_ref, x_ref, g_ref, b_ref, of_ref, ob_ref):
    i = pl.program_id(1)

    def emit(src):
        y = _ln_rows(src, g_ref[...], b_ref[...])
        of_ref[...] = y
        ob_ref[...] = y.astype(BF16)

    @pl.when(i == 0)
    def _():
        emit(head_ref[...])

    @pl.when(i > 0)
    def _():
        emit(x_ref[...])


def _layer_norm_in(x, head, g, b):
    batch, seq, d = x.shape
    nt = seq // LANES + 1
    row = pl.BlockSpec((LANES, d), lambda bb, i: (bb * nt + i, 0))
    vec = pl.BlockSpec((1, d), lambda bb, i: (0, 0))
    n = batch * nt * LANES
    return pl.pallas_call(
        _ln_in_kernel,
        grid=(batch, nt),
        in_specs=[pl.BlockSpec((LANES, d), lambda bb, i: (0, 0)),
                  pl.BlockSpec((None, LANES, d), lambda bb, i: (bb, jnp.maximum(i - 1, 0), 0)),
                  vec, vec],
        out_specs=[row, row],
        out_shape=[jax.ShapeDtypeStruct((n, d), F32), jax.ShapeDtypeStruct((n, d), BF16)],
        compiler_params=_cp(("parallel", "arbitrary")),
        name="ln_in",
    )(head, x, g.reshape(1, d), b.reshape(1, d))


def _mm_kernel(x_ref, w_ref, b_ref, o_ref, *, sig_tile):
    acc = jnp.dot(x_ref[...], w_ref[...], preferred_element_type=F32) + b_ref[...]
    if sig_tile is None:
        o_ref[...] = acc.astype(o_ref.dtype)
        return
    j = pl.program_id(0)

    @pl.when(j < sig_tile)
    def _():
        o_ref[...] = acc.astype(o_ref.dtype)

    @pl.when(j >= sig_tile)
    def _():
        o_ref[...] = jax.nn.sigmoid(acc).astype(o_ref.dtype)


def _matmul_bias(x, w, b, out_dtype, sig_col=None, name="mm"):
    n, kd = x.shape
    nc = w.shape[1]
    tm = _pick(n, (1024, 512, 256, 128))
    tn = _pick(nc, (1024, 512, 256, 128))
    sig_tile = None
    if sig_col is not None:
        assert sig_col % tn == 0
        sig_tile = sig_col // tn
    return pl.pallas_call(
        functools.partial(_mm_kernel, sig_tile=sig_tile),
        grid=(nc // tn, n // tm),
        in_specs=[pl.BlockSpec((tm, kd), lambda j, i: (i, 0)),
                  pl.BlockSpec((kd, tn), lambda j, i: (0, j)),
                  pl.BlockSpec((1, tn), lambda j, i: (0, j))],
        out_specs=pl.BlockSpec((tm, tn), lambda j, i: (i, j)),
        out_shape=jax.ShapeDtypeStruct((n, nc), out_dtype),
        compiler_params=_cp(("parallel", "parallel")),
        name=name,
    )(x, w, b.reshape(1, nc))


def _rope_slab(r, cos_t, sin_t):
    return r * cos_t + pltpu.roll(r, 2 * (QK_ROPE // 2), 1) * sin_t


def _qproj_kernel(c_ref, g_ref, w_ref, cos_ref, sin_ref, o_ref, *, heads, scale):
    xn = _rms_rows(c_ref[...].astype(F32), g_ref[...])
    q = jnp.dot(xn.astype(BF16), w_ref[...], preferred_element_type=F32) * scale
    cos_t, sin_t = cos_ref[...], sin_ref[...]
    for h in range(heads):
        lo = h * HEAD_SLAB
        o_ref[:, lo:lo + LANES] = q[:, lo:lo + LANES].astype(BF16)
        o_ref[:, lo + LANES:lo + HEAD_SLAB] = _rope_slab(q[:, lo + LANES:lo + HEAD_SLAB], cos_t, sin_t).astype(BF16)


def _kproj_kernel(c_ref, g_ref, w_ref, kr_ref, cos_ref, sin_ref, k_ref, v_ref, *, heads):
    xn = _rms_rows(c_ref[...].astype(F32), g_ref[...])
    kv = jnp.dot(xn.astype(BF16), w_ref[...], preferred_element_type=F32)
    rr = _rope_slab(kr_ref[...], cos_ref[...], sin_ref[...]).astype(BF16)
    for h in range(heads):
        lo = h * HEAD_SLAB
        k_ref[:, lo:lo + LANES] = kv[:, h * LANES:(h + 1) * LANES].astype(BF16)
        k_ref[:, lo + LANES:lo + HEAD_SLAB] = rr
    v_ref[...] = kv[:, heads * LANES:].astype(BF16)


def _mla_qkv(z, zs, qg, kvg, wq, wkv, cos_t, sin_t, tp, ql):
    n = z.shape[0]
    heads = MLA_HEADS
    tm = _pick(tp, (384, 256, 128))
    nt = tp // tm
    scale = (QK_NOPE + QK_ROPE) ** -0.5
    tab = pl.BlockSpec((tm, LANES), lambda i: (i % nt, 0))
    q = pl.pallas_call(
        functools.partial(_qproj_kernel, heads=heads, scale=scale),
        grid=(n // tm,),
        in_specs=[pl.BlockSpec((tm, ql), lambda i: (i, 0)),
                  pl.BlockSpec((1, ql), lambda i: (0, 0)),
                  pl.BlockSpec((ql, heads * HEAD_SLAB), lambda i: (0, 0)),
                  tab, tab],
        out_specs=pl.BlockSpec((tm, heads * HEAD_SLAB), lambda i: (i, 0)),
        out_shape=jax.ShapeDtypeStruct((n, heads * HEAD_SLAB), BF16),
        compiler_params=_cp(("parallel",)),
        name="mla_qproj",
    )(z, qg.reshape(1, ql), wq, cos_t, sin_t)
    k, v = pl.pallas_call(
        functools.partial(_kproj_kernel, heads=heads),
        grid=(n // tm,),
        in_specs=[pl.BlockSpec((tm, ql), lambda i: (i, 1)),
                  pl.BlockSpec((1, ql), lambda i: (0, 0)),
                  pl.BlockSpec((ql, 2 * heads * LANES), lambda i: (0, 0)),
                  pl.BlockSpec((tm, LANES), lambda i: (i, 0)),
                  tab, tab],
        out_specs=[pl.BlockSpec((tm, heads * HEAD_SLAB), lambda i: (i, 0)),
                   pl.BlockSpec((tm, heads * LANES), lambda i: (i, 0))],
        out_shape=[jax.ShapeDtypeStruct((n, heads * HEAD_SLAB), BF16),
                   jax.ShapeDtypeStruct((n, heads * LANES), BF16)],
        compiler_params=_cp(("parallel",)),
        name="mla_kvproj",
    )(z, kvg.reshape(1, ql), wkv, zs, cos_t, sin_t)
    return q, k, v


ATTN_HEADS_PER_STEP = 2


def _attn_kernel(q_ref, k_ref, v_ref, o_ref, s_sc, m_sc, l_sc, acc_sc, *, t, hps):
    i = pl.program_id(2)
    nl = t // LANES

    def lane_fold(x, fn):
        r = x[:, :LANES]
        for c in range(1, nl):
            r = fn(r, x[:, c * LANES:(c + 1) * LANES])
        return r

    m_sc[...] = jnp.full(m_sc.shape, NEG, F32)

    def scores(j, masked):
        rows = pl.ds(pl.multiple_of(j * t, t), t)
        if masked:
            qpos = i * t + lax.broadcasted_iota(jnp.int32, (t, t), 0)
            kpos = j * t + lax.broadcasted_iota(jnp.int32, (t, t), 1)
            keep = (kpos <= qpos) & (kpos >= PADL)
        for h in range(hps):
            q = q_ref[:, h * HEAD_SLAB:(h + 1) * HEAD_SLAB]
            s = lax.dot_general(q, k_ref[rows, h * HEAD_SLAB:(h + 1) * HEAD_SLAB], NT, preferred_element_type=F32)
            if masked:
                s = jnp.where(keep, s, NEG)
            s_sc[h, j] = s
            m_sc[h] = jnp.maximum(m_sc[h], lane_fold(s, jnp.maximum))

    scores(0, True)

    def score_body(j, carry):
        scores(j, False)
        return carry

    lax.fori_loop(1, i, score_body, 0)

    @pl.when(i > 0)
    def _():
        scores(i, True)

    for h in range(hps):
        m_sc[h] = jnp.broadcast_to(jnp.max(m_sc[h], axis=1, keepdims=True), (t, LANES))
    l_sc[...] = jnp.zeros(l_sc.shape, F32)
    acc_sc[...] = jnp.zeros(acc_sc.shape, F32)

    def pv_body(j, carry):
        rows = pl.ds(pl.multiple_of(j * t, t), t)
        for h in range(hps):
            s = s_sc[h, j]
            mb = m_sc[h]
            p = jnp.concatenate([jnp.exp(s[:, c * LANES:(c + 1) * LANES] - mb) for c in range(nl)], axis=1)
            l_sc[h] = l_sc[h] + lane_fold(p, jnp.add)
            acc_sc[h] = acc_sc[h] + jnp.dot(p.astype(BF16), v_ref[rows, h * V_HEAD:(h + 1) * V_HEAD],
                                            preferred_element_type=F32)
        return carry

    lax.fori_loop(0, i + 1, pv_body, 0)

    for h in range(hps):
        l = jnp.sum(l_sc[h], axis=1, keepdims=True)
        o_ref[:, h * V_HEAD:(h + 1) * V_HEAD] = (acc_sc[h] / l).astype(o_ref.dtype)


def _attention(q, k, v, batch, tp):
    n = q.shape[0]
    heads = MLA_HEADS
    hps = ATTN_HEADS_PER_STEP
    assert heads % hps == 0
    t = _pick(tp, (384, 256, 128))
    nt = tp // t
    return pl.pallas_call(
        functools.partial(_attn_kernel, t=t, hps=hps),
        grid=(batch, heads // hps, nt),
        in_specs=[pl.BlockSpec((t, hps * HEAD_SLAB), lambda b, h, i: (b * nt + i, h)),
                  pl.BlockSpec((tp, hps * HEAD_SLAB), lambda b, h, i: (b, h)),
                  pl.BlockSpec((tp, hps * V_HEAD), lambda b, h, i: (b, h))],
        out_specs=pl.BlockSpec((t, hps * V_HEAD), lambda b, h, i: (b * nt + i, h)),
        out_shape=jax.ShapeDtypeStruct((n, heads * V_HEAD), BF16),
        scratch_shapes=[pltpu.VMEM((hps, nt, t, t), F32), pltpu.VMEM((hps, t, LANES), F32),
                        pltpu.VMEM((hps, t, LANES), F32), pltpu.VMEM((hps, t, V_HEAD), F32)],
        compiler_params=_cp(("parallel", "parallel", "parallel")),
        name="mla_attention",
    )(q, k, v)


HALO = 16


def _conv_kernel(x_ref, halo_ref, w_ref, b_ref, s_ref, o_ref, *, tm, tp):
    i = pl.program_id(0)
    start = lax.rem(i * tm, tp)
    pos = start + lax.broadcasted_iota(jnp.int32, (tm, 1), 0)
    x = jnp.where(pos >= PADL, x_ref[...].astype(F32), 0.0)
    hpos = start - HALO + lax.broadcasted_iota(jnp.int32, (HALO, 1), 0)
    halo = jnp.where(hpos >= PADL, halo_ref[...].astype(F32), 0.0)
    ext = jnp.concatenate([halo, x], axis=0)
    w = w_ref[...]
    y = b_ref[...] + w[CONV_K - 1:CONV_K, :] * x
    for j in range(1, CONV_K):
        y = y + w[CONV_K - 1 - j:CONV_K - j, :] * pltpu.roll(ext, j, 0)[HALO:, :]
    o_ref[...] = (_silu(y) * s_ref[...]).astype(BF16)


def _qk_conv(z, conv_w, conv_b, col_scale, tp, off_blk):
    n = z.shape[0]
    cw = conv_w.shape[1] // 2
    tm = _pick(tp, (384, 256, 128))
    return pl.pallas_call(
        functools.partial(_conv_kernel, tm=tm, tp=tp),
        grid=(n // tm, 2),
        in_specs=[pl.BlockSpec((tm, cw), lambda i, c: (i, off_blk + c)),
                  pl.BlockSpec((HALO, cw), lambda i, c: (jnp.maximum(i * (tm // HALO) - 1, 0), off_blk + c)),
                  pl.BlockSpec((CONV_K, cw), lambda i, c: (0, c)),
                  pl.BlockSpec((1, cw), lambda i, c: (0, c)),
                  pl.BlockSpec((1, cw), lambda i, c: (0, c))],
        out_specs=pl.BlockSpec((tm, cw), lambda i, c: (i, c)),
        out_shape=jax.ShapeDtypeStruct((n, 2 * cw), BF16),
        compiler_params=_cp(("parallel", "parallel")),
        name="mlstm_qk_conv",
    )(z, z, conv_w, conv_b.reshape(1, -1), col_scale)


def _mlstm_kernel(q_ref, k_ref, v_ref, gt_ref, og_ref, ng_ref, o_ref, c_sc, n_sc, m_sc, *, group, heads):
    h = pl.program_id(1)
    g = pl.program_id(2)
    L = CHUNK

    @pl.when(g == 0)
    def _():
        c_sc[...] = jnp.zeros(c_sc.shape, F32)
        n_sc[...] = jnp.zeros(n_sc.shape, F32)
        m_sc[...] = jnp.zeros(m_sc.shape, F32)

    lane = lax.broadcasted_iota(jnp.int32, (L, LANES), 1)
    r_i = lax.broadcasted_iota(jnp.int32, (L, L), 0)
    c_i = lax.broadcasted_iota(jnp.int32, (L, L), 1)
    eye = r_i == c_i
    causal = c_i <= r_i

    def to_row(col):
        return jnp.sum(jnp.where(eye, col, 0.0), axis=0, keepdims=True)

    def to_col(row):
        return jnp.sum(jnp.where(eye, row, 0.0), axis=1, keepdims=True)

    for c in range(group):
        rows = slice(c * L, (c + 1) * L)
        gt = gt_ref[rows, :]
        i_col = jnp.sum(jnp.where(lane == h, gt, 0.0), axis=1, keepdims=True)
        f_col = jnp.sum(jnp.where(lane == heads + h, gt, 0.0), axis=1, keepdims=True)
        pos = (g * group + c) * L + lax.broadcasted_iota(jnp.int32, (L, 1), 0)
        pad = pos < PADL
        i_col = jnp.where(pad, NEG, i_col)
        lf_col = jnp.where(pad, 0.0, -(jnp.maximum(-f_col, 0.0) + jnp.log1p(jnp.exp(-jnp.abs(f_col)))))
        b_row = jnp.sum(jnp.where(r_i <= c_i, lf_col, 0.0), axis=0, keepdims=True)
        b_col = to_col(b_row)
        i_row = to_row(i_col)
        gtot = jnp.sum(lf_col, axis=0, keepdims=True)
        m_prev = m_sc[...]
        dmat = jnp.where(causal, b_col - b_row + i_row, NEG)
        inter = b_col + m_prev
        mrow = jnp.maximum(jnp.max(dmat, axis=1, keepdims=True), inter)
        dexp = jnp.exp(dmat - mrow)
        q = q_ref[rows, :]
        k = k_ref[rows, :]
        v = v_ref[rows, :]
        s = lax.dot_general(q, k, NT, preferred_element_type=F32) * dexp
        e_in = jnp.exp(inter - mrow)
        num = (jnp.dot(s.astype(BF16), v, preferred_element_type=F32)
               + e_in * jnp.dot(q, c_sc[...].astype(BF16), preferred_element_type=F32))
        den = (jnp.sum(s, axis=1, keepdims=True)
               + e_in * jnp.sum(q.astype(F32) * n_sc[...], axis=1, keepdims=True))
        hc = num / jnp.maximum(jnp.abs(den), jnp.exp(-mrow))
        a_row = gtot - b_row + i_row
        m_new = jnp.maximum(gtot + m_prev, jnp.max(a_row, axis=1, keepdims=True))
        ea_col = to_col(jnp.exp(a_row - m_new))
        decay = jnp.exp(gtot + m_prev - m_new)
        ks = k.astype(F32) * ea_col
        c_sc[...] = decay * c_sc[...] + lax.dot_general(ks.astype(BF16), v, TN, preferred_element_type=F32)
        n_sc[...] = decay * n_sc[...] + jnp.sum(ks, axis=0, keepdims=True)
        m_sc[...] = m_new
        hn = _rms_rows(hc, ng_ref[...])
        o_ref[rows, :] = (hn * og_ref[rows, :].astype(F32)).astype(BF16)


def _mlstm(qk, z, zs, norm_g, batch, tp, v_blk, og_blk):
    n = qk.shape[0]
    heads = ML_HEADS
    nchunks = tp // CHUNK
    group = _pick(nchunks, (6, 5, 4, 3, 2, 1))
    rows = group * CHUNK
    ng = nchunks // group
    return pl.pallas_call(
        functools.partial(_mlstm_kernel, group=group, heads=heads),
        grid=(batch, heads, ng),
        in_specs=[pl.BlockSpec((rows, ML_QK), lambda b, h, g: (b * ng + g, h)),
                  pl.BlockSpec((rows, ML_QK), lambda b, h, g: (b * ng + g, heads + h)),
                  pl.BlockSpec((rows, ML_V), lambda b, h, g: (b * ng + g, v_blk + h)),
                  pl.BlockSpec((rows, LANES), lambda b, h, g: (b * ng + g, 1)),
                  pl.BlockSpec((rows, ML_V), lambda b, h, g: (b * ng + g, og_blk + h)),
                  pl.BlockSpec((1, ML_V), lambda b, h, g: (0, h))],
        out_specs=pl.BlockSpec((rows, ML_V), lambda b, h, g: (b * ng + g, h)),
        out_shape=jax.ShapeDtypeStruct((n, heads * ML_V), BF16),
        scratch_shapes=[pltpu.VMEM((ML_QK, ML_V), F32), pltpu.VMEM((1, ML_QK), F32), pltpu.VMEM((1, 1), F32)],
        compiler_params=_cp(("parallel", "parallel", "arbitrary")),
        name="mlstm_scan",
    )(qk, qk, z, zs, z, norm_g.reshape(1, -1))


def _merge_kernel(a_ref, hm_ref, wa_ref, wb_ref, ga_ref, gb_ref, o_ref):
    ya = jnp.dot(a_ref[...], wa_ref[...], preferred_element_type=F32)
    yb = jnp.dot(hm_ref[...], wb_ref[...], preferred_element_type=F32)
    o_ref[...] = (ga_ref[...].astype(F32) * ya + gb_ref[...].astype(F32) * yb).astype(BF16)


def _merge(attn, hm, wa, wb, z, ga_off, gb_off):
    n = attn.shape[0]
    d = wa.shape[1]
    tm = _pick(n, (512, 256, 128))
    tn = _pick(d, (1024, 512, 256, 128))
    assert ga_off % tn == 0 and gb_off % tn == 0
    ga_blk, gb_blk = ga_off // tn, gb_off // tn
    return pl.pallas_call(
        _merge_kernel,
        grid=(d // tn, n // tm),
        in_specs=[pl.BlockSpec((tm, attn.shape[1]), lambda j, i: (i, 0)),
                  pl.BlockSpec((tm, hm.shape[1]), lambda j, i: (i, 0)),
                  pl.BlockSpec((wa.shape[0], tn), lambda j, i: (0, j)),
                  pl.BlockSpec((wb.shape[0], tn), lambda j, i: (0, j)),
                  pl.BlockSpec((tm, tn), lambda j, i: (i, ga_blk + j)),
                  pl.BlockSpec((tm, tn), lambda j, i: (i, gb_blk + j))],
        out_specs=pl.BlockSpec((tm, tn), lambda j, i: (i, j)),
        out_shape=jax.ShapeDtypeStruct((n, d), BF16),
        compiler_params=_cp(("parallel", "parallel")),
        name="branch_merge",
    )(attn, hm, wa, wb, z, z)


def _outln_kernel(y_ref, w_ref, h_ref, g_ref, b_ref, of_ref, os_ref):
    o = jnp.dot(y_ref[...], w_ref[...], preferred_element_type=F32)
    r = _ln_rows(ALPHA * h_ref[...] + o, g_ref[...], b_ref[...])
    of_ref[...] = r
    _store_slab(os_ref, r)


def _outproj_ln(y, w, h, g, b):
    n, d = h.shape
    tm = _pick(n, (256, 128))
    ns = d // HEAD_SLAB
    row = pl.BlockSpec((tm, d), lambda i: (i, 0))
    vec = pl.BlockSpec((1, d), lambda i: (0, 0))
    return pl.pallas_call(
        _outln_kernel,
        grid=(n // tm,),
        in_specs=[row, pl.BlockSpec((d, d), lambda i: (0, 0)), row, vec, vec],
        out_specs=[row, pl.BlockSpec((tm * ns, LANES), lambda i: (i, 0))],
        out_shape=[jax.ShapeDtypeStruct((n, d), F32), jax.ShapeDtypeStruct((n * ns, LANES), U32)],
        compiler_params=_cp(("parallel",)),
        name="outproj_ln",
    )(y, w, h, g.reshape(1, d), b.reshape(1, d))


def _router_kernel(x_ref, wh_ref, wl_ref, eb_ref, idx_ref, wt_ref, rank_ref, cnt_ref, base_sc, *, tm):
    i = pl.program_id(0)
    E, G = N_EXPERTS, N_GROUPS
    per = E // G

    @pl.when(i == 0)
    def _():
        base_sc[...] = jnp.zeros(base_sc.shape, F32)

    x = x_ref[...]
    xh = x.astype(BF16)
    xl = (x - xh.astype(F32)).astype(BF16)
    wh, wl = wh_ref[...], wl_ref[...]
    logits = (lax.dot_general(wh, xh, NT, preferred_element_type=F32)
              + lax.dot_general(wh, xl, NT, preferred_element_type=F32)
              + lax.dot_general(wl, xh, NT, preferred_element_type=F32))
    scores = jax.nn.sigmoid(logits)
    biased = scores + eb_ref[...]
    s3 = scores.reshape(G, per, tm)
    b3 = biased.reshape(G, per, tm)
    j_io = lax.broadcasted_iota(jnp.int32, (G, per, tm), 1).astype(F32)
    g_io3 = lax.broadcasted_iota(jnp.int32, (G, per, tm), 0).astype(F32)
    e_io = g_io3 * per + j_io
    g_io = lax.broadcasted_iota(jnp.int32, (G, 1, tm), 0).astype(F32)
    ninf = -jnp.inf

    m1 = jnp.max(b3, axis=1, keepdims=True)
    i1 = jnp.min(jnp.where(b3 == m1, j_io, float(per)), axis=1, keepdims=True)
    m2 = jnp.max(jnp.where(j_io == i1, ninf, b3), axis=1, keepdims=True)
    gs = m1 + m2
    gsel = jnp.zeros((G, 1, tm), F32)
    for _ in range(TOPK_GROUPS):
        gm = jnp.max(gs, axis=0, keepdims=True)
        gi = jnp.min(jnp.where(gs == gm, g_io, float(G)), axis=0, keepdims=True)
        hit = g_io == gi
        gsel = jnp.where(hit, 1.0, gsel)
        gs = jnp.where(hit, ninf, gs)
    masked = jnp.where(gsel > 0.0, b3, ninf)

    def red2(fn, a):
        return fn(fn(a, axis=1, keepdims=True), axis=0, keepdims=True)

    sel = jnp.zeros((G, per, tm), F32)
    idxs, scs = [], []
    for _ in range(TOP_K):
        mx = red2(jnp.max, masked)
        ei = red2(jnp.min, jnp.where(masked == mx, e_io, float(E)))
        hit = e_io == ei
        scs.append(red2(jnp.sum, jnp.where(hit, s3, 0.0)))
        idxs.append(ei)
        sel = jnp.where(hit, 1.0, sel)
        masked = jnp.where(hit, ninf, masked)
    wsum = scs[0]
    for k in range(1, TOP_K):
        wsum = wsum + scs[k]

    sel2 = sel.reshape(E, tm)
    upper = (lax.broadcasted_iota(jnp.int32, (tm, tm), 0) < lax.broadcasted_iota(jnp.int32, (tm, tm), 1))
    excl = jnp.dot(sel2.astype(BF16), upper.astype(BF16), preferred_element_type=F32)
    cnt3 = (excl + base_sc[...]).reshape(G, per, tm)
    for k in range(TOP_K):
        hit = e_io == idxs[k]
        rank = red2(jnp.sum, jnp.where(hit, cnt3, 0.0))
        idx_ref[k:k + 1, :] = idxs[k].reshape(1, tm).astype(jnp.int32)
        wt_ref[k:k + 1, :] = (scs[k] / wsum * ROUTED_SCALE).reshape(1, tm)
        rank_ref[k:k + 1, :] = rank.reshape(1, tm).astype(jnp.int32)
    base_sc[...] = base_sc[...] + jnp.sum(sel2, axis=1, keepdims=True)
    cnt_ref[...] = jnp.broadcast_to(base_sc[...], cnt_ref.shape).astype(jnp.int32)


def _router(h, wr_hi, wr_lo, e_bias):
    n, d = h.shape
    tm = _pick(n, (512, 256, 128))
    outk = pl.BlockSpec((TOP_K, tm), lambda i: (0, i))
    return pl.pallas_call(
        functools.partial(_router_kernel, tm=tm),
        grid=(n // tm,),
        in_specs=[pl.BlockSpec((tm, d), lambda i: (i, 0)),
                  pl.BlockSpec((N_EXPERTS, d), lambda i: (0, 0)),
                  pl.BlockSpec((N_EXPERTS, d), lambda i: (0, 0)),
                  pl.BlockSpec((N_EXPERTS, 1), lambda i: (0, 0))],
        out_specs=[outk, outk, outk, pl.BlockSpec((N_EXPERTS, LANES), lambda i: (0, 0))],
        out_shape=[jax.ShapeDtypeStruct((TOP_K, n), jnp.int32),
                   jax.ShapeDtypeStruct((TOP_K, n), F32),
                   jax.ShapeDtypeStruct((TOP_K, n), jnp.int32),
                   jax.ShapeDtypeStruct((N_EXPERTS, LANES), jnp.int32)],
        scratch_shapes=[pltpu.VMEM((N_EXPERTS, 1), F32)],
        compiler_params=_cp(("arbitrary",)),
        name="moe_router",
    )(h, wr_hi, wr_lo, e_bias.reshape(N_EXPERTS, 1))


def _row_copy(src, dst, sem):
    return pltpu.make_async_copy(src, dst, sem)


TOK_ROWS = 8
DMA_UNROLL = 8


def _slab_rows(i, ns):
    return pl.ds(pl.multiple_of(i * ns, ns), ns)


def _expert_kernel(be_ref, nu_ref, tok_cur, tok_nxt, h_ref, w1_ref, w3_ref, w2_ref, o_ref,
                   xbuf, w1b, w3b, w2b, sem, *, tm, ns):
    j = pl.program_id(0)
    n_used = nu_ref[0]
    slot = lax.rem(j, 2)

    def gather(tok_ref, blk, dst_slot, wait):
        row = lax.rem(blk, TOK_ROWS)

        def body(g, carry):
            for u in range(DMA_UNROLL):
                r = g * DMA_UNROLL + u
                t = tok_ref[row, r]
                cp = _row_copy(h_ref.at[_slab_rows(t, ns), :], xbuf.at[dst_slot, _slab_rows(r, ns), :],
                               sem.at[dst_slot])
                if wait:
                    cp.wait()
                else:
                    cp.start(priority=u % 2)
            return carry

        lax.fori_loop(0, tm // DMA_UNROLL, body, 0)

    @pl.when(j == 0)
    def _():
        gather(tok_cur, j, slot, False)

    @pl.when(j + 1 < n_used)
    def _():
        gather(tok_nxt, j + 1, 1 - slot, False)

    @pl.when(j < n_used)
    def _():
        @pl.when((j == 0) | (be_ref[j] != be_ref[jnp.maximum(j - 1, 0)]))
        def _():
            w1b[...] = w1_ref[...].astype(BF16)
            w3b[...] = w3_ref[...].astype(BF16)
            w2b[...] = w2_ref[...].astype(BF16)

        gather(tok_cur, j, slot, True)
        parts = []
        for s in range(ns):
            lo, hi = _load_slab_pairs(xbuf, (slot,), s, tm, ns)
            parts += [lo.astype(BF16), hi.astype(BF16)]
        x = jnp.concatenate(parts, axis=1)
        a = jnp.dot(x, w1b[...], preferred_element_type=F32)
        b = jnp.dot(x, w3b[...], preferred_element_type=F32)
        hb = (_silu(a) * b).astype(BF16)
        _store_slab(o_ref, jnp.dot(hb, w2b[...], preferred_element_type=F32))

    @pl.when(j >= n_used)
    def _():
        o_ref[...] = jnp.zeros(o_ref.shape, U32)


def _experts(hs, tok_of_slot, blk_e, n_used, w1, w3, w2, layer):
    d, f = w1.shape[2], w1.shape[3]
    ns = d // HEAD_SLAB
    nb, tm = tok_of_slot.shape
    assert nb % TOK_ROWS == 0
    grid_spec = pltpu.PrefetchScalarGridSpec(
        num_scalar_prefetch=2,
        grid=(nb,),
        in_specs=[pl.BlockSpec((TOK_ROWS, tm), lambda j, be, nu: (j // TOK_ROWS, 0), memory_space=pltpu.SMEM),
                  pl.BlockSpec((TOK_ROWS, tm), lambda j, be, nu: (jnp.minimum(j + 1, nb - 1) // TOK_ROWS, 0),
                               memory_space=pltpu.SMEM),
                  pl.BlockSpec(memory_space=pl.ANY),
                  pl.BlockSpec((None, None, d, f), lambda j, be, nu: (layer, be[j], 0, 0)),
                  pl.BlockSpec((None, None, d, f), lambda j, be, nu: (layer, be[j], 0, 0)),
                  pl.BlockSpec((None, None, f, d), lambda j, be, nu: (layer, be[j], 0, 0))],
        out_specs=pl.BlockSpec((tm * ns, LANES), lambda j, be, nu: (j, 0)),
        scratch_shapes=[pltpu.VMEM((2, tm * ns, LANES), U32),
                        pltpu.VMEM((d, f), BF16), pltpu.VMEM((d, f), BF16), pltpu.VMEM((f, d), BF16),
                        pltpu.SemaphoreType.DMA((2,))],
    )
    return pl.pallas_call(
        functools.partial(_expert_kernel, tm=tm, ns=ns),
        grid_spec=grid_spec,
        out_shape=jax.ShapeDtypeStruct((nb * tm * ns, LANES), U32),
        compiler_params=_cp(("arbitrary",)),
        name="moe_experts",
    )(blk_e, n_used, tok_of_slot, tok_of_slot, hs, w1, w3, w2)


def _combine_kernel(dest_ref, h_ref, wt_ref, ws1_ref, ws3_ref, ws2_ref, g_ref, b_ref, ys_ref,
                    of_ref, *rest, tm, ns):
    ob_ref = rest[0] if len(rest) == 3 else None
    gbuf, sem = rest[-2:]

    def copy(k, r):
        d = dest_ref[k, r]
        return _row_copy(ys_ref.at[_slab_rows(d, ns), :], gbuf.at[k, _slab_rows(r, ns), :], sem.at[0])

    def start(r, carry):
        for k in range(TOP_K):
            copy(k, r).start(priority=k % 2)
        return carry

    def wait(r, carry):
        for k in range(TOP_K):
            copy(k, r).wait()
        return carry

    lax.fori_loop(0, tm, start, 0)
    h = h_ref[...]
    xb = h.astype(BF16)
    a = jnp.dot(xb, ws1_ref[...], preferred_element_type=F32)
    b = jnp.dot(xb, ws3_ref[...], preferred_element_type=F32)
    shared = jnp.dot((_silu(a) * b).astype(BF16), ws2_ref[...], preferred_element_type=F32)
    lax.fori_loop(0, tm, wait, 0)
    wt = wt_ref[...]
    parts = []
    for s in range(ns):
        lo = hi = None
        for k in range(TOP_K):
            plo, phi = _load_slab_pairs(gbuf, (k,), s, tm, ns)
            w = wt[:, k:k + 1]
            lo = w * plo if lo is None else lo + w * plo
            hi = w * phi if hi is None else hi + w * phi
        parts += [lo, hi]
    acc = shared + jnp.concatenate(parts, axis=1)
    r = _ln_rows(ALPHA * h + acc, g_ref[...], b_ref[...])
    of_ref[...] = r
    if ob_ref is not None:
        ob_ref[...] = r.astype(BF16)


def _combine(h, dest, wt_tok, ws1, ws3, ws2, g, b, ys, final_shape=None):
    n, d = h.shape
    f = ws1.shape[1]
    tm = LANES
    row = pl.BlockSpec((tm, d), lambda i: (i, 0))
    vec = pl.BlockSpec((1, d), lambda i: (0, 0))
    if final_shape is None:
        out_specs = [row, row]
        out_shape = [jax.ShapeDtypeStruct((n, d), F32), jax.ShapeDtypeStruct((n, d), BF16)]
    else:
        batch, seq = final_shape
        nt = seq // tm + 1
        out_specs = [pl.BlockSpec((None, tm, d), lambda i: (i // nt, jnp.maximum(i % nt - 1, 0), 0))]
        out_shape = [jax.ShapeDtypeStruct((batch, seq, d), F32)]
    ns = d // HEAD_SLAB
    return pl.pallas_call(
        functools.partial(_combine_kernel, tm=tm, ns=ns),
        grid=(n // tm,),
        in_specs=[pl.BlockSpec((TOP_K, tm), lambda i: (0, i), memory_space=pltpu.SMEM),
                  row,
                  pl.BlockSpec((tm, TOP_K), lambda i: (i, 0)),
                  pl.BlockSpec((d, f), lambda i: (0, 0)),
                  pl.BlockSpec((d, f), lambda i: (0, 0)),
                  pl.BlockSpec((f, d), lambda i: (0, 0)),
                  vec, vec,
                  pl.BlockSpec(memory_space=pl.ANY)],
        out_specs=out_specs,
        out_shape=out_shape,
        scratch_shapes=[pltpu.VMEM((TOP_K, tm * ns, LANES), U32), pltpu.SemaphoreType.DMA((1,))],
        compiler_params=_cp(("arbitrary",)),
        name="moe_combine",
    )(dest, h, wt_tok, ws1, ws3, ws2, g.reshape(1, d), b.reshape(1, d), ys)


def _moe(h_f32, hs, w_router, e_bias, w1, w3, w2, layer, ws1, ws3, ws2, g, b, final_shape=None):
    n = h_f32.shape[0]
    wr_t = w_router.T
    wr_hi = wr_t.astype(BF16)
    wr_lo = (wr_t - wr_hi.astype(F32)).astype(BF16)
    idx, wts, rank, cnt = _router(h_f32, wr_hi, wr_lo, e_bias)
    counts = cnt[:, 0]
    blk = EXPERT_ROWS
    pcounts = (counts + blk - 1) // blk * blk
    pends = jnp.cumsum(pcounts)
    pstarts = pends - pcounts
    e_ids = jnp.arange(N_EXPERTS, dtype=jnp.int32)
    start_of = jnp.sum(jnp.where(idx[None] == e_ids[:, None, None], pstarts[:, None, None], 0), axis=0)
    dest = (start_of + rank).astype(jnp.int32)
    nb = n * TOP_K // blk + N_EXPERTS
    blk_e = jnp.minimum(jnp.sum(pends[None, :] <= (jnp.arange(nb, dtype=jnp.int32) * blk)[:, None], axis=1),
                        N_EXPERTS - 1).astype(jnp.int32)
    n_used = (pends[-1:] // blk).astype(jnp.int32)
    tok = jnp.broadcast_to(jnp.arange(n, dtype=jnp.int32)[None], (TOP_K, n))
    tok_of_slot = jnp.zeros((nb * blk,), jnp.int32).at[dest.reshape(-1)].set(
        tok.reshape(-1), unique_indices=True).reshape(nb, blk)
    ys = _experts(hs, tok_of_slot, blk_e, n_used, w1, w3, w2, layer)
    return _combine(h_f32, dest, wts.T, ws1.astype(BF16), ws3.astype(BF16), ws2.astype(BF16), g, b, ys,
                    final_shape)


def _in_proj_layout(w_in, b_in, d, ql, kvl):
    hq, hv = ML_HEADS * ML_QK, ML_HEADS * ML_V
    sizes = (ql, kvl, QK_ROPE, hq, hq, hv, hv, ML_HEADS, ML_HEADS, d, d)
    offs = np.concatenate([[0], np.cumsum(sizes)])
    seg = lambda a, i: a[..., int(offs[i]):int(offs[i + 1])]
    order = (0, 1, 3, 4, 5, 6, 9, 10)
    w_main = jnp.concatenate([seg(w_in, i) for i in order], axis=-1)
    b_main = jnp.concatenate([seg(b_in, i) for i in order], axis=-1)
    half = QK_ROPE // 2

    def small(a):
        kr = seg(a, 2)
        x1, x2 = kr[..., :half], kr[..., half:]
        pad = jnp.zeros(a.shape[:-1] + (LANES - 2 * ML_HEADS,), a.dtype)
        return jnp.concatenate([x1, x2, x2, x1, seg(a, 7), seg(a, 8), pad], axis=-1)

    main_offs = np.concatenate([[0], np.cumsum([sizes[i] for i in order])])
    return w_main, b_main, small(w_in), small(b_in), [int(o) for o in main_offs]


def _uq_layout(w_uq):
    kq = w_uq.shape[0]
    w = w_uq.reshape(kq, MLA_HEADS, QK_NOPE + QK_ROPE)
    half = QK_ROPE // 2
    nope, x1, x2 = w[..., :QK_NOPE], w[..., QK_NOPE:QK_NOPE + half], w[..., QK_NOPE + half:]
    return jnp.concatenate([nope, x1, x2, x2, x1], axis=-1).reshape(kq, MLA_HEADS * HEAD_SLAB)


def _ukv_layout(w_ukv):
    kk = w_ukv.shape[0]
    w = w_ukv.reshape(kk, MLA_HEADS, QK_NOPE + V_HEAD)
    return jnp.concatenate([w[..., :QK_NOPE].reshape(kk, -1), w[..., QK_NOPE:].reshape(kk, -1)], axis=-1)


def _rope_tables(tp):
    half = QK_ROPE // 2
    pos = jnp.arange(tp, dtype=F32) - PADL
    inv_freq = 1.0 / (ROPE_THETA ** (jnp.arange(0, QK_ROPE, 2, dtype=F32) / QK_ROPE))
    ang = pos[:, None] * inv_freq[None, :]
    cos, sin = jnp.cos(ang), jnp.sin(ang)
    zero = jnp.zeros((tp, LANES - 2 * half), F32)
    return jnp.concatenate([cos, cos, zero], axis=1), jnp.concatenate([-sin, sin, zero], axis=1)


def kernel(x, meta, ln_in_g, ln_in_b, w_in, b_in, q_norm_g, kv_norm_g, w_uq, w_ukv, conv_w, conv_b, ml_norm_g, w_br_mla, w_br_mlstm, w_out, ln1_g, ln1_b, w_router, e_bias, w1, w3, w2, ws1, ws3, ws2, ln2_g, ln2_b):
    batch, seq, d = x.shape
    depth = w_in.shape[0]
    ql, kvl = q_norm_g.shape[1], kv_norm_g.shape[1]
    assert ql == kvl and seq % LANES == 0
    tp = LANES + seq
    n = batch * tp
    hq = ML_HEADS * ML_QK

    head = jnp.concatenate([jnp.zeros((PADL, d), x.dtype), meta.astype(x.dtype)], axis=0)
    h_f32, h_bf = _layer_norm_in(x, head, ln_in_g, ln_in_b)
    cos_t, sin_t = _rope_tables(tp)
    col_scale = jnp.concatenate([jnp.ones((1, hq), F32), jnp.full((1, hq), ML_QK ** -0.5, F32)], axis=1)

    for l in range(depth):
        w_main, b_main, w_small, b_small, offs = _in_proj_layout(w_in[l], b_in[l], d, ql, kvl)
        o_mq, o_mv, o_mo, o_ga, o_gb = offs[2], offs[4], offs[5], offs[6], offs[7]
        z = _matmul_bias(h_bf, w_main.astype(BF16), b_main, BF16, sig_col=o_mo, name="in_proj")
        zs = _matmul_bias(h_bf, w_small.astype(BF16), b_small, F32, name="in_proj_small")

        q, k, v = _mla_qkv(z, zs, q_norm_g[l], kv_norm_g[l], _uq_layout(w_uq[l]).astype(BF16),
                           _ukv_layout(w_ukv[l]).astype(BF16), cos_t, sin_t, tp, ql)
        attn = _attention(q, k, v, batch, tp)

        assert o_mq % hq == 0 and o_mv % ML_V == 0 and o_mo % ML_V == 0
        qk = _qk_conv(z, conv_w[l], conv_b[l], col_scale, tp, o_mq // hq)
        hm = _mlstm(qk, z, zs, ml_norm_g[l], batch, tp, o_mv // ML_V, o_mo // ML_V)

        y = _merge(attn, hm, w_br_mla[l].astype(BF16), w_br_mlstm[l].astype(BF16), z, o_ga, o_gb)
        h_f32, hs = _outproj_ln(y, w_out[l].astype(BF16), h_f32, ln1_g[l], ln1_b[l])
        outs = _moe(h_f32, hs, w_router[l], e_bias[l], w1, w3, w2, l, ws1[l], ws3[l], ws2[l],
                    ln2_g[l], ln2_b[l], final_shape=(batch, seq) if l == depth - 1 else None)
        if l == depth - 1:
            return outs[0]
        h_f32, h_bf = outs
```

```python
import functools

import numpy as np
import jax
import jax.numpy as jnp
from jax import lax
from jax.experimental import pallas as pl
from jax.experimental.pallas import tpu as pltpu

N_META = 16
MLA_HEADS = 8
QK_NOPE = 128
QK_ROPE = 64
V_HEAD = 128
ROPE_THETA = 10000.0
ML_HEADS = 8
ML_QK = 128
ML_V = 256
CONV_K = 4
CHUNK = 64
N_EXPERTS = 64
TOP_K = 8
N_GROUPS = 8
TOPK_GROUPS = 4
ROUTED_SCALE = 2.5
DEPTH = 2
ALPHA = (2 * DEPTH) ** 0.25
LN_EPS = 1e-5
RMS_EPS = 1e-6
NEG = -1e30

LANES = 128
HEAD_SLAB = 2 * LANES
PADL = LANES - N_META
EXPERT_ROWS = 256
VMEM_LIMIT = 56 * 1024 * 1024

F32 = jnp.float32
BF16 = jnp.bfloat16
U32 = jnp.uint32
NT = (((1,), (1,)), ((), ()))
TN = (((0,), (0,)), ((), ()))


def _pick(n, cands):
    for c in cands:
        if n % c == 0:
            return c
    raise ValueError(f"no tile in {cands} divides {n}")


def _cp(sem, vmem=None):
    return pltpu.CompilerParams(dimension_semantics=sem, vmem_limit_bytes=vmem or VMEM_LIMIT)


def _ln_rows(x, g, b):
    mu = jnp.mean(x, axis=-1, keepdims=True)
    xc = x - mu
    var = jnp.mean(xc * xc, axis=-1, keepdims=True)
    return xc * lax.rsqrt(var + LN_EPS) * g + b


def _rms_rows(x, g):
    return x * lax.rsqrt(jnp.mean(x * x, axis=-1, keepdims=True) + RMS_EPS) * g


def _silu(x):
    return x * jax.nn.sigmoid(x)


def _store_slab(ref, x):
    rows, ns = x.shape[0], x.shape[1] // HEAD_SLAB
    for s in range(ns):
        lo = lax.bitcast_convert_type(x[:, s * HEAD_SLAB:s * HEAD_SLAB + LANES].astype(BF16).astype(F32), U32)
        hi = lax.bitcast_convert_type(x[:, s * HEAD_SLAB + LANES:(s + 1) * HEAD_SLAB].astype(BF16).astype(F32), U32)
        ref[pl.ds(s, rows, stride=ns), :] = (lo >> 16) | hi


def _load_slab_pairs(ref, lead, s, rows, ns):
    u = ref[lead + (pl.ds(s, rows, stride=ns), slice(None))]
    return (lax.bitcast_convert_type(u << 16, F32),
            lax.bitcast_convert_type(u & jnp.uint32(0xFFFF0000), F32))


def _ln_in_kernel(head_ref, x_ref, g_ref, b_ref, of_ref, ob_ref):
    i = pl.program_id(1)

    def emit(src):
        y = _ln_rows(src, g_ref[...], b_ref[...])
        of_ref[...] = y
        ob_ref[...] = y.astype(BF16)

    @pl.when(i == 0)
    def _():
        emit(head_ref[...])

    @pl.when(i > 0)
    def _():
        emit(x_ref[...])


def _layer_norm_in(x, head, g, b):
    batch, seq, d = x.shape
    nt = seq // LANES + 1
    row = pl.BlockSpec((LANES, d), lambda bb, i: (bb * nt + i, 0))
    vec = pl.BlockSpec((1, d), lambda bb, i: (0, 0))
    n = batch * nt * LANES
    return pl.pallas_call(
        _ln_in_kernel,
        grid=(batch, nt),
        in_specs=[pl.BlockSpec((LANES, d), lambda bb, i: (0, 0)),
                  pl.BlockSpec((None, LANES, d), lambda bb, i: (bb, jnp.maximum(i - 1, 0), 0)),
                  vec, vec],
        out_specs=[row, row],
        out_shape=[jax.ShapeDtypeStruct((n, d), F32), jax.ShapeDtypeStruct((n, d), BF16)],
        compiler_params=_cp(("parallel", "arbitrary")),
        name="ln_in",
    )(head, x, g.reshape(1, d), b.reshape(1, d))


def _mm_kernel(x_ref, w_ref, b_ref, o_ref, *, sig_tile):
    acc = jnp.dot(x_ref[...], w_ref[...], preferred_element_type=F32) + b_ref[...]
    if sig_tile is None:
        o_ref[...] = acc.astype(o_ref.dtype)
        return
    j = pl.program_id(0)

    @pl.when(j < sig_tile)
    def _():
        o_ref[...] = acc.astype(o_ref.dtype)

    @pl.when(j >= sig_tile)
    def _():
        o_ref[...] = jax.nn.sigmoid(acc).astype(o_ref.dtype)


def _matmul_bias(x, w, b, out_dtype, sig_col=None, name="mm"):
    n, kd = x.shape
    nc = w.shape[1]
    tm = _pick(n, (1024, 512, 256, 128))
    tn = _pick(nc, (1024, 512, 256, 128))
    sig_tile = None
    if sig_col is not None:
        assert sig_col % tn == 0
        sig_tile = sig_col // tn
    return pl.pallas_call(
        functools.partial(_mm_kernel, sig_tile=sig_tile),
        grid=(nc // tn, n // tm),
        in_specs=[pl.BlockSpec((tm, kd), lambda j, i: (i, 0)),
                  pl.BlockSpec((kd, tn), lambda j, i: (0, j)),
                  pl.BlockSpec((1, tn), lambda j, i: (0, j))],
        out_specs=pl.BlockSpec((tm, tn), lambda j, i: (i, j)),
        out_shape=jax.ShapeDtypeStruct((n, nc), out_dtype),
        compiler_params=_cp(("parallel", "parallel")),
        name=name,
    )(x, w, b.reshape(1, nc))


def _rope_slab(r, cos_t, sin_t):
    return r * cos_t + pltpu.roll(r, 2 * (QK_ROPE // 2), 1) * sin_t


def _qproj_kernel(c_ref, g_ref, w_ref, cos_ref, sin_ref, o_ref, *, heads, scale):
    xn = _rms_rows(c_ref[...].astype(F32), g_ref[...])
    q = jnp.dot(xn.astype(BF16), w_ref[...], preferred_element_type=F32) * scale
    cos_t, sin_t = cos_ref[...], sin_ref[...]
    for h in range(heads):
        lo = h * HEAD_SLAB
        o_ref[:, lo:lo + LANES] = q[:, lo:lo + LANES].astype(BF16)
        o_ref[:, lo + LANES:lo + HEAD_SLAB] = _rope_slab(q[:, lo + LANES:lo + HEAD_SLAB], cos_t, sin_t).astype(BF16)


def _kproj_kernel(c_ref, g_ref, w_ref, kr_ref, cos_ref, sin_ref, k_ref, v_ref, *, heads):
    xn = _rms_rows(c_ref[...].astype(F32), g_ref[...])
    kv = jnp.dot(xn.astype(BF16), w_ref[...], preferred_element_type=F32)
    rr = _rope_slab(kr_ref[...], cos_ref[...], sin_ref[...]).astype(BF16)
    for h in range(heads):
        lo = h * HEAD_SLAB
        k_ref[:, lo:lo + LANES] = kv[:, h * LANES:(h + 1) * LANES].astype(BF16)
        k_ref[:, lo + LANES:lo + HEAD_SLAB] = rr
    v_ref[...] = kv[:, heads * LANES:].astype(BF16)


def _mla_qkv(z, zs, qg, kvg, wq, wkv, cos_t, sin_t, tp, ql):
    n = z.shape[0]
    heads = MLA_HEADS
    tm = _pick(tp, (384, 256, 128))
    nt = tp // tm
    scale = (QK_NOPE + QK_ROPE) ** -0.5
    tab = pl.BlockSpec((tm, LANES), lambda i: (i % nt, 0))
    q = pl.pallas_call(
        functools.partial(_qproj_kernel, heads=heads, scale=scale),
        grid=(n // tm,),
        in_specs=[pl.BlockSpec((tm, ql), lambda i: (i, 0)),
                  pl.BlockSpec((1, ql), lambda i: (0, 0)),
                  pl.BlockSpec((ql, heads * HEAD_SLAB), lambda i: (0, 0)),
                  tab, tab],
        out_specs=pl.BlockSpec((tm, heads * HEAD_SLAB), lambda i: (i, 0)),
        out_shape=jax.ShapeDtypeStruct((n, heads * HEAD_SLAB), BF16),
        compiler_params=_cp(("parallel",)),
        name="mla_qproj",
    )(z, qg.reshape(1, ql), wq, cos_t, sin_t)
    k, v = pl.pallas_call(
        functools.partial(_kproj_kernel, heads=heads),
        grid=(n // tm,),
        in_specs=[pl.BlockSpec((tm, ql), lambda i: (i, 1)),
                  pl.BlockSpec((1, ql), lambda i: (0, 0)),
                  pl.BlockSpec((ql, 2 * heads * LANES), lambda i: (0, 0)),
                  pl.BlockSpec((tm, LANES), lambda i: (i, 0)),
                  tab, tab],
        out_specs=[pl.BlockSpec((tm, heads * HEAD_SLAB), lambda i: (i, 0)),
                   pl.BlockSpec((tm, heads * LANES), lambda i: (i, 0))],
        out_shape=[jax.ShapeDtypeStruct((n, heads * HEAD_SLAB), BF16),
                   jax.ShapeDtypeStruct((n, heads * LANES), BF16)],
        compiler_params=_cp(("parallel",)),
        name="mla_kvproj",
    )(z, kvg.reshape(1, ql), wkv, zs, cos_t, sin_t)
    return q, k, v


ATTN_HEADS_PER_STEP = 2


def _attn_kernel(q_ref, k_ref, v_ref, o_ref, s_sc, m_sc, l_sc, acc_sc, *, t, hps):
    i = pl.program_id(2)
    nl = t // LANES

    def lane_fold(x, fn):
        r = x[:, :LANES]
        for c in range(1, nl):
            r = fn(r, x[:, c * LANES:(c + 1) * LANES])
        return r

    m_sc[...] = jnp.full(m_sc.shape, NEG, F32)

    def scores(j, masked):
        rows = pl.ds(pl.multiple_of(j * t, t), t)
        if masked:
            qpos = i * t + lax.broadcasted_iota(jnp.int32, (t, t), 0)
            kpos = j * t + lax.broadcasted_iota(jnp.int32, (t, t), 1)
            keep = (kpos <= qpos) & (kpos >= PADL)
        for h in range(hps):
            q = q_ref[:, h * HEAD_SLAB:(h + 1) * HEAD_SLAB]
            s = lax.dot_general(q, k_ref[rows, h * HEAD_SLAB:(h + 1) * HEAD_SLAB], NT, preferred_element_type=F32)
            if masked:
                s = jnp.where(keep, s, NEG)
            s_sc[h, j] = s
            m_sc[h] = jnp.maximum(m_sc[h], lane_fold(s, jnp.maximum))

    scores(0, True)

    def score_body(j, carry):
        scores(j, False)
        return carry

    lax.fori_loop(1, i, score_body, 0)

    @pl.when(i > 0)
    def _():
        scores(i, True)

    for h in range(hps):
        m_sc[h] = jnp.broadcast_to(jnp.max(m_sc[h], axis=1, keepdims=True), (t, LANES))
    l_sc[...] = jnp.zeros(l_sc.shape, F32)
    acc_sc[...] = jnp.zeros(acc_sc.shape, F32)

    def pv_body(j, carry):
        rows = pl.ds(pl.multiple_of(j * t, t), t)
        for h in range(hps):
            s = s_sc[h, j]
            mb = m_sc[h]
            p = jnp.concatenate([jnp.exp(s[:, c * LANES:(c + 1) * LANES] - mb) for c in range(nl)], axis=1)
            l_sc[h] = l_sc[h] + lane_fold(p, jnp.add)
            acc_sc[h] = acc_sc[h] + jnp.dot(p.astype(BF16), v_ref[rows, h * V_HEAD:(h + 1) * V_HEAD],
                                            preferred_element_type=F32)
        return carry

    lax.fori_loop(0, i + 1, pv_body, 0)

    for h in range(hps):
        l = jnp.sum(l_sc[h], axis=1, keepdims=True)
        o_ref[:, h * V_HEAD:(h + 1) * V_HEAD] = (acc_sc[h] / l).astype(o_ref.dtype)


def _attention(q, k, v, batch, tp):
    n = q.shape[0]
    heads = MLA_HEADS
    hps = ATTN_HEADS_PER_STEP
    assert heads % hps == 0
    t = _pick(tp, (384, 256, 128))
    nt = tp // t
    return pl.pallas_call(
        functools.partial(_attn_kernel, t=t, hps=hps),
        grid=(batch, heads // hps, nt),
        in_specs=[pl.BlockSpec((t, hps * HEAD_SLAB), lambda b, h, i: (b * nt + i, h)),
                  pl.BlockSpec((tp, hps * HEAD_SLAB), lambda b, h, i: (b, h)),
                  pl.BlockSpec((tp, hps * V_HEAD), lambda b, h, i: (b, h))],
        out_specs=pl.BlockSpec((t, hps * V_HEAD), lambda b, h, i: (b * nt + i, h)),
        out_shape=jax.ShapeDtypeStruct((n, heads * V_HEAD), BF16),
        scratch_shapes=[pltpu.VMEM((hps, nt, t, t), F32), pltpu.VMEM((hps, t, LANES), F32),
                        pltpu.VMEM((hps, t, LANES), F32), pltpu.VMEM((hps, t, V_HEAD), F32)],
        compiler_params=_cp(("parallel", "parallel", "parallel")),
        name="mla_attention",
    )(q, k, v)


HALO = 16


def _conv_kernel(x_ref, halo_ref, w_ref, b_ref, s_ref, o_ref, *, tm, tp):
    i = pl.program_id(0)
    start = lax.rem(i * tm, tp)
    pos = start + lax.broadcasted_iota(jnp.int32, (tm, 1), 0)
    x = jnp.where(pos >= PADL, x_ref[...].astype(F32), 0.0)
    hpos = start - HALO + lax.broadcasted_iota(jnp.int32, (HALO, 1), 0)
    halo = jnp.where(hpos >= PADL, halo_ref[...].astype(F32), 0.0)
    ext = jnp.concatenate([halo, x], axis=0)
    w = w_ref[...]
    y = b_ref[...] + w[CONV_K - 1:CONV_K, :] * x
    for j in range(1, CONV_K):
        y = y + w[CONV_K - 1 - j:CONV_K - j, :] * pltpu.roll(ext, j, 0)[HALO:, :]
    o_ref[...] = (_silu(y) * s_ref[...]).astype(BF16)


def _qk_conv(z, conv_w, conv_b, col_scale, tp, off_blk):
    n = z.shape[0]
    cw = conv_w.shape[1] // 2
    tm = _pick(tp, (384, 256, 128))
    return pl.pallas_call(
        functools.partial(_conv_kernel, tm=tm, tp=tp),
        grid=(n // tm, 2),
        in_specs=[pl.BlockSpec((tm, cw), lambda i, c: (i, off_blk + c)),
                  pl.BlockSpec((HALO, cw), lambda i, c: (jnp.maximum(i * (tm // HALO) - 1, 0), off_blk + c)),
                  pl.BlockSpec((CONV_K, cw), lambda i, c: (0, c)),
                  pl.BlockSpec((1, cw), lambda i, c: (0, c)),
                  pl.BlockSpec((1, cw), lambda i, c: (0, c))],
        out_specs=pl.BlockSpec((tm, cw), lambda i, c: (i, c)),
        out_shape=jax.ShapeDtypeStruct((n, 2 * cw), BF16),
        compiler_params=_cp(("parallel", "parallel")),
        name="mlstm_qk_conv",
    )(z, z, conv_w, conv_b.reshape(1, -1), col_scale)


MLSTM_HEADS_PER_STEP = 2


def _mlstm_kernel(*refs, group, heads, hps):
    q_refs, k_refs, v_refs = refs[:hps], refs[hps:2 * hps], refs[2 * hps:3 * hps]
    gt_ref = refs[3 * hps]
    og_refs = refs[3 * hps + 1:4 * hps + 1]
    ng_ref, o_ref, c_sc, n_sc, m_sc = refs[4 * hps + 1:]
    hp = pl.program_id(1)
    g = pl.program_id(2)
    L = CHUNK

    @pl.when(g == 0)
    def _():
        c_sc[...] = jnp.zeros(c_sc.shape, F32)
        n_sc[...] = jnp.zeros(n_sc.shape, F32)
        m_sc[...] = jnp.zeros(m_sc.shape, F32)

    R = group * L
    lane = lax.broadcasted_iota(jnp.int32, (R, LANES), 1)
    r_i = lax.broadcasted_iota(jnp.int32, (L, L), 0)
    c_i = lax.broadcasted_iota(jnp.int32, (L, L), 1)
    eye = r_i == c_i
    causal = c_i <= r_i

    gt = gt_ref[...]
    pad = (g * R + lax.broadcasted_iota(jnp.int32, (R, 1), 0)) < PADL
    i_parts, lf_parts = [], []
    for hh in range(hps):
        h = hp * hps + hh
        i_all = jnp.sum(jnp.where(lane == h, gt, 0.0), axis=1, keepdims=True)
        f_all = jnp.sum(jnp.where(lane == heads + h, gt, 0.0), axis=1, keepdims=True)
        i_parts.append(jnp.where(pad, NEG, i_all).reshape(group, L, 1))
        lf_parts.append(jnp.where(pad, 0.0, -(jnp.maximum(-f_all, 0.0) + jnp.log1p(jnp.exp(-jnp.abs(f_all)))))
                        .reshape(group, L, 1))
    i_col = jnp.concatenate(i_parts, axis=0)
    lf_col = jnp.concatenate(lf_parts, axis=0)
    b_row = jnp.sum(jnp.where((r_i <= c_i)[None], lf_col, 0.0), axis=1, keepdims=True)
    b_col = jnp.sum(jnp.where(eye[None], b_row, 0.0), axis=2, keepdims=True)
    i_row = jnp.sum(jnp.where(eye[None], i_col, 0.0), axis=1, keepdims=True)
    gtot = jnp.sum(lf_col, axis=1, keepdims=True)
    dmat = jnp.where(causal[None], b_col - b_row + i_row, NEG)
    rmax = jnp.max(dmat, axis=2, keepdims=True)
    a_col = gtot - b_col + i_col
    amax = jnp.max(a_col, axis=1, keepdims=True)

    m_prev_parts, m_new_parts = [], []
    for hh in range(hps):
        m = m_sc[hh].reshape(1, 1, 1)
        for c in range(group):
            idx = hh * group + c
            m_prev_parts.append(m)
            m = jnp.maximum(gtot[idx:idx + 1] + m, amax[idx:idx + 1])
            m_new_parts.append(m)
        m_sc[hh] = m.reshape(1, 1)
    m_prev = jnp.concatenate(m_prev_parts, axis=0)
    m_new = jnp.concatenate(m_new_parts, axis=0)
    inter = b_col + m_prev
    mrow = jnp.maximum(rmax, inter)
    dexp = jnp.exp(dmat - mrow)
    e_in = jnp.exp(inter - mrow)
    floor = jnp.exp(-mrow)
    ea = jnp.exp(a_col - m_new)
    decay = jnp.exp(gtot + m_prev - m_new)

    for hh in range(hps):
        cmat, nvec = c_sc[hh], n_sc[hh]
        for c in range(group):
            idx = hh * group + c
            rows = slice(c * L, (c + 1) * L)
            q = q_refs[hh][rows, :]
            k = k_refs[hh][rows, :]
            v = v_refs[hh][rows, :]
            s = lax.dot_general(q, k, NT, preferred_element_type=F32) * dexp[idx]
            num = (jnp.dot(s.astype(BF16), v, preferred_element_type=F32)
                   + e_in[idx] * jnp.dot(q, cmat.astype(BF16), preferred_element_type=F32))
            den = (jnp.sum(s, axis=1, keepdims=True)
                   + e_in[idx] * jnp.sum(q.astype(F32) * nvec, axis=1, keepdims=True))
            hc = num / jnp.maximum(jnp.abs(den), floor[idx])
            ks = k.astype(F32) * ea[idx]
            cmat = decay[idx] * cmat + lax.dot_general(ks.astype(BF16), v, TN, preferred_element_type=F32)
            nvec = decay[idx] * nvec + jnp.sum(ks, axis=0, keepdims=True)
            hn = _rms_rows(hc, ng_ref[:, hh * ML_V:(hh + 1) * ML_V])
            o_ref[rows, hh * ML_V:(hh + 1) * ML_V] = (hn * og_refs[hh][rows, :].astype(F32)).astype(BF16)
        c_sc[hh], n_sc[hh] = cmat, nvec


def _mlstm(qk, z, zs, norm_g, batch, tp, v_blk, og_blk):
    n = qk.shape[0]
    heads = ML_HEADS
    hps = MLSTM_HEADS_PER_STEP
    assert heads % hps == 0
    nchunks = tp // CHUNK
    group = _pick(nchunks, (6, 5, 4, 3, 2, 1))
    rows = group * CHUNK
    ng = nchunks // group

    def col(width, base):
        return [pl.BlockSpec((rows, width), lambda b, p, g, u=u: (b * ng + g, base + p * hps + u)) for u in range(hps)]

    return pl.pallas_call(
        functools.partial(_mlstm_kernel, group=group, heads=heads, hps=hps),
        grid=(batch, heads // hps, ng),
        in_specs=(col(ML_QK, 0) + col(ML_QK, heads) + col(ML_V, v_blk)
                  + [pl.BlockSpec((rows, LANES), lambda b, p, g: (b * ng + g, 1))]
                  + col(ML_V, og_blk)
                  + [pl.BlockSpec((1, hps * ML_V), lambda b, p, g: (0, p))]),
        out_specs=pl.BlockSpec((rows, hps * ML_V), lambda b, p, g: (b * ng + g, p)),
        out_shape=jax.ShapeDtypeStruct((n, heads * ML_V), BF16),
        scratch_shapes=[pltpu.VMEM((hps, ML_QK, ML_V), F32), pltpu.VMEM((hps, 1, ML_QK), F32),
                        pltpu.VMEM((hps, 1, 1), F32)],
        compiler_params=_cp(("parallel", "parallel", "arbitrary")),
        name="mlstm_scan",
    )(*([qk] * (2 * hps) + [z] * hps + [zs] + [z] * hps + [norm_g.reshape(1, -1)]))


def _merge_kernel(a_ref, hm_ref, wa_ref, wb_ref, ga_ref, gb_ref, o_ref):
    ya = jnp.dot(a_ref[...], wa_ref[...], preferred_element_type=F32)
    yb = jnp.dot(hm_ref[...], wb_ref[...], preferred_element_type=F32)
    o_ref[...] = (ga_ref[...].astype(F32) * ya + gb_ref[...].astype(F32) * yb).astype(BF16)


def _merge(attn, hm, wa, wb, z, ga_off, gb_off):
    n = attn.shape[0]
    d = wa.shape[1]
    tm = _pick(n, (512, 256, 128))
    tn = _pick(d, (1024, 512, 256, 128))
    assert ga_off % tn == 0 and gb_off % tn == 0
    ga_blk, gb_blk = ga_off // tn, gb_off // tn
    return pl.pallas_call(
        _merge_kernel,
        grid=(d // tn, n // tm),
        in_specs=[pl.BlockSpec((tm, attn.shape[1]), lambda j, i: (i, 0)),
                  pl.BlockSpec((tm, hm.shape[1]), lambda j, i: (i, 0)),
                  pl.BlockSpec((wa.shape[0], tn), lambda j, i: (0, j)),
                  pl.BlockSpec((wb.shape[0], tn), lambda j, i: (0, j)),
                  pl.BlockSpec((tm, tn), lambda j, i: (i, ga_blk + j)),
                  pl.BlockSpec((tm, tn), lambda j, i: (i, gb_blk + j))],
        out_specs=pl.BlockSpec((tm, tn), lambda j, i: (i, j)),
        out_shape=jax.ShapeDtypeStruct((n, d), BF16),
        compiler_params=_cp(("parallel", "parallel")),
        name="branch_merge",
    )(attn, hm, wa, wb, z, z)


def _outln_kernel(y_ref, w_ref, h_ref, g_ref, b_ref, of_ref, os_ref):
    o = jnp.dot(y_ref[...], w_ref[...], preferred_element_type=F32)
    r = _ln_rows(ALPHA * h_ref[...] + o, g_ref[...], b_ref[...])
    of_ref[...] = r
    _store_slab(os_ref, r)


def _outproj_ln(y, w, h, g, b):
    n, d = h.shape
    tm = _pick(n, (256, 128))
    ns = d // HEAD_SLAB
    row = pl.BlockSpec((tm, d), lambda i: (i, 0))
    vec = pl.BlockSpec((1, d), lambda i: (0, 0))
    return pl.pallas_call(
        _outln_kernel,
        grid=(n // tm,),
        in_specs=[row, pl.BlockSpec((d, d), lambda i: (0, 0)), row, vec, vec],
        out_specs=[row, pl.BlockSpec((tm * ns, LANES), lambda i: (i, 0))],
        out_shape=[jax.ShapeDtypeStruct((n, d), F32), jax.ShapeDtypeStruct((n * ns, LANES), U32)],
        compiler_params=_cp(("parallel",)),
        name="outproj_ln",
    )(y, w, h, g.reshape(1, d), b.reshape(1, d))


def _router_kernel(x_ref, wh_ref, wl_ref, eb_ref, idx_ref, wt_ref, rank_ref, cnt_ref, base_sc, *, tm):
    i = pl.program_id(0)
    E, G = N_EXPERTS, N_GROUPS
    per = E // G

    @pl.when(i == 0)
    def _():
        base_sc[...] = jnp.zeros(base_sc.shape, F32)

    x = x_ref[...]
    xh = x.astype(BF16)
    xl = (x - xh.astype(F32)).astype(BF16)
    wh, wl = wh_ref[...], wl_ref[...]
    logits = (lax.dot_general(wh, xh, NT, preferred_element_type=F32)
              + lax.dot_general(wh, xl, NT, preferred_element_type=F32)
              + lax.dot_general(wl, xh, NT, preferred_element_type=F32))
    scores = jax.nn.sigmoid(logits)
    biased = scores + eb_ref[...]
    s3 = scores.reshape(G, per, tm)
    b3 = biased.reshape(G, per, tm)
    j_io = lax.broadcasted_iota(jnp.int32, (G, per, tm), 1).astype(F32)
    g_io3 = lax.broadcasted_iota(jnp.int32, (G, per, tm), 0).astype(F32)
    e_io = g_io3 * per + j_io
    g_io = lax.broadcasted_iota(jnp.int32, (G, 1, tm), 0).astype(F32)
    ninf = -jnp.inf

    m1 = jnp.max(b3, axis=1, keepdims=True)
    i1 = jnp.min(jnp.where(b3 == m1, j_io, float(per)), axis=1, keepdims=True)
    m2 = jnp.max(jnp.where(j_io == i1, ninf, b3), axis=1, keepdims=True)
    gs = m1 + m2
    gsel = jnp.zeros((G, 1, tm), F32)
    for _ in range(TOPK_GROUPS):
        gm = jnp.max(gs, axis=0, keepdims=True)
        gi = jnp.min(jnp.where(gs == gm, g_io, float(G)), axis=0, keepdims=True)
        hit = g_io == gi
        gsel = jnp.where(hit, 1.0, gsel)
        gs = jnp.where(hit, ninf, gs)
    masked = jnp.where(gsel > 0.0, b3, ninf)

    def red2(fn, a):
        return fn(fn(a, axis=1, keepdims=True), axis=0, keepdims=True)

    sel = jnp.zeros((G, per, tm), F32)
    idxs, scs = [], []
    for _ in range(TOP_K):
        mx = red2(jnp.max, masked)
        ei = red2(jnp.min, jnp.where(masked == mx, e_io, float(E)))
        hit = e_io == ei
        scs.append(red2(jnp.sum, jnp.where(hit, s3, 0.0)))
        idxs.append(ei)
        sel = jnp.where(hit, 1.0, sel)
        masked = jnp.where(hit, ninf, masked)
    wsum = scs[0]
    for k in range(1, TOP_K):
        wsum = wsum + scs[k]

    sel2 = sel.reshape(E, tm)
    upper = (lax.broadcasted_iota(jnp.int32, (tm, tm), 0) < lax.broadcasted_iota(jnp.int32, (tm, tm), 1))
    excl = jnp.dot(sel2.astype(BF16), upper.astype(BF16), preferred_element_type=F32)
    cnt3 = (excl + base_sc[...]).reshape(G, per, tm)
    for k in range(TOP_K):
        hit = e_io == idxs[k]
        rank = red2(jnp.sum, jnp.where(hit, cnt3, 0.0))
        idx_ref[k:k + 1, :] = idxs[k].reshape(1, tm).astype(jnp.int32)
        wt_ref[k:k + 1, :] = (scs[k] / wsum * ROUTED_SCALE).reshape(1, tm)
        rank_ref[k:k + 1, :] = rank.reshape(1, tm).astype(jnp.int32)
    base_sc[...] = base_sc[...] + jnp.sum(sel2, axis=1, keepdims=True)
    cnt_ref[...] = jnp.broadcast_to(base_sc[...], cnt_ref.shape).astype(jnp.int32)


def _router(h, wr_hi, wr_lo, e_bias):
    n, d = h.shape
    tm = _pick(n, (512, 256, 128))
    outk = pl.BlockSpec((TOP_K, tm), lambda i: (0, i))
    return pl.pallas_call(
        functools.partial(_router_kernel, tm=tm),
        grid=(n // tm,),
        in_specs=[pl.BlockSpec((tm, d), lambda i: (i, 0)),
                  pl.BlockSpec((N_EXPERTS, d), lambda i: (0, 0)),
                  pl.BlockSpec((N_EXPERTS, d), lambda i: (0, 0)),
                  pl.BlockSpec((N_EXPERTS, 1), lambda i: (0, 0))],
        out_specs=[outk, outk, outk, pl.BlockSpec((N_EXPERTS, LANES), lambda i: (0, 0))],
        out_shape=[jax.ShapeDtypeStruct((TOP_K, n), jnp.int32),
                   jax.ShapeDtypeStruct((TOP_K, n), F32),
                   jax.ShapeDtypeStruct((TOP_K, n), jnp.int32),
                   jax.ShapeDtypeStruct((N_EXPERTS, LANES), jnp.int32)],
        scratch_shapes=[pltpu.VMEM((N_EXPERTS, 1), F32)],
        compiler_params=_cp(("arbitrary",)),
        name="moe_router",
    )(h, wr_hi, wr_lo, e_bias.reshape(N_EXPERTS, 1))


def _row_copy(src, dst, sem):
    return pltpu.make_async_copy(src, dst, sem)


TOK_ROWS = 8
DMA_UNROLL = 8


def _slab_rows(i, ns):
    return pl.ds(pl.multiple_of(i * ns, ns), ns)


def _expert_kernel(be_ref, nu_ref, tok_cur, tok_nxt, h_ref, w1_ref, w3_ref, w2_ref, o_ref,
                   xbuf, w1b, w3b, w2b, sem, *, tm, ns):
    j = pl.program_id(0)
    n_used = nu_ref[0]
    slot = lax.rem(j, 2)

    def row_copy(tok_ref, row, r, dst_slot):
        t = tok_ref[row, r]
        return _row_copy(h_ref.at[_slab_rows(t, ns), :], xbuf.at[dst_slot, _slab_rows(r, ns), :], sem.at[dst_slot])

    def gather_loop(tok_ref, blk, dst_slot, wait):
        row = lax.rem(blk, TOK_ROWS)

        def body(g, carry):
            for u in range(DMA_UNROLL):
                cp = row_copy(tok_ref, row, g * DMA_UNROLL + u, dst_slot)
                if wait:
                    cp.wait()
                else:
                    cp.start(priority=u % 2)
            return carry

        lax.fori_loop(0, tm // DMA_UNROLL, body, 0)

    @pl.when(j == 0)
    def _():
        gather_loop(tok_cur, j, slot, False)

    @pl.when(j == n_used)
    def _():
        gather_loop(tok_cur, j, slot, True)

    @pl.when(j < n_used)
    def _():
        @pl.when((j == 0) | (be_ref[j] != be_ref[jnp.maximum(j - 1, 0)]))
        def _():
            w1b[...] = w1_ref[...].astype(BF16)
            w3b[...] = w3_ref[...].astype(BF16)
            w2b[...] = w2_ref[...].astype(BF16)

        gather_loop(tok_cur, j, slot, True)
        nxt_row = lax.rem(j + 1, TOK_ROWS)
        for r in range(tm):
            row_copy(tok_nxt, nxt_row, r, 1 - slot).start(priority=r % 2)
        parts = []
        for s in range(ns):
            lo, hi = _load_slab_pairs(xbuf, (slot,), s, tm, ns)
            parts += [lo.astype(BF16), hi.astype(BF16)]
        x = jnp.concatenate(parts, axis=1)
        a = jnp.dot(x, w1b[...], preferred_element_type=F32)
        b = jnp.dot(x, w3b[...], preferred_element_type=F32)
        hb = (_silu(a) * b).astype(BF16)
        _store_slab(o_ref, jnp.dot(hb, w2b[...], preferred_element_type=F32))

    @pl.when(j >= n_used)
    def _():
        o_ref[...] = jnp.zeros(o_ref.shape, U32)


def _experts(hs, tok_of_slot, blk_e, n_used, w1, w3, w2, layer):
    d, f = w1.shape[2], w1.shape[3]
    ns = d // HEAD_SLAB
    nb, tm = tok_of_slot.shape
    assert nb % TOK_ROWS == 0
    grid_spec = pltpu.PrefetchScalarGridSpec(
        num_scalar_prefetch=2,
        grid=(nb,),
        in_specs=[pl.BlockSpec((TOK_ROWS, tm), lambda j, be, nu: (j // TOK_ROWS, 0), memory_space=pltpu.SMEM),
                  pl.BlockSpec((TOK_ROWS, tm), lambda j, be, nu: (jnp.minimum(j + 1, nb - 1) // TOK_ROWS, 0),
                               memory_space=pltpu.SMEM),
                  pl.BlockSpec(memory_space=pl.ANY),
                  pl.BlockSpec((None, None, d, f), lambda j, be, nu: (layer, be[j], 0, 0)),
                  pl.BlockSpec((None, None, d, f), lambda j, be, nu: (layer, be[j], 0, 0)),
                  pl.BlockSpec((None, None, f, d), lambda j, be, nu: (layer, be[j], 0, 0))],
        out_specs=pl.BlockSpec((tm * ns, LANES), lambda j, be, nu: (j, 0)),
        scratch_shapes=[pltpu.VMEM((2, tm * ns, LANES), U32),
                        pltpu.VMEM((d, f), BF16), pltpu.VMEM((d, f), BF16), pltpu.VMEM((f, d), BF16),
                        pltpu.SemaphoreType.DMA((2,))],
    )
    return pl.pallas_call(
        functools.partial(_expert_kernel, tm=tm, ns=ns),
        grid_spec=grid_spec,
        out_shape=jax.ShapeDtypeStruct((nb * tm * ns, LANES), U32),
        compiler_params=_cp(("arbitrary",)),
        name="moe_experts",
    )(blk_e, n_used, tok_of_slot, tok_of_slot, hs, w1, w3, w2)


def _combine_kernel(dest_ref, dnxt_ref, h_ref, wt_ref, ws1_ref, ws3_ref, ws2_ref, g_ref, b_ref, ys_ref,
                    of_ref, *rest, tm, ns, nsteps):
    ob_ref = rest[0] if len(rest) == 3 else None
    gbuf, sem = rest[-2:]
    i = pl.program_id(0)
    slot = lax.rem(i, 2)

    def copy(d_ref, k, r, dst_slot):
        d = d_ref[k, r]
        return _row_copy(ys_ref.at[_slab_rows(d, ns), :], gbuf.at[dst_slot, k, _slab_rows(r, ns), :],
                         sem.at[dst_slot])

    def loop(d_ref, dst_slot, wait):
        def body(r, carry):
            for k in range(TOP_K):
                cp = copy(d_ref, k, r, dst_slot)
                if wait:
                    cp.wait()
                else:
                    cp.start(priority=k % 2)
            return carry

        lax.fori_loop(0, tm, body, 0)

    @pl.when(i == 0)
    def _():
        loop(dest_ref, slot, False)

    loop(dest_ref, slot, True)
    for r in range(tm):
        for k in range(TOP_K):
            copy(dnxt_ref, k, r, 1 - slot).start(priority=k % 2)
    h = h_ref[...]
    xb = h.astype(BF16)
    a = jnp.dot(xb, ws1_ref[...], preferred_element_type=F32)
    b = jnp.dot(xb, ws3_ref[...], preferred_element_type=F32)
    shared = jnp.dot((_silu(a) * b).astype(BF16), ws2_ref[...], preferred_element_type=F32)
    wt = wt_ref[...]
    parts = []
    for s in range(ns):
        lo = hi = None
        for k in range(TOP_K):
            plo, phi = _load_slab_pairs(gbuf, (slot, k), s, tm, ns)
            w = wt[:, k:k + 1]
            lo = w * plo if lo is None else lo + w * plo
            hi = w * phi if hi is None else hi + w * phi
        parts += [lo, hi]
    acc = shared + jnp.concatenate(parts, axis=1)
    r = _ln_rows(ALPHA * h + acc, g_ref[...], b_ref[...])
    of_ref[...] = r
    if ob_ref is not None:
        ob_ref[...] = r.astype(BF16)

    @pl.when(i == nsteps - 1)
    def _():
        loop(dnxt_ref, 1 - slot, True)


def _combine(h, dest, wt_tok, ws1, ws3, ws2, g, b, ys, final_shape=None):
    n, d = h.shape
    f = ws1.shape[1]
    tm = LANES
    row = pl.BlockSpec((tm, d), lambda i: (i, 0))
    vec = pl.BlockSpec((1, d), lambda i: (0, 0))
    if final_shape is None:
        out_specs = [row, row]
        out_shape = [jax.ShapeDtypeStruct((n, d), F32), jax.ShapeDtypeStruct((n, d), BF16)]
    else:
        batch, seq = final_shape
        nt = seq // tm + 1
        out_specs = [pl.BlockSpec((None, tm, d), lambda i: (i // nt, jnp.maximum(i % nt - 1, 0), 0))]
        out_shape = [jax.ShapeDtypeStruct((batch, seq, d), F32)]
    ns = d // HEAD_SLAB
    nsteps = n // tm
    return pl.pallas_call(
        functools.partial(_combine_kernel, tm=tm, ns=ns, nsteps=nsteps),
        grid=(nsteps,),
        in_specs=[pl.BlockSpec((TOP_K, tm), lambda i: (0, i), memory_space=pltpu.SMEM),
                  pl.BlockSpec((TOP_K, tm), lambda i: (0, jnp.minimum(i + 1, nsteps - 1)), memory_space=pltpu.SMEM),
                  row,
                  pl.BlockSpec((tm, TOP_K), lambda i: (i, 0)),
                  pl.BlockSpec((d, f), lambda i: (0, 0)),
                  pl.BlockSpec((d, f), lambda i: (0, 0)),
                  pl.BlockSpec((f, d), lambda i: (0, 0)),
                  vec, vec,
                  pl.BlockSpec(memory_space=pl.ANY)],
        out_specs=out_specs,
        out_shape=out_shape,
        scratch_shapes=[pltpu.VMEM((2, TOP_K, tm * ns, LANES), U32), pltpu.SemaphoreType.DMA((2,))],
        compiler_params=_cp(("arbitrary",)),
        name="moe_combine",
    )(dest, dest, h, wt_tok, ws1, ws3, ws2, g.reshape(1, d), b.reshape(1, d), ys)


def _moe(h_f32, hs, w_router, e_bias, w1, w3, w2, layer, ws1, ws3, ws2, g, b, final_shape=None):
    n = h_f32.shape[0]
    wr_t = w_router.T
    wr_hi = wr_t.astype(BF16)
    wr_lo = (wr_t - wr_hi.astype(F32)).astype(BF16)
    idx, wts, rank, cnt = _router(h_f32, wr_hi, wr_lo, e_bias)
    counts = cnt[:, 0]
    blk = EXPERT_ROWS
    pcounts = (counts + blk - 1) // blk * blk
    pends = jnp.cumsum(pcounts)
    pstarts = pends - pcounts
    e_ids = jnp.arange(N_EXPERTS, dtype=jnp.int32)
    start_of = jnp.sum(jnp.where(idx[None] == e_ids[:, None, None], pstarts[:, None, None], 0), axis=0)
    dest = (start_of + rank).astype(jnp.int32)
    nb = n * TOP_K // blk + N_EXPERTS
    blk_e = jnp.minimum(jnp.sum(pends[None, :] <= (jnp.arange(nb, dtype=jnp.int32) * blk)[:, None], axis=1),
                        N_EXPERTS - 1).astype(jnp.int32)
    n_used = (pends[-1:] // blk).astype(jnp.int32)
    tok = jnp.broadcast_to(jnp.arange(n, dtype=jnp.int32)[None], (TOP_K, n))
    tok_of_slot = jnp.zeros((nb * blk,), jnp.int32).at[dest.reshape(-1)].set(
        tok.reshape(-1), unique_indices=True).reshape(nb, blk)
    ys = _experts(hs, tok_of_slot, blk_e, n_used, w1, w3, w2, layer)
    return _combine(h_f32, dest, wts.T, ws1.astype(BF16), ws3.astype(BF16), ws2.astype(BF16), g, b, ys,
                    final_shape)


def _in_proj_layout(w_in, b_in, d, ql, kvl):
    hq, hv = ML_HEADS * ML_QK, ML_HEADS * ML_V
    sizes = (ql, kvl, QK_ROPE, hq, hq, hv, hv, ML_HEADS, ML_HEADS, d, d)
    offs = np.concatenate([[0], np.cumsum(sizes)])
    seg = lambda a, i: a[..., int(offs[i]):int(offs[i + 1])]
    order = (0, 1, 3, 4, 5, 6, 9, 10)
    w_main = jnp.concatenate([seg(w_in, i) for i in order], axis=-1)
    b_main = jnp.concatenate([seg(b_in, i) for i in order], axis=-1)
    half = QK_ROPE // 2

    def small(a):
        kr = seg(a, 2)
        x1, x2 = kr[..., :half], kr[..., half:]
        pad = jnp.zeros(a.shape[:-1] + (LANES - 2 * ML_HEADS,), a.dtype)
        return jnp.concatenate([x1, x2, x2, x1, seg(a, 7), seg(a, 8), pad], axis=-1)

    main_offs = np.concatenate([[0], np.cumsum([sizes[i] for i in order])])
    return w_main, b_main, small(w_in), small(b_in), [int(o) for o in main_offs]


def _uq_layout(w_uq):
    kq = w_uq.shape[0]
    w = w_uq.reshape(kq, MLA_HEADS, QK_NOPE + QK_ROPE)
    half = QK_ROPE // 2
    nope, x1, x2 = w[..., :QK_NOPE], w[..., QK_NOPE:QK_NOPE + half], w[..., QK_NOPE + half:]
    return jnp.concatenate([nope, x1, x2, x2, x1], axis=-1).reshape(kq, MLA_HEADS * HEAD_SLAB)


def _ukv_layout(w_ukv):
    kk = w_ukv.shape[0]
    w = w_ukv.reshape(kk, MLA_HEADS, QK_NOPE + V_HEAD)
    return jnp.concatenate([w[..., :QK_NOPE].reshape(kk, -1), w[..., QK_NOPE:].reshape(kk, -1)], axis=-1)


def _rope_tables(tp):
    half = QK_ROPE // 2
    pos = jnp.arange(tp, dtype=F32) - PADL
    inv_freq = 1.0 / (ROPE_THETA ** (jnp.arange(0, QK_ROPE, 2, dtype=F32) / QK_ROPE))
    ang = pos[:, None] * inv_freq[None, :]
    cos, sin = jnp.cos(ang), jnp.sin(ang)
    zero = jnp.zeros((tp, LANES - 2 * half), F32)
    return jnp.concatenate([cos, cos, zero], axis=1), jnp.concatenate([-sin, sin, zero], axis=1)


def kernel(x, meta, ln_in_g, ln_in_b, w_in, b_in, q_norm_g, kv_norm_g, w_uq, w_ukv, conv_w, conv_b, ml_norm_g, w_br_mla, w_br_mlstm, w_out, ln1_g, ln1_b, w_router, e_bias, w1, w3, w2, ws1, ws3, ws2, ln2_g, ln2_b):
    batch, seq, d = x.shape
    depth = w_in.shape[0]
    ql, kvl = q_norm_g.shape[1], kv_norm_g.shape[1]
    assert ql == kvl and seq % LANES == 0
    tp = LANES + seq
    n = batch * tp
    hq = ML_HEADS * ML_QK

    head = jnp.concatenate([jnp.zeros((PADL, d), x.dtype), meta.astype(x.dtype)], axis=0)
    h_f32, h_bf = _layer_norm_in(x, head, ln_in_g, ln_in_b)
    cos_t, sin_t = _rope_tables(tp)
    col_scale = jnp.concatenate([jnp.ones((1, hq), F32), jnp.full((1, hq), ML_QK ** -0.5, F32)], axis=1)

    for l in range(depth):
        w_main, b_main, w_small, b_small, offs = _in_proj_layout(w_in[l], b_in[l], d, ql, kvl)
        o_mq, o_mv, o_mo, o_ga, o_gb = offs[2], offs[4], offs[5], offs[6], offs[7]
        z = _matmul_bias(h_bf, w_main.astype(BF16), b_main, BF16, sig_col=o_mo, name="in_proj")
        zs = _matmul_bias(h_bf, w_small.astype(BF16), b_small, F32, name="in_proj_small")

        q, k, v = _mla_qkv(z, zs, q_norm_g[l], kv_norm_g[l], _uq_layout(w_uq[l]).astype(BF16),
                           _ukv_layout(w_ukv[l]).astype(BF16), cos_t, sin_t, tp, ql)
        attn = _attention(q, k, v, batch, tp)

        assert o_mq % hq == 0 and o_mv % ML_V == 0 and o_mo % ML_V == 0
        qk = _qk_conv(z, conv_w[l], conv_b[l], col_scale, tp, o_mq // hq)
        hm = _mlstm(qk, z, zs, ml_norm_g[l], batch, tp, o_mv // ML_V, o_mo // ML_V)

        y = _merge(attn, hm, w_br_mla[l].astype(BF16), w_br_mlstm[l].astype(BF16), z, o_ga, o_gb)
        h_f32, hs = _outproj_ln(y, w_out[l].astype(BF16), h_f32, ln1_g[l], ln1_b[l])
        outs = _moe(h_f32, hs, w_router[l], e_bias[l], w1, w3, w2, l, ws1[l], ws3[l], ws2[l],
                    ln2_g[l], ln2_b[l], final_shape=(batch, seq) if l == depth - 1 else None)
        if l == depth - 1:
            return outs[0]
        h_f32, h_bf = outs
```

```python
import functools

import numpy as np
import jax
import jax.numpy as jnp
from jax import lax
from jax.experimental import pallas as pl
from jax.experimental.pallas import tpu as pltpu

N_META = 16
MLA_HEADS = 8
QK_NOPE = 128
QK_ROPE = 64
V_HEAD = 128
ROPE_THETA = 10000.0
ML_HEADS = 8
ML_QK = 128
ML_V = 256
CONV_K = 4
CHUNK = 64
N_EXPERTS = 64
TOP_K = 8
N_GROUPS = 8
TOPK_GROUPS = 4
ROUTED_SCALE = 2.5
DEPTH = 2
ALPHA = (2 * DEPTH) ** 0.25
LN_EPS = 1e-5
RMS_EPS = 1e-6
NEG = -1e30

LANES = 128
HEAD_SLAB = 2 * LANES
PADL = LANES - N_META
EXPERT_ROWS = 512
VMEM_LIMIT = 56 * 1024 * 1024

F32 = jnp.float32
BF16 = jnp.bfloat16
U32 = jnp.uint32
NT = (((1,), (1,)), ((), ()))
TN = (((0,), (0,)), ((), ()))


def _pick(n, cands):
    for c in cands:
        if n % c == 0:
            return c
    raise ValueError(f"no tile in {cands} divides {n}")


def _cp(sem, vmem=None):
    return pltpu.CompilerParams(dimension_semantics=sem, vmem_limit_bytes=vmem or VMEM_LIMIT)


def _ln_rows(x, g, b):
    mu = jnp.mean(x, axis=-1, keepdims=True)
    xc = x - mu
    var = jnp.mean(xc * xc, axis=-1, keepdims=True)
    return xc * lax.rsqrt(var + LN_EPS) * g + b


def _rms_rows(x, g):
    return x * lax.rsqrt(jnp.mean(x * x, axis=-1, keepdims=True) + RMS_EPS) * g


def _silu(x):
    return x * jax.nn.sigmoid(x)


def _store_slab(ref, x):
    rows, ns = x.shape[0], x.shape[1] // HEAD_SLAB
    for s in range(ns):
        lo = lax.bitcast_convert_type(x[:, s * HEAD_SLAB:s * HEAD_SLAB + LANES].astype(BF16).astype(F32), U32)
        hi = lax.bitcast_convert_type(x[:, s * HEAD_SLAB + LANES:(s + 1) * HEAD_SLAB].astype(BF16).astype(F32), U32)
        ref[pl.ds(s, rows, stride=ns), :] = (lo >> 16) | hi


def _load_slab_pairs(ref, lead, s, rows, ns):
    u = ref[lead + (pl.ds(s, rows, stride=ns), slice(None))]
    return (lax.bitcast_convert_type(u << 16, F32),
            lax.bitcast_convert_type(u & jnp.uint32(0xFFFF0000), F32))


def _ln_in_kernel(head_ref, x_ref, g_ref, b_ref, of_ref, ob_ref):
    i = pl.program_id(1)

    def emit(src):
        y = _ln_rows(src, g_ref[...], b_ref[...])
        of_ref[...] = y
        ob_ref[...] = y.astype(BF16)

    @pl.when(i == 0)
    def _():
        emit(head_ref[...])

    @pl.when(i > 0)
    def _():
        emit(x_ref[...])


def _layer_norm_in(x, head, g, b):
    batch, seq, d = x.shape
    nt = seq // LANES + 1
    row = pl.BlockSpec((LANES, d), lambda bb, i: (bb * nt + i, 0))
    vec = pl.BlockSpec((1, d), lambda bb, i: (0, 0))
    n = batch * nt * LANES
    return pl.pallas_call(
        _ln_in_kernel,
        grid=(batch, nt),
        in_specs=[pl.BlockSpec((LANES, d), lambda bb, i: (0, 0)),
                  pl.BlockSpec((None, LANES, d), lambda bb, i: (bb, jnp.maximum(i - 1, 0), 0)),
                  vec, vec],
        out_specs=[row, row],
        out_shape=[jax.ShapeDtypeStruct((n, d), F32), jax.ShapeDtypeStruct((n, d), BF16)],
        compiler_params=_cp(("parallel", "arbitrary")),
        name="ln_in",
    )(head, x, g.reshape(1, d), b.reshape(1, d))


def _mm_kernel(x_ref, w_ref, b_ref, o_ref, *, sig_tile):
    acc = jnp.dot(x_ref[...], w_ref[...], preferred_element_type=F32) + b_ref[...]
    if sig_tile is None:
        o_ref[...] = acc.astype(o_ref.dtype)
        return
    j = pl.program_id(0)

    @pl.when(j < sig_tile)
    def _():
        o_ref[...] = acc.astype(o_ref.dtype)

    @pl.when(j >= sig_tile)
    def _():
        o_ref[...] = jax.nn.sigmoid(acc).astype(o_ref.dtype)


def _matmul_bias(x, w, b, out_dtype, sig_col=None, name="mm"):
    n, kd = x.shape
    nc = w.shape[1]
    tm = _pick(n, (1024, 512, 256, 128))
    tn = _pick(nc, (1024, 512, 256, 128))
    sig_tile = None
    if sig_col is not None:
        assert sig_col % tn == 0
        sig_tile = sig_col // tn
    return pl.pallas_call(
        functools.partial(_mm_kernel, sig_tile=sig_tile),
        grid=(nc // tn, n // tm),
        in_specs=[pl.BlockSpec((tm, kd), lambda j, i: (i, 0)),
                  pl.BlockSpec((kd, tn), lambda j, i: (0, j)),
                  pl.BlockSpec((1, tn), lambda j, i: (0, j))],
        out_specs=pl.BlockSpec((tm, tn), lambda j, i: (i, j)),
        out_shape=jax.ShapeDtypeStruct((n, nc), out_dtype),
        compiler_params=_cp(("parallel", "parallel")),
        name=name,
    )(x, w, b.reshape(1, nc))


def _rope_slab(r, cos_t, sin_t):
    return r * cos_t + pltpu.roll(r, 2 * (QK_ROPE // 2), 1) * sin_t


def _qproj_kernel(c_ref, g_ref, w_ref, cos_ref, sin_ref, o_ref, *, heads, scale):
    xn = _rms_rows(c_ref[...].astype(F32), g_ref[...])
    q = jnp.dot(xn.astype(BF16), w_ref[...], preferred_element_type=F32) * scale
    cos_t, sin_t = cos_ref[...], sin_ref[...]
    for h in range(heads):
        lo = h * HEAD_SLAB
        o_ref[:, lo:lo + LANES] = q[:, lo:lo + LANES].astype(BF16)
        o_ref[:, lo + LANES:lo + HEAD_SLAB] = _rope_slab(q[:, lo + LANES:lo + HEAD_SLAB], cos_t, sin_t).astype(BF16)


def _kproj_kernel(c_ref, g_ref, w_ref, kr_ref, cos_ref, sin_ref, k_ref, v_ref, *, heads):
    xn = _rms_rows(c_ref[...].astype(F32), g_ref[...])
    kv = jnp.dot(xn.astype(BF16), w_ref[...], preferred_element_type=F32)
    rr = _rope_slab(kr_ref[...], cos_ref[...], sin_ref[...]).astype(BF16)
    for h in range(heads):
        lo = h * HEAD_SLAB
        k_ref[:, lo:lo + LANES] = kv[:, h * LANES:(h + 1) * LANES].astype(BF16)
        k_ref[:, lo + LANES:lo + HEAD_SLAB] = rr
    v_ref[...] = kv[:, heads * LANES:].astype(BF16)


def _mla_qkv(z, zs, qg, kvg, wq, wkv, cos_t, sin_t, tp, ql):
    n = z.shape[0]
    heads = MLA_HEADS
    tm = _pick(tp, (384, 256, 128))
    nt = tp // tm
    scale = (QK_NOPE + QK_ROPE) ** -0.5
    tab = pl.BlockSpec((tm, LANES), lambda i: (i % nt, 0))
    q = pl.pallas_call(
        functools.partial(_qproj_kernel, heads=heads, scale=scale),
        grid=(n // tm,),
        in_specs=[pl.BlockSpec((tm, ql), lambda i: (i, 0)),
                  pl.BlockSpec((1, ql), lambda i: (0, 0)),
                  pl.BlockSpec((ql, heads * HEAD_SLAB), lambda i: (0, 0)),
                  tab, tab],
        out_specs=pl.BlockSpec((tm, heads * HEAD_SLAB), lambda i: (i, 0)),
        out_shape=jax.ShapeDtypeStruct((n, heads * HEAD_SLAB), BF16),
        compiler_params=_cp(("parallel",)),
        name="mla_qproj",
    )(z, qg.reshape(1, ql), wq, cos_t, sin_t)
    k, v = pl.pallas_call(
        functools.partial(_kproj_kernel, heads=heads),
        grid=(n // tm,),
        in_specs=[pl.BlockSpec((tm, ql), lambda i: (i, 1)),
                  pl.BlockSpec((1, ql), lambda i: (0, 0)),
                  pl.BlockSpec((ql, 2 * heads * LANES), lambda i: (0, 0)),
                  pl.BlockSpec((tm, LANES), lambda i: (i, 0)),
                  tab, tab],
        out_specs=[pl.BlockSpec((tm, heads * HEAD_SLAB), lambda i: (i, 0)),
                   pl.BlockSpec((tm, heads * LANES), lambda i: (i, 0))],
        out_shape=[jax.ShapeDtypeStruct((n, heads * HEAD_SLAB), BF16),
                   jax.ShapeDtypeStruct((n, heads * LANES), BF16)],
        compiler_params=_cp(("parallel",)),
        name="mla_kvproj",
    )(z, kvg.reshape(1, ql), wkv, zs, cos_t, sin_t)
    return q, k, v


ATTN_HEADS_PER_STEP = 2


def _attn_kernel(q_ref, k_ref, v_ref, o_ref, s_sc, m_sc, l_sc, acc_sc, *, t, hps):
    i = pl.program_id(2)
    nl = t // LANES

    def lane_fold(x, fn):
        r = x[:, :LANES]
        for c in range(1, nl):
            r = fn(r, x[:, c * LANES:(c + 1) * LANES])
        return r

    m_sc[...] = jnp.full(m_sc.shape, NEG, F32)

    def scores(j, masked):
        rows = pl.ds(pl.multiple_of(j * t, t), t)
        if masked:
            qpos = i * t + lax.broadcasted_iota(jnp.int32, (t, t), 0)
            kpos = j * t + lax.broadcasted_iota(jnp.int32, (t, t), 1)
            keep = (kpos <= qpos) & (kpos >= PADL)
        for h in range(hps):
            q = q_ref[:, h * HEAD_SLAB:(h + 1) * HEAD_SLAB]
            s = lax.dot_general(q, k_ref[rows, h * HEAD_SLAB:(h + 1) * HEAD_SLAB], NT, preferred_element_type=F32)
            if masked:
                s = jnp.where(keep, s, NEG)
            s_sc[h, j] = s
            m_sc[h] = jnp.maximum(m_sc[h], lane_fold(s, jnp.maximum))

    scores(0, True)

    def score_body(j, carry):
        scores(j, False)
        return carry

    lax.fori_loop(1, i, score_body, 0)

    @pl.when(i > 0)
    def _():
        scores(i, True)

    for h in range(hps):
        m_sc[h] = jnp.broadcast_to(jnp.max(m_sc[h], axis=1, keepdims=True), (t, LANES))
    l_sc[...] = jnp.zeros(l_sc.shape, F32)
    acc_sc[...] = jnp.zeros(acc_sc.shape, F32)

    def pv_body(j, carry):
        rows = pl.ds(pl.multiple_of(j * t, t), t)
        for h in range(hps):
            s = s_sc[h, j]
            mb = m_sc[h]
            p = jnp.concatenate([jnp.exp(s[:, c * LANES:(c + 1) * LANES] - mb) for c in range(nl)], axis=1)
            l_sc[h] = l_sc[h] + lane_fold(p, jnp.add)
            acc_sc[h] = acc_sc[h] + jnp.dot(p.astype(BF16), v_ref[rows, h * V_HEAD:(h + 1) * V_HEAD],
                                            preferred_element_type=F32)
        return carry

    lax.fori_loop(0, i + 1, pv_body, 0)

    for h in range(hps):
        l = jnp.sum(l_sc[h], axis=1, keepdims=True)
        o_ref[:, h * V_HEAD:(h + 1) * V_HEAD] = (acc_sc[h] / l).astype(o_ref.dtype)


def _attention(q, k, v, batch, tp):
    n = q.shape[0]
    heads = MLA_HEADS
    hps = ATTN_HEADS_PER_STEP
    assert heads % hps == 0
    t = _pick(tp, (384, 256, 128))
    nt = tp // t
    return pl.pallas_call(
        functools.partial(_attn_kernel, t=t, hps=hps),
        grid=(batch, heads // hps, nt),
        in_specs=[pl.BlockSpec((t, hps * HEAD_SLAB), lambda b, h, i: (b * nt + i, h)),
                  pl.BlockSpec((tp, hps * HEAD_SLAB), lambda b, h, i: (b, h)),
                  pl.BlockSpec((tp, hps * V_HEAD), lambda b, h, i: (b, h))],
        out_specs=pl.BlockSpec((t, hps * V_HEAD), lambda b, h, i: (b * nt + i, h)),
        out_shape=jax.ShapeDtypeStruct((n, heads * V_HEAD), BF16),
        scratch_shapes=[pltpu.VMEM((hps, nt, t, t), F32), pltpu.VMEM((hps, t, LANES), F32),
                        pltpu.VMEM((hps, t, LANES), F32), pltpu.VMEM((hps, t, V_HEAD), F32)],
        compiler_params=_cp(("parallel", "parallel", "parallel")),
        name="mla_attention",
    )(q, k, v)


HALO = 16


def _conv_kernel(x_ref, halo_ref, w_ref, b_ref, s_ref, o_ref, *, tm, tp):
    i = pl.program_id(0)
    start = lax.rem(i * tm, tp)
    pos = start + lax.broadcasted_iota(jnp.int32, (tm, 1), 0)
    x = jnp.where(pos >= PADL, x_ref[...].astype(F32), 0.0)
    hpos = start - HALO + lax.broadcasted_iota(jnp.int32, (HALO, 1), 0)
    halo = jnp.where(hpos >= PADL, halo_ref[...].astype(F32), 0.0)
    ext = jnp.concatenate([halo, x], axis=0)
    w = w_ref[...]
    y = b_ref[...] + w[CONV_K - 1:CONV_K, :] * x
    for j in range(1, CONV_K):
        y = y + w[CONV_K - 1 - j:CONV_K - j, :] * pltpu.roll(ext, j, 0)[HALO:, :]
    o_ref[...] = (_silu(y) * s_ref[...]).astype(BF16)


def _qk_conv(z, conv_w, conv_b, col_scale, tp, off_blk):
    n = z.shape[0]
    cw = conv_w.shape[1] // 2
    tm = _pick(tp, (384, 256, 128))
    return pl.pallas_call(
        functools.partial(_conv_kernel, tm=tm, tp=tp),
        grid=(n // tm, 2),
        in_specs=[pl.BlockSpec((tm, cw), lambda i, c: (i, off_blk + c)),
                  pl.BlockSpec((HALO, cw), lambda i, c: (jnp.maximum(i * (tm // HALO) - 1, 0), off_blk + c)),
                  pl.BlockSpec((CONV_K, cw), lambda i, c: (0, c)),
                  pl.BlockSpec((1, cw), lambda i, c: (0, c)),
                  pl.BlockSpec((1, cw), lambda i, c: (0, c))],
        out_specs=pl.BlockSpec((tm, cw), lambda i, c: (i, c)),
        out_shape=jax.ShapeDtypeStruct((n, 2 * cw), BF16),
        compiler_params=_cp(("parallel", "parallel")),
        name="mlstm_qk_conv",
    )(z, z, conv_w, conv_b.reshape(1, -1), col_scale)


MLSTM_HEADS_PER_STEP = 2


def _mlstm_kernel(*refs, group, heads, hps):
    q_refs, k_refs, v_refs = refs[:hps], refs[hps:2 * hps], refs[2 * hps:3 * hps]
    gt_ref = refs[3 * hps]
    og_refs = refs[3 * hps + 1:4 * hps + 1]
    ng_ref, o_ref, c_sc, n_sc, m_sc = refs[4 * hps + 1:]
    hp = pl.program_id(1)
    g = pl.program_id(2)
    L = CHUNK

    @pl.when(g == 0)
    def _():
        c_sc[...] = jnp.zeros(c_sc.shape, F32)
        n_sc[...] = jnp.zeros(n_sc.shape, F32)
        m_sc[...] = jnp.zeros(m_sc.shape, F32)

    R = group * L
    lane = lax.broadcasted_iota(jnp.int32, (R, LANES), 1)
    r_i = lax.broadcasted_iota(jnp.int32, (L, L), 0)
    c_i = lax.broadcasted_iota(jnp.int32, (L, L), 1)
    eye = r_i == c_i
    causal = c_i <= r_i

    gt = gt_ref[...]
    pad = (g * R + lax.broadcasted_iota(jnp.int32, (R, 1), 0)) < PADL
    i_parts, lf_parts = [], []
    for hh in range(hps):
        h = hp * hps + hh
        i_all = jnp.sum(jnp.where(lane == h, gt, 0.0), axis=1, keepdims=True)
        f_all = jnp.sum(jnp.where(lane == heads + h, gt, 0.0), axis=1, keepdims=True)
        i_parts.append(jnp.where(pad, NEG, i_all).reshape(group, L, 1))
        lf_parts.append(jnp.where(pad, 0.0, -(jnp.maximum(-f_all, 0.0) + jnp.log1p(jnp.exp(-jnp.abs(f_all)))))
                        .reshape(group, L, 1))
    i_col = jnp.concatenate(i_parts, axis=0)
    lf_col = jnp.concatenate(lf_parts, axis=0)
    b_row = jnp.sum(jnp.where((r_i <= c_i)[None], lf_col, 0.0), axis=1, keepdims=True)
    b_col = jnp.sum(jnp.where(eye[None], b_row, 0.0), axis=2, keepdims=True)
    i_row = jnp.sum(jnp.where(eye[None], i_col, 0.0), axis=1, keepdims=True)
    gtot = jnp.sum(lf_col, axis=1, keepdims=True)
    dmat = jnp.where(causal[None], b_col - b_row + i_row, NEG)
    rmax = jnp.max(dmat, axis=2, keepdims=True)
    a_col = gtot - b_col + i_col
    amax = jnp.max(a_col, axis=1, keepdims=True)

    m_prev_parts, m_new_parts = [], []
    for hh in range(hps):
        m = m_sc[hh].reshape(1, 1, 1)
        for c in range(group):
            idx = hh * group + c
            m_prev_parts.append(m)
            m = jnp.maximum(gtot[idx:idx + 1] + m, amax[idx:idx + 1])
            m_new_parts.append(m)
        m_sc[hh] = m.reshape(1, 1)
    m_prev = jnp.concatenate(m_prev_parts, axis=0)
    m_new = jnp.concatenate(m_new_parts, axis=0)
    inter = b_col + m_prev
    mrow = jnp.maximum(rmax, inter)
    dexp = jnp.exp(dmat - mrow)
    e_in = jnp.exp(inter - mrow)
    floor = jnp.exp(-mrow)
    ea = jnp.exp(a_col - m_new)
    decay = jnp.exp(gtot + m_prev - m_new)

    for hh in range(hps):
        cmat, nvec = c_sc[hh], n_sc[hh]
        for c in range(group):
            idx = hh * group + c
            rows = slice(c * L, (c + 1) * L)
            q = q_refs[hh][rows, :]
            k = k_refs[hh][rows, :]
            v = v_refs[hh][rows, :]
            s = lax.dot_general(q, k, NT, preferred_element_type=F32) * dexp[idx]
            num = (jnp.dot(s.astype(BF16), v, preferred_element_type=F32)
                   + e_in[idx] * jnp.dot(q, cmat.astype(BF16), preferred_element_type=F32))
            den = (jnp.sum(s, axis=1, keepdims=True)
                   + e_in[idx] * jnp.sum(q.astype(F32) * nvec, axis=1, keepdims=True))
            hc = num / jnp.maximum(jnp.abs(den), floor[idx])
            ks = k.astype(F32) * ea[idx]
            cmat = decay[idx] * cmat + lax.dot_general(ks.astype(BF16), v, TN, preferred_element_type=F32)
            nvec = decay[idx] * nvec + jnp.sum(ks, axis=0, keepdims=True)
            hn = _rms_rows(hc, ng_ref[:, hh * ML_V:(hh + 1) * ML_V])
            o_ref[rows, hh * ML_V:(hh + 1) * ML_V] = (hn * og_refs[hh][rows, :].astype(F32)).astype(BF16)
        c_sc[hh], n_sc[hh] = cmat, nvec


def _mlstm(qk, z, zs, norm_g, batch, tp, v_blk, og_blk):
    n = qk.shape[0]
    heads = ML_HEADS
    hps = MLSTM_HEADS_PER_STEP
    assert heads % hps == 0
    nchunks = tp // CHUNK
    group = _pick(nchunks, (6, 5, 4, 3, 2, 1))
    rows = group * CHUNK
    ng = nchunks // group

    def col(width, base):
        return [pl.BlockSpec((rows, width), lambda b, p, g, u=u: (b * ng + g, base + p * hps + u)) for u in range(hps)]

    return pl.pallas_call(
        functools.partial(_mlstm_kernel, group=group, heads=heads, hps=hps),
        grid=(batch, heads // hps, ng),
        in_specs=(col(ML_QK, 0) + col(ML_QK, heads) + col(ML_V, v_blk)
                  + [pl.BlockSpec((rows, LANES), lambda b, p, g: (b * ng + g, 1))]
                  + col(ML_V, og_blk)
                  + [pl.BlockSpec((1, hps * ML_V), lambda b, p, g: (0, p))]),
        out_specs=pl.BlockSpec((rows, hps * ML_V), lambda b, p, g: (b * ng + g, p)),
        out_shape=jax.ShapeDtypeStruct((n, heads * ML_V), BF16),
        scratch_shapes=[pltpu.VMEM((hps, ML_QK, ML_V), F32), pltpu.VMEM((hps, 1, ML_QK), F32),
                        pltpu.VMEM((hps, 1, 1), F32)],
        compiler_params=_cp(("parallel", "parallel", "arbitrary")),
        name="mlstm_scan",
    )(*([qk] * (2 * hps) + [z] * hps + [zs] + [z] * hps + [norm_g.reshape(1, -1)]))


def _merge_kernel(a_ref, hm_ref, wa_ref, wb_ref, ga_ref, gb_ref, o_ref):
    ya = jnp.dot(a_ref[...], wa_ref[...], preferred_element_type=F32)
    yb = jnp.dot(hm_ref[...], wb_ref[...], preferred_element_type=F32)
    o_ref[...] = (ga_ref[...].astype(F32) * ya + gb_ref[...].astype(F32) * yb).astype(BF16)


def _merge(attn, hm, wa, wb, z, ga_off, gb_off):
    n = attn.shape[0]
    d = wa.shape[1]
    tm = _pick(n, (512, 256, 128))
    tn = _pick(d, (1024, 512, 256, 128))
    assert ga_off % tn == 0 and gb_off % tn == 0
    ga_blk, gb_blk = ga_off // tn, gb_off // tn
    return pl.pallas_call(
        _merge_kernel,
        grid=(d // tn, n // tm),
        in_specs=[pl.BlockSpec((tm, attn.shape[1]), lambda j, i: (i, 0)),
                  pl.BlockSpec((tm, hm.shape[1]), lambda j, i: (i, 0)),
                  pl.BlockSpec((wa.shape[0], tn), lambda j, i: (0, j)),
                  pl.BlockSpec((wb.shape[0], tn), lambda j, i: (0, j)),
                  pl.BlockSpec((tm, tn), lambda j, i: (i, ga_blk + j)),
                  pl.BlockSpec((tm, tn), lambda j, i: (i, gb_blk + j))],
        out_specs=pl.BlockSpec((tm, tn), lambda j, i: (i, j)),
        out_shape=jax.ShapeDtypeStruct((n, d), BF16),
        compiler_params=_cp(("parallel", "parallel")),
        name="branch_merge",
    )(attn, hm, wa, wb, z, z)


def _outln_kernel(y_ref, w_ref, h_ref, g_ref, b_ref, of_ref, os_ref):
    o = jnp.dot(y_ref[...], w_ref[...], preferred_element_type=F32)
    r = _ln_rows(ALPHA * h_ref[...] + o, g_ref[...], b_ref[...])
    of_ref[...] = r
    _store_slab(os_ref, r)


def _outproj_ln(y, w, h, g, b):
    n, d = h.shape
    tm = _pick(n, (256, 128))
    ns = d // HEAD_SLAB
    row = pl.BlockSpec((tm, d), lambda i: (i, 0))
    vec = pl.BlockSpec((1, d), lambda i: (0, 0))
    return pl.pallas_call(
        _outln_kernel,
        grid=(n // tm,),
        in_specs=[row, pl.BlockSpec((d, d), lambda i: (0, 0)), row, vec, vec],
        out_specs=[row, pl.BlockSpec((tm * ns, LANES), lambda i: (i, 0))],
        out_shape=[jax.ShapeDtypeStruct((n, d), F32), jax.ShapeDtypeStruct((n * ns, LANES), U32)],
        compiler_params=_cp(("parallel",)),
        name="outproj_ln",
    )(y, w, h, g.reshape(1, d), b.reshape(1, d))


def _router_kernel(x_ref, wh_ref, wl_ref, eb_ref, idx_ref, wt_ref, rank_ref, cnt_ref, base_sc, *, tm):
    i = pl.program_id(0)
    E, G = N_EXPERTS, N_GROUPS
    per = E // G

    @pl.when(i == 0)
    def _():
        base_sc[...] = jnp.zeros(base_sc.shape, F32)

    x = x_ref[...]
    xh = x.astype(BF16)
    xl = (x - xh.astype(F32)).astype(BF16)
    wh, wl = wh_ref[...], wl_ref[...]
    logits = (lax.dot_general(wh, xh, NT, preferred_element_type=F32)
              + lax.dot_general(wh, xl, NT, preferred_element_type=F32)
              + lax.dot_general(wl, xh, NT, preferred_element_type=F32))
    scores = jax.nn.sigmoid(logits)
    biased = scores + eb_ref[...]
    s3 = scores.reshape(G, per, tm)
    b3 = biased.reshape(G, per, tm)
    j_io = lax.broadcasted_iota(jnp.int32, (G, per, tm), 1).astype(F32)
    g_io3 = lax.broadcasted_iota(jnp.int32, (G, per, tm), 0).astype(F32)
    e_io = g_io3 * per + j_io
    g_io = lax.broadcasted_iota(jnp.int32, (G, 1, tm), 0).astype(F32)
    ninf = -jnp.inf

    m1 = jnp.max(b3, axis=1, keepdims=True)
    i1 = jnp.min(jnp.where(b3 == m1, j_io, float(per)), axis=1, keepdims=True)
    m2 = jnp.max(jnp.where(j_io == i1, ninf, b3), axis=1, keepdims=True)
    gs = m1 + m2
    gsel = jnp.zeros((G, 1, tm), F32)
    for _ in range(TOPK_GROUPS):
        gm = jnp.max(gs, axis=0, keepdims=True)
        gi = jnp.min(jnp.where(gs == gm, g_io, float(G)), axis=0, keepdims=True)
        hit = g_io == gi
        gsel = jnp.where(hit, 1.0, gsel)
        gs = jnp.where(hit, ninf, gs)
    masked = jnp.where(gsel > 0.0, b3, ninf)

    def red2(fn, a):
        return fn(fn(a, axis=1, keepdims=True), axis=0, keepdims=True)

    sel = jnp.zeros((G, per, tm), F32)
    idxs, scs = [], []
    for _ in range(TOP_K):
        mx = red2(jnp.max, masked)
        ei = red2(jnp.min, jnp.where(masked == mx, e_io, float(E)))
        hit = e_io == ei
        scs.append(red2(jnp.sum, jnp.where(hit, s3, 0.0)))
        idxs.append(ei)
        sel = jnp.where(hit, 1.0, sel)
        masked = jnp.where(hit, ninf, masked)
    wsum = scs[0]
    for k in range(1, TOP_K):
        wsum = wsum + scs[k]

    sel2 = sel.reshape(E, tm)
    upper = (lax.broadcasted_iota(jnp.int32, (tm, tm), 0) < lax.broadcasted_iota(jnp.int32, (tm, tm), 1))
    excl = jnp.dot(sel2.astype(BF16), upper.astype(BF16), preferred_element_type=F32)
    cnt3 = (excl + base_sc[...]).reshape(G, per, tm)
    for k in range(TOP_K):
        hit = e_io == idxs[k]
        rank = red2(jnp.sum, jnp.where(hit, cnt3, 0.0))
        idx_ref[k:k + 1, :] = idxs[k].reshape(1, tm).astype(jnp.int32)
        wt_ref[k:k + 1, :] = (scs[k] / wsum * ROUTED_SCALE).reshape(1, tm)
        rank_ref[k:k + 1, :] = rank.reshape(1, tm).astype(jnp.int32)
    base_sc[...] = base_sc[...] + jnp.sum(sel2, axis=1, keepdims=True)
    cnt_ref[...] = jnp.broadcast_to(base_sc[...], cnt_ref.shape).astype(jnp.int32)


def _router(h, wr_hi, wr_lo, e_bias):
    n, d = h.shape
    tm = _pick(n, (512, 256, 128))
    outk = pl.BlockSpec((TOP_K, tm), lambda i: (0, i))
    return pl.pallas_call(
        functools.partial(_router_kernel, tm=tm),
        grid=(n // tm,),
        in_specs=[pl.BlockSpec((tm, d), lambda i: (i, 0)),
                  pl.BlockSpec((N_EXPERTS, d), lambda i: (0, 0)),
                  pl.BlockSpec((N_EXPERTS, d), lambda i: (0, 0)),
                  pl.BlockSpec((N_EXPERTS, 1), lambda i: (0, 0))],
        out_specs=[outk, outk, outk, pl.BlockSpec((N_EXPERTS, LANES), lambda i: (0, 0))],
        out_shape=[jax.ShapeDtypeStruct((TOP_K, n), jnp.int32),
                   jax.ShapeDtypeStruct((TOP_K, n), F32),
                   jax.ShapeDtypeStruct((TOP_K, n), jnp.int32),
                   jax.ShapeDtypeStruct((N_EXPERTS, LANES), jnp.int32)],
        scratch_shapes=[pltpu.VMEM((N_EXPERTS, 1), F32)],
        compiler_params=_cp(("arbitrary",)),
        name="moe_router",
    )(h, wr_hi, wr_lo, e_bias.reshape(N_EXPERTS, 1))


def _row_copy(src, dst, sem):
    return pltpu.make_async_copy(src, dst, sem)


def _slab_rows(i, ns):
    return pl.ds(pl.multiple_of(i * ns, ns), ns)


def _dispatch_kernel(ps_ref, pc_ref, dest_ref, x_ref, xs_ref, zbuf, sem, *, tm, ns):
    i = pl.program_id(0)

    @pl.when(i == 0)
    def _():
        zbuf[...] = jnp.zeros(zbuf.shape, U32)

        def per_expert(e, carry):
            base, cnt = ps_ref[e], pc_ref[e]

            def fill(wait):
                def body(r, c2):
                    cp = _row_copy(zbuf, xs_ref.at[_slab_rows(base + r, ns), :], sem.at[1])
                    cp.wait() if wait else cp.start()
                    return c2
                lax.fori_loop(0, cnt, body, 0)

            fill(False)
            fill(True)
            return carry

        lax.fori_loop(0, N_EXPERTS, per_expert, 0)

    def copy(k, r):
        d = dest_ref[k, r]
        return _row_copy(x_ref.at[pl.ds(r * ns, ns), :], xs_ref.at[_slab_rows(d, ns), :], sem.at[0])

    for r in range(tm):
        for k in range(TOP_K):
            copy(k, r).start(priority=k % 2)

    def drain(r, carry):
        for k in range(TOP_K):
            copy(k, r).wait()
        return carry

    lax.fori_loop(0, tm, drain, 0)


def _dispatch(hs, dest, pad_start, pad_count, n_slots):
    ns = hs.shape[0] // dest.shape[1]
    n = dest.shape[1]
    tm = LANES
    grid_spec = pltpu.PrefetchScalarGridSpec(
        num_scalar_prefetch=2,
        grid=(n // tm,),
        in_specs=[pl.BlockSpec((TOP_K, tm), lambda i, ps, pc: (0, i), memory_space=pltpu.SMEM),
                  pl.BlockSpec((tm * ns, LANES), lambda i, ps, pc: (i, 0))],
        out_specs=pl.BlockSpec(memory_space=pl.ANY),
        scratch_shapes=[pltpu.VMEM((ns, LANES), U32), pltpu.SemaphoreType.DMA((2,))],
    )
    return pl.pallas_call(
        functools.partial(_dispatch_kernel, tm=tm, ns=ns),
        grid_spec=grid_spec,
        out_shape=jax.ShapeDtypeStruct((n_slots * ns, LANES), U32),
        compiler_params=_cp(("arbitrary",)),
        name="moe_dispatch",
    )(pad_start, pad_count, dest, hs)


def _expert_kernel(be_ref, nu_ref, x_ref, w1_ref, w3_ref, w2_ref, o_ref, w1b, w3b, w2b, *, tm, ns):
    j = pl.program_id(0)
    n_used = nu_ref[0]

    @pl.when(j < n_used)
    def _():
        @pl.when((j == 0) | (be_ref[j] != be_ref[jnp.maximum(j - 1, 0)]))
        def _():
            w1b[...] = w1_ref[...].astype(BF16)
            w3b[...] = w3_ref[...].astype(BF16)
            w2b[...] = w2_ref[...].astype(BF16)

        parts = []
        for s in range(ns):
            lo, hi = _load_slab_pairs(x_ref, (), s, tm, ns)
            parts += [lo.astype(BF16), hi.astype(BF16)]
        x = jnp.concatenate(parts, axis=1)
        a = jnp.dot(x, w1b[...], preferred_element_type=F32)
        b = jnp.dot(x, w3b[...], preferred_element_type=F32)
        hb = (_silu(a) * b).astype(BF16)
        _store_slab(o_ref, jnp.dot(hb, w2b[...], preferred_element_type=F32))

    @pl.when(j >= n_used)
    def _():
        o_ref[...] = jnp.zeros(o_ref.shape, U32)


def _experts(xs, blk_e, n_used, w1, w3, w2, layer):
    d, f = w1.shape[2], w1.shape[3]
    ns = d // HEAD_SLAB
    nb = blk_e.shape[0]
    tm = EXPERT_ROWS
    grid_spec = pltpu.PrefetchScalarGridSpec(
        num_scalar_prefetch=2,
        grid=(nb,),
        in_specs=[pl.BlockSpec((tm * ns, LANES), lambda j, be, nu: (jnp.minimum(j, nu[0] - 1), 0)),
                  pl.BlockSpec((None, None, d, f), lambda j, be, nu: (layer, be[j], 0, 0)),
                  pl.BlockSpec((None, None, d, f), lambda j, be, nu: (layer, be[j], 0, 0)),
                  pl.BlockSpec((None, None, f, d), lambda j, be, nu: (layer, be[j], 0, 0))],
        out_specs=pl.BlockSpec((tm * ns, LANES), lambda j, be, nu: (j, 0)),
        scratch_shapes=[pltpu.VMEM((d, f), BF16), pltpu.VMEM((d, f), BF16), pltpu.VMEM((f, d), BF16)],
    )
    return pl.pallas_call(
        functools.partial(_expert_kernel, tm=tm, ns=ns),
        grid_spec=grid_spec,
        out_shape=jax.ShapeDtypeStruct((nb * tm * ns, LANES), U32),
        compiler_params=_cp(("arbitrary",)),
        name="moe_experts",
    )(blk_e, n_used, xs, w1, w3, w2)


def _combine_kernel(dest_ref, dnxt_ref, h_ref, wt_ref, ws1_ref, ws3_ref, ws2_ref, g_ref, b_ref, ys_ref,
                    of_ref, *rest, tm, ns, nsteps):
    ob_ref = rest[0] if len(rest) == 3 else None
    gbuf, sem = rest[-2:]
    i = pl.program_id(0)
    slot = lax.rem(i, 2)

    def copy(d_ref, k, r, dst_slot):
        d = d_ref[k, r]
        return _row_copy(ys_ref.at[_slab_rows(d, ns), :], gbuf.at[dst_slot, k, _slab_rows(r, ns), :],
                         sem.at[dst_slot])

    def loop(d_ref, dst_slot, wait):
        def body(r, carry):
            for k in range(TOP_K):
                cp = copy(d_ref, k, r, dst_slot)
                if wait:
                    cp.wait()
                else:
                    cp.start(priority=k % 2)
            return carry

        lax.fori_loop(0, tm, body, 0)

    @pl.when(i == 0)
    def _():
        loop(dest_ref, slot, False)

    loop(dest_ref, slot, True)
    for r in range(tm):
        for k in range(TOP_K):
            copy(dnxt_ref, k, r, 1 - slot).start(priority=k % 2)
    h = h_ref[...]
    xb = h.astype(BF16)
    a = jnp.dot(xb, ws1_ref[...], preferred_element_type=F32)
    b = jnp.dot(xb, ws3_ref[...], preferred_element_type=F32)
    shared = jnp.dot((_silu(a) * b).astype(BF16), ws2_ref[...], preferred_element_type=F32)
    wt = wt_ref[...]
    parts = []
    for s in range(ns):
        lo = hi = None
        for k in range(TOP_K):
            plo, phi = _load_slab_pairs(gbuf, (slot, k), s, tm, ns)
            w = wt[:, k:k + 1]
            lo = w * plo if lo is None else lo + w * plo
            hi = w * phi if hi is None else hi + w * phi
        parts += [lo, hi]
    acc = shared + jnp.concatenate(parts, axis=1)
    r = _ln_rows(ALPHA * h + acc, g_ref[...], b_ref[...])
    of_ref[...] = r
    if ob_ref is not None:
        ob_ref[...] = r.astype(BF16)

    @pl.when(i == nsteps - 1)
    def _():
        loop(dnxt_ref, 1 - slot, True)


def _combine(h, dest, wt_tok, ws1, ws3, ws2, g, b, ys, final_shape=None):
    n, d = h.shape
    f = ws1.shape[1]
    tm = LANES
    row = pl.BlockSpec((tm, d), lambda i: (i, 0))
    vec = pl.BlockSpec((1, d), lambda i: (0, 0))
    if final_shape is None:
        out_specs = [row, row]
        out_shape = [jax.ShapeDtypeStruct((n, d), F32), jax.ShapeDtypeStruct((n, d), BF16)]
    else:
        batch, seq = final_shape
        nt = seq // tm + 1
        out_specs = [pl.BlockSpec((None, tm, d), lambda i: (i // nt, jnp.maximum(i % nt - 1, 0), 0))]
        out_shape = [jax.ShapeDtypeStruct((batch, seq, d), F32)]
    ns = d // HEAD_SLAB
    nsteps = n // tm
    return pl.pallas_call(
        functools.partial(_combine_kernel, tm=tm, ns=ns, nsteps=nsteps),
        grid=(nsteps,),
        in_specs=[pl.BlockSpec((TOP_K, tm), lambda i: (0, i), memory_space=pltpu.SMEM),
                  pl.BlockSpec((TOP_K, tm), lambda i: (0, jnp.minimum(i + 1, nsteps - 1)), memory_space=pltpu.SMEM),
                  row,
                  pl.BlockSpec((tm, TOP_K), lambda i: (i, 0)),
                  pl.BlockSpec((d, f), lambda i: (0, 0)),
                  pl.BlockSpec((d, f), lambda i: (0, 0)),
                  pl.BlockSpec((f, d), lambda i: (0, 0)),
                  vec, vec,
                  pl.BlockSpec(memory_space=pl.ANY)],
        out_specs=out_specs,
        out_shape=out_shape,
        scratch_shapes=[pltpu.VMEM((2, TOP_K, tm * ns, LANES), U32), pltpu.SemaphoreType.DMA((2,))],
        compiler_params=_cp(("arbitrary",)),
        name="moe_combine",
    )(dest, dest, h, wt_tok, ws1, ws3, ws2, g.reshape(1, d), b.reshape(1, d), ys)


def _moe(h_f32, hs, w_router, e_bias, w1, w3, w2, layer, ws1, ws3, ws2, g, b, final_shape=None):
    n = h_f32.shape[0]
    wr_t = w_router.T
    wr_hi = wr_t.astype(BF16)
    wr_lo = (wr_t - wr_hi.astype(F32)).astype(BF16)
    idx, wts, rank, cnt = _router(h_f32, wr_hi, wr_lo, e_bias)
    counts = cnt[:, 0]
    blk = EXPERT_ROWS
    pcounts = (counts + blk - 1) // blk * blk
    pends = jnp.cumsum(pcounts)
    pstarts = pends - pcounts
    e_ids = jnp.arange(N_EXPERTS, dtype=jnp.int32)
    start_of = jnp.sum(jnp.where(idx[None] == e_ids[:, None, None], pstarts[:, None, None], 0), axis=0)
    dest = (start_of + rank).astype(jnp.int32)
    nb = n * TOP_K // blk + N_EXPERTS
    blk_e = jnp.minimum(jnp.sum(pends[None, :] <= (jnp.arange(nb, dtype=jnp.int32) * blk)[:, None], axis=1),
                        N_EXPERTS - 1).astype(jnp.int32)
    n_used = (pends[-1:] // blk).astype(jnp.int32)
    xs = _dispatch(hs, dest, (pstarts + counts).astype(jnp.int32), (pcounts - counts).astype(jnp.int32), nb * blk)
    ys = _experts(xs, blk_e, n_used, w1, w3, w2, layer)
    return _combine(h_f32, dest, wts.T, ws1.astype(BF16), ws3.astype(BF16), ws2.astype(BF16), g, b, ys,
                    final_shape)


def _in_proj_layout(w_in, b_in, d, ql, kvl):
    hq, hv = ML_HEADS * ML_QK, ML_HEADS * ML_V
    sizes = (ql, kvl, QK_ROPE, hq, hq, hv, hv, ML_HEADS, ML_HEADS, d, d)
    offs = np.concatenate([[0], np.cumsum(sizes)])
    seg = lambda a, i: a[..., int(offs[i]):int(offs[i + 1])]
    order = (0, 1, 3, 4, 5, 6, 9, 10)
    w_main = jnp.concatenate([seg(w_in, i) for i in order], axis=-1)
    b_main = jnp.concatenate([seg(b_in, i) for i in order], axis=-1)
    half = QK_ROPE // 2

    def small(a):
        kr = seg(a, 2)
        x1, x2 = kr[..., :half], kr[..., half:]
        pad = jnp.zeros(a.shape[:-1] + (LANES - 2 * ML_HEADS,), a.dtype)
        return jnp.concatenate([x1, x2, x2, x1, seg(a, 7), seg(a, 8), pad], axis=-1)

    main_offs = np.concatenate([[0], np.cumsum([sizes[i] for i in order])])
    return w_main, b_main, small(w_in), small(b_in), [int(o) for o in main_offs]


def _uq_layout(w_uq):
    kq = w_uq.shape[0]
    w = w_uq.reshape(kq, MLA_HEADS, QK_NOPE + QK_ROPE)
    half = QK_ROPE // 2
    nope, x1, x2 = w[..., :QK_NOPE], w[..., QK_NOPE:QK_NOPE + half], w[..., QK_NOPE + half:]
    return jnp.concatenate([nope, x1, x2, x2, x1], axis=-1).reshape(kq, MLA_HEADS * HEAD_SLAB)


def _ukv_layout(w_ukv):
    kk = w_ukv.shape[0]
    w = w_ukv.reshape(kk, MLA_HEADS, QK_NOPE + V_HEAD)
    return jnp.concatenate([w[..., :QK_NOPE].reshape(kk, -1), w[..., QK_NOPE:].reshape(kk, -1)], axis=-1)


def _rope_tables(tp):
    half = QK_ROPE // 2
    pos = jnp.arange(tp, dtype=F32) - PADL
    inv_freq = 1.0 / (ROPE_THETA ** (jnp.arange(0, QK_ROPE, 2, dtype=F32) / QK_ROPE))
    ang = pos[:, None] * inv_freq[None, :]
    cos, sin = jnp.cos(ang), jnp.sin(ang)
    zero = jnp.zeros((tp, LANES - 2 * half), F32)
    return jnp.concatenate([cos, cos, zero], axis=1), jnp.concatenate([-sin, sin, zero], axis=1)


def kernel(x, meta, ln_in_g, ln_in_b, w_in, b_in, q_norm_g, kv_norm_g, w_uq, w_ukv, conv_w, conv_b, ml_norm_g, w_br_mla, w_br_mlstm, w_out, ln1_g, ln1_b, w_router, e_bias, w1, w3, w2, ws1, ws3, ws2, ln2_g, ln2_b):
    batch, seq, d = x.shape
    depth = w_in.shape[0]
    ql, kvl = q_norm_g.shape[1], kv_norm_g.shape[1]
    assert ql == kvl and seq % LANES == 0
    tp = LANES + seq
    n = batch * tp
    hq = ML_HEADS * ML_QK

    head = jnp.concatenate([jnp.zeros((PADL, d), x.dtype), meta.astype(x.dtype)], axis=0)
    h_f32, h_bf = _layer_norm_in(x, head, ln_in_g, ln_in_b)
    cos_t, sin_t = _rope_tables(tp)
    col_scale = jnp.concatenate([jnp.ones((1, hq), F32), jnp.full((1, hq), ML_QK ** -0.5, F32)], axis=1)

    for l in range(depth):
        w_main, b_main, w_small, b_small, offs = _in_proj_layout(w_in[l], b_in[l], d, ql, kvl)
        o_mq, o_mv, o_mo, o_ga, o_gb = offs[2], offs[4], offs[5], offs[6], offs[7]
        z = _matmul_bias(h_bf, w_main.astype(BF16), b_main, BF16, sig_col=o_mo, name="in_proj")
        zs = _matmul_bias(h_bf, w_small.astype(BF16), b_small, F32, name="in_proj_small")

        q, k, v = _mla_qkv(z, zs, q_norm_g[l], kv_norm_g[l], _uq_layout(w_uq[l]).astype(BF16),
                           _ukv_layout(w_ukv[l]).astype(BF16), cos_t, sin_t, tp, ql)
        attn = _attention(q, k, v, batch, tp)

        assert o_mq % hq == 0 and o_mv % ML_V == 0 and o_mo % ML_V == 0
        qk = _qk_conv(z, conv_w[l], conv_b[l], col_scale, tp, o_mq // hq)
        hm = _mlstm(qk, z, zs, ml_norm_g[l], batch, tp, o_mv // ML_V, o_mo // ML_V)

        y = _merge(attn, hm, w_br_mla[l].astype(BF16), w_br_mlstm[l].astype(BF16), z, o_ga, o_gb)
        h_f32, hs = _outproj_ln(y, w_out[l].astype(BF16), h_f32, ln1_g[l], ln1_b[l])
        outs = _moe(h_f32, hs, w_router[l], e_bias[l], w1, w3, w2, l, ws1[l], ws3[l], ws2[l],
                    ln2_g[l], ln2_b[l], final_shape=(batch, seq) if l == depth - 1 else None)
        if l == depth - 1:
            return outs[0]
        h_f32, h_bf = outs
```

```python
import functools

import numpy as np
import jax
import jax.numpy as jnp
from jax import lax
from jax.experimental import pallas as pl
from jax.experimental.pallas import tpu as pltpu

N_META = 16
MLA_HEADS = 8
QK_NOPE = 128
QK_ROPE = 64
V_HEAD = 128
ROPE_THETA = 10000.0
ML_HEADS = 8
ML_QK = 128
ML_V = 256
CONV_K = 4
CHUNK = 64
N_EXPERTS = 64
TOP_K = 8
N_GROUPS = 8
TOPK_GROUPS = 4
ROUTED_SCALE = 2.5
DEPTH = 2
ALPHA = (2 * DEPTH) ** 0.25
LN_EPS = 1e-5
RMS_EPS = 1e-6
NEG = -1e30

LANES = 128
HEAD_SLAB = 2 * LANES
PADL = LANES - N_META
EXPERT_ROWS = 512
VMEM_LIMIT = 56 * 1024 * 1024

F32 = jnp.float32
BF16 = jnp.bfloat16
U32 = jnp.uint32
NT = (((1,), (1,)), ((), ()))
TN = (((0,), (0,)), ((), ()))


def _pick(n, cands):
    for c in cands:
        if n % c == 0:
            return c
    raise ValueError(f"no tile in {cands} divides {n}")


def _cp(sem, vmem=None):
    return pltpu.CompilerParams(dimension_semantics=sem, vmem_limit_bytes=vmem or VMEM_LIMIT)


def _ln_rows(x, g, b):
    mu = jnp.mean(x, axis=-1, keepdims=True)
    xc = x - mu
    var = jnp.mean(xc * xc, axis=-1, keepdims=True)
    return xc * lax.rsqrt(var + LN_EPS) * g + b


def _rms_rows(x, g):
    return x * lax.rsqrt(jnp.mean(x * x, axis=-1, keepdims=True) + RMS_EPS) * g


def _silu(x):
    return x * jax.nn.sigmoid(x)


def _store_slab(ref, x):
    rows, ns = x.shape[0], x.shape[1] // HEAD_SLAB
    for s in range(ns):
        lo = lax.bitcast_convert_type(x[:, s * HEAD_SLAB:s * HEAD_SLAB + LANES].astype(BF16).astype(F32), U32)
        hi = lax.bitcast_convert_type(x[:, s * HEAD_SLAB + LANES:(s + 1) * HEAD_SLAB].astype(BF16).astype(F32), U32)
        ref[pl.ds(s, rows, stride=ns), :] = (lo >> 16) | hi


def _load_slab_pairs(ref, lead, s, rows, ns):
    u = ref[lead + (pl.ds(s, rows, stride=ns), slice(None))]
    return (lax.bitcast_convert_type(u << 16, F32),
            lax.bitcast_convert_type(u & jnp.uint32(0xFFFF0000), F32))


def _ln_in_kernel(head_ref, x_ref, g_ref, b_ref, of_ref, ob_ref):
    i = pl.program_id(1)

    def emit(src):
        y = _ln_rows(src, g_ref[...], b_ref[...])
        of_ref[...] = y
        ob_ref[...] = y.astype(BF16)

    @pl.when(i == 0)
    def _():
        emit(head_ref[...])

    @pl.when(i > 0)
    def _():
        emit(x_ref[...])


def _layer_norm_in(x, head, g, b):
    batch, seq, d = x.shape
    nt = seq // LANES + 1
    row = pl.BlockSpec((LANES, d), lambda bb, i: (bb * nt + i, 0))
    vec = pl.BlockSpec((1, d), lambda bb, i: (0, 0))
    n = batch * nt * LANES
    return pl.pallas_call(
        _ln_in_kernel,
        grid=(batch, nt),
        in_specs=[pl.BlockSpec((LANES, d), lambda bb, i: (0, 0)),
                  pl.BlockSpec((None, LANES, d), lambda bb, i: (bb, jnp.maximum(i - 1, 0), 0)),
                  vec, vec],
        out_specs=[row, row],
        out_shape=[jax.ShapeDtypeStruct((n, d), F32), jax.ShapeDtypeStruct((n, d), BF16)],
        compiler_params=_cp(("parallel", "arbitrary")),
        name="ln_in",
    )(head, x, g.reshape(1, d), b.reshape(1, d))


def _mm_kernel(x_ref, w_ref, b_ref, o_ref, *, sig_tile):
    acc = jnp.dot(x_ref[...], w_ref[...], preferred_element_type=F32) + b_ref[...]
    if sig_tile is None:
        o_ref[...] = acc.astype(o_ref.dtype)
        return
    j = pl.program_id(0)

    @pl.when(j < sig_tile)
    def _():
        o_ref[...] = acc.astype(o_ref.dtype)

    @pl.when(j >= sig_tile)
    def _():
        o_ref[...] = jax.nn.sigmoid(acc).astype(o_ref.dtype)


def _matmul_bias(x, w, b, out_dtype, sig_col=None, name="mm"):
    n, kd = x.shape
    nc = w.shape[1]
    tm = _pick(n, (1024, 512, 256, 128))
    tn = _pick(nc, (1024, 512, 256, 128))
    sig_tile = None
    if sig_col is not None:
        assert sig_col % tn == 0
        sig_tile = sig_col // tn
    return pl.pallas_call(
        functools.partial(_mm_kernel, sig_tile=sig_tile),
        grid=(nc // tn, n // tm),
        in_specs=[pl.BlockSpec((tm, kd), lambda j, i: (i, 0)),
                  pl.BlockSpec((kd, tn), lambda j, i: (0, j)),
                  pl.BlockSpec((1, tn), lambda j, i: (0, j))],
        out_specs=pl.BlockSpec((tm, tn), lambda j, i: (i, j)),
        out_shape=jax.ShapeDtypeStruct((n, nc), out_dtype),
        compiler_params=_cp(("parallel", "parallel")),
        name=name,
    )(x, w, b.reshape(1, nc))


def _rope_slab(r, cos_t, sin_t):
    return r * cos_t + pltpu.roll(r, 2 * (QK_ROPE // 2), 1) * sin_t


def _qproj_kernel(c_ref, g_ref, w_ref, cos_ref, sin_ref, o_ref, *, heads, scale):
    xn = _rms_rows(c_ref[...].astype(F32), g_ref[...])
    q = jnp.dot(xn.astype(BF16), w_ref[...], preferred_element_type=F32) * scale
    cos_t, sin_t = cos_ref[...], sin_ref[...]
    for h in range(heads):
        lo = h * HEAD_SLAB
        o_ref[:, lo:lo + LANES] = q[:, lo:lo + LANES].astype(BF16)
        o_ref[:, lo + LANES:lo + HEAD_SLAB] = _rope_slab(q[:, lo + LANES:lo + HEAD_SLAB], cos_t, sin_t).astype(BF16)


def _kproj_kernel(c_ref, g_ref, w_ref, kr_ref, cos_ref, sin_ref, k_ref, v_ref, *, heads):
    xn = _rms_rows(c_ref[...].astype(F32), g_ref[...])
    kv = jnp.dot(xn.astype(BF16), w_ref[...], preferred_element_type=F32)
    rr = _rope_slab(kr_ref[...], cos_ref[...], sin_ref[...]).astype(BF16)
    for h in range(heads):
        lo = h * HEAD_SLAB
        k_ref[:, lo:lo + LANES] = kv[:, h * LANES:(h + 1) * LANES].astype(BF16)
        k_ref[:, lo + LANES:lo + HEAD_SLAB] = rr
    v_ref[...] = kv[:, heads * LANES:].astype(BF16)


def _mla_qkv(z, zs, qg, kvg, wq, wkv, cos_t, sin_t, tp, ql):
    n = z.shape[0]
    heads = MLA_HEADS
    tm = _pick(tp, (384, 256, 128))
    nt = tp // tm
    scale = (QK_NOPE + QK_ROPE) ** -0.5
    tab = pl.BlockSpec((tm, LANES), lambda i: (i % nt, 0))
    q = pl.pallas_call(
        functools.partial(_qproj_kernel, heads=heads, scale=scale),
        grid=(n // tm,),
        in_specs=[pl.BlockSpec((tm, ql), lambda i: (i, 0)),
                  pl.BlockSpec((1, ql), lambda i: (0, 0)),
                  pl.BlockSpec((ql, heads * HEAD_SLAB), lambda i: (0, 0)),
                  tab, tab],
        out_specs=pl.BlockSpec((tm, heads * HEAD_SLAB), lambda i: (i, 0)),
        out_shape=jax.ShapeDtypeStruct((n, heads * HEAD_SLAB), BF16),
        compiler_params=_cp(("parallel",)),
        name="mla_qproj",
    )(z, qg.reshape(1, ql), wq, cos_t, sin_t)
    k, v = pl.pallas_call(
        functools.partial(_kproj_kernel, heads=heads),
        grid=(n // tm,),
        in_specs=[pl.BlockSpec((tm, ql), lambda i: (i, 1)),
                  pl.BlockSpec((1, ql), lambda i: (0, 0)),
                  pl.BlockSpec((ql, 2 * heads * LANES), lambda i: (0, 0)),
                  pl.BlockSpec((tm, LANES), lambda i: (i, 0)),
                  tab, tab],
        out_specs=[pl.BlockSpec((tm, heads * HEAD_SLAB), lambda i: (i, 0)),
                   pl.BlockSpec((tm, heads * LANES), lambda i: (i, 0))],
        out_shape=[jax.ShapeDtypeStruct((n, heads * HEAD_SLAB), BF16),
                   jax.ShapeDtypeStruct((n, heads * LANES), BF16)],
        compiler_params=_cp(("parallel",)),
        name="mla_kvproj",
    )(z, kvg.reshape(1, ql), wkv, zs, cos_t, sin_t)
    return q, k, v


ATTN_HEADS_PER_STEP = 2


def _attn_kernel(q_ref, k_ref, v_ref, o_ref, s_sc, m_sc, l_sc, acc_sc, *, t, hps):
    i = pl.program_id(2)
    nl = t // LANES

    def lane_fold(x, fn):
        r = x[:, :LANES]
        for c in range(1, nl):
            r = fn(r, x[:, c * LANES:(c + 1) * LANES])
        return r

    m_sc[...] = jnp.full(m_sc.shape, NEG, F32)

    def scores(j, masked):
        rows = pl.ds(pl.multiple_of(j * t, t), t)
        if masked:
            qpos = i * t + lax.broadcasted_iota(jnp.int32, (t, t), 0)
            kpos = j * t + lax.broadcasted_iota(jnp.int32, (t, t), 1)
            keep = (kpos <= qpos) & (kpos >= PADL)
        for h in range(hps):
            q = q_ref[:, h * HEAD_SLAB:(h + 1) * HEAD_SLAB]
            s = lax.dot_general(q, k_ref[rows, h * HEAD_SLAB:(h + 1) * HEAD_SLAB], NT, preferred_element_type=F32)
            if masked:
                s = jnp.where(keep, s, NEG)
            s_sc[h, j] = s
            m_sc[h] = jnp.maximum(m_sc[h], lane_fold(s, jnp.maximum))

    scores(0, True)

    def score_body(j, carry):
        scores(j, False)
        return carry

    lax.fori_loop(1, i, score_body, 0)

    @pl.when(i > 0)
    def _():
        scores(i, True)

    for h in range(hps):
        m_sc[h] = jnp.broadcast_to(jnp.max(m_sc[h], axis=1, keepdims=True), (t, LANES))
    l_sc[...] = jnp.zeros(l_sc.shape, F32)
    acc_sc[...] = jnp.zeros(acc_sc.shape, F32)

    def pv_body(j, carry):
        rows = pl.ds(pl.multiple_of(j * t, t), t)
        for h in range(hps):
            s = s_sc[h, j]
            mb = m_sc[h]
            p = jnp.concatenate([jnp.exp(s[:, c * LANES:(c + 1) * LANES] - mb) for c in range(nl)], axis=1)
            l_sc[h] = l_sc[h] + lane_fold(p, jnp.add)
            acc_sc[h] = acc_sc[h] + jnp.dot(p.astype(BF16), v_ref[rows, h * V_HEAD:(h + 1) * V_HEAD],
                                            preferred_element_type=F32)
        return carry

    lax.fori_loop(0, i + 1, pv_body, 0)

    for h in range(hps):
        l = jnp.sum(l_sc[h], axis=1, keepdims=True)
        o_ref[:, h * V_HEAD:(h + 1) * V_HEAD] = (acc_sc[h] / l).astype(o_ref.dtype)


def _attention(q, k, v, batch, tp):
    n = q.shape[0]
    heads = MLA_HEADS
    hps = ATTN_HEADS_PER_STEP
    assert heads % hps == 0
    t = _pick(tp, (384, 256, 128))
    nt = tp // t
    return pl.pallas_call(
        functools.partial(_attn_kernel, t=t, hps=hps),
        grid=(batch, heads // hps, nt),
        in_specs=[pl.BlockSpec((t, hps * HEAD_SLAB), lambda b, h, i: (b * nt + i, h)),
                  pl.BlockSpec((tp, hps * HEAD_SLAB), lambda b, h, i: (b, h)),
                  pl.BlockSpec((tp, hps * V_HEAD), lambda b, h, i: (b, h))],
        out_specs=pl.BlockSpec((t, hps * V_HEAD), lambda b, h, i: (b * nt + i, h)),
        out_shape=jax.ShapeDtypeStruct((n, heads * V_HEAD), BF16),
        scratch_shapes=[pltpu.VMEM((hps, nt, t, t), F32), pltpu.VMEM((hps, t, LANES), F32),
                        pltpu.VMEM((hps, t, LANES), F32), pltpu.VMEM((hps, t, V_HEAD), F32)],
        compiler_params=_cp(("parallel", "parallel", "parallel")),
        name="mla_attention",
    )(q, k, v)


HALO = 16


def _conv_kernel(x_ref, halo_ref, w_ref, b_ref, s_ref, o_ref, *, tm, tp):
    i = pl.program_id(0)
    start = lax.rem(i * tm, tp)
    pos = start + lax.broadcasted_iota(jnp.int32, (tm, 1), 0)
    x = jnp.where(pos >= PADL, x_ref[...].astype(F32), 0.0)
    hpos = start - HALO + lax.broadcasted_iota(jnp.int32, (HALO, 1), 0)
    halo = jnp.where(hpos >= PADL, halo_ref[...].astype(F32), 0.0)
    ext = jnp.concatenate([halo, x], axis=0)
    w = w_ref[...]
    y = b_ref[...] + w[CONV_K - 1:CONV_K, :] * x
    for j in range(1, CONV_K):
        y = y + w[CONV_K - 1 - j:CONV_K - j, :] * pltpu.roll(ext, j, 0)[HALO:, :]
    o_ref[...] = (_silu(y) * s_ref[...]).astype(BF16)


def _qk_conv(z, conv_w, conv_b, col_scale, tp, off_blk):
    n = z.shape[0]
    cw = conv_w.shape[1] // 2
    tm = _pick(tp, (384, 256, 128))
    return pl.pallas_call(
        functools.partial(_conv_kernel, tm=tm, tp=tp),
        grid=(n // tm, 2),
        in_specs=[pl.BlockSpec((tm, cw), lambda i, c: (i, off_blk + c)),
                  pl.BlockSpec((HALO, cw), lambda i, c: (jnp.maximum(i * (tm // HALO) - 1, 0), off_blk + c)),
                  pl.BlockSpec((CONV_K, cw), lambda i, c: (0, c)),
                  pl.BlockSpec((1, cw), lambda i, c: (0, c)),
                  pl.BlockSpec((1, cw), lambda i, c: (0, c))],
        out_specs=pl.BlockSpec((tm, cw), lambda i, c: (i, c)),
        out_shape=jax.ShapeDtypeStruct((n, 2 * cw), BF16),
        compiler_params=_cp(("parallel", "parallel")),
        name="mlstm_qk_conv",
    )(z, z, conv_w, conv_b.reshape(1, -1), col_scale)


MLSTM_HEADS_PER_STEP = 2


def _mlstm_kernel(*refs, group, heads, hps):
    q_refs, k_refs, v_refs = refs[:hps], refs[hps:2 * hps], refs[2 * hps:3 * hps]
    gt_ref = refs[3 * hps]
    og_refs = refs[3 * hps + 1:4 * hps + 1]
    ng_ref, o_ref, c_sc, n_sc, m_sc = refs[4 * hps + 1:]
    hp = pl.program_id(1)
    g = pl.program_id(2)
    L = CHUNK

    @pl.when(g == 0)
    def _():
        c_sc[...] = jnp.zeros(c_sc.shape, F32)
        n_sc[...] = jnp.zeros(n_sc.shape, F32)
        m_sc[...] = jnp.zeros(m_sc.shape, F32)

    R = group * L
    lane = lax.broadcasted_iota(jnp.int32, (R, LANES), 1)
    r_i = lax.broadcasted_iota(jnp.int32, (L, L), 0)
    c_i = lax.broadcasted_iota(jnp.int32, (L, L), 1)
    eye = r_i == c_i
    causal = c_i <= r_i

    gt = gt_ref[...]
    pad = (g * R + lax.broadcasted_iota(jnp.int32, (R, 1), 0)) < PADL
    i_parts, lf_parts = [], []
    for hh in range(hps):
        h = hp * hps + hh
        i_all = jnp.sum(jnp.where(lane == h, gt, 0.0), axis=1, keepdims=True)
        f_all = jnp.sum(jnp.where(lane == heads + h, gt, 0.0), axis=1, keepdims=True)
        i_parts.append(jnp.where(pad, NEG, i_all).reshape(group, L, 1))
        lf_parts.append(jnp.where(pad, 0.0, -(jnp.maximum(-f_all, 0.0) + jnp.log1p(jnp.exp(-jnp.abs(f_all)))))
                        .reshape(group, L, 1))
    i_col = jnp.concatenate(i_parts, axis=0)
    lf_col = jnp.concatenate(lf_parts, axis=0)
    b_row = jnp.sum(jnp.where((r_i <= c_i)[None], lf_col, 0.0), axis=1, keepdims=True)
    b_col = jnp.sum(jnp.where(eye[None], b_row, 0.0), axis=2, keepdims=True)
    i_row = jnp.sum(jnp.where(eye[None], i_col, 0.0), axis=1, keepdims=True)
    gtot = jnp.sum(lf_col, axis=1, keepdims=True)
    dmat = jnp.where(causal[None], b_col - b_row + i_row, NEG)
    rmax = jnp.max(dmat, axis=2, keepdims=True)
    a_col = gtot - b_col + i_col
    amax = jnp.max(a_col, axis=1, keepdims=True)

    m_prev_parts, m_new_parts = [], []
    for hh in range(hps):
        m = m_sc[hh].reshape(1, 1, 1)
        for c in range(group):
            idx = hh * group + c
            m_prev_parts.append(m)
            m = jnp.maximum(gtot[idx:idx + 1] + m, amax[idx:idx + 1])
            m_new_parts.append(m)
        m_sc[hh] = m.reshape(1, 1)
    m_prev = jnp.concatenate(m_prev_parts, axis=0)
    m_new = jnp.concatenate(m_new_parts, axis=0)
    inter = b_col + m_prev
    mrow = jnp.maximum(rmax, inter)
    dexp = jnp.exp(dmat - mrow)
    e_in = jnp.exp(inter - mrow)
    floor = jnp.exp(-mrow)
    ea = jnp.exp(a_col - m_new)
    decay = jnp.exp(gtot + m_prev - m_new)

    def stack(refs):
        return jnp.concatenate([r[...].reshape(group, L, r.shape[1]) for r in refs], axis=0)

    q3, k3, v3 = stack(q_refs), stack(k_refs), stack(v_refs)
    s3 = jnp.einsum("bqd,bkd->bqk", q3, k3, preferred_element_type=F32) * dexp
    intra = jnp.einsum("bqk,bkv->bqv", s3.astype(BF16), v3, preferred_element_type=F32)
    rowsum = jnp.sum(s3, axis=2, keepdims=True)
    ks3 = k3.astype(F32) * ea
    kv3 = jnp.einsum("bdk,bkv->bdv", jnp.swapaxes(ks3, 1, 2).astype(BF16), v3, preferred_element_type=F32)
    ksum = jnp.sum(ks3, axis=1, keepdims=True)

    c_parts, n_parts = [], []
    for hh in range(hps):
        cmat, nvec = c_sc[hh], n_sc[hh]
        for c in range(group):
            idx = hh * group + c
            c_parts.append(cmat[None])
            n_parts.append(nvec[None])
            cmat = decay[idx] * cmat + kv3[idx]
            nvec = decay[idx] * nvec + ksum[idx]
        c_sc[hh], n_sc[hh] = cmat, nvec
    c_prev = jnp.concatenate(c_parts, axis=0)
    n_prev = jnp.concatenate(n_parts, axis=0)
    num = intra + e_in * jnp.einsum("bqd,bdv->bqv", q3, c_prev.astype(BF16), preferred_element_type=F32)
    den = rowsum + e_in * jnp.sum(q3.astype(F32) * n_prev, axis=2, keepdims=True)
    hc = num / jnp.maximum(jnp.abs(den), floor)
    hc = hc * lax.rsqrt(jnp.mean(hc * hc, axis=2, keepdims=True) + RMS_EPS)
    for hh in range(hps):
        cols = slice(hh * ML_V, (hh + 1) * ML_V)
        hn = hc[hh * group:(hh + 1) * group].reshape(R, ML_V) * ng_ref[:, cols]
        o_ref[:, cols] = (hn * og_refs[hh][...].astype(F32)).astype(BF16)


def _mlstm(qk, z, zs, norm_g, batch, tp, v_blk, og_blk):
    n = qk.shape[0]
    heads = ML_HEADS
    hps = MLSTM_HEADS_PER_STEP
    assert heads % hps == 0
    nchunks = tp // CHUNK
    group = _pick(nchunks, (6, 5, 4, 3, 2, 1))
    rows = group * CHUNK
    ng = nchunks // group

    def col(width, base):
        return [pl.BlockSpec((rows, width), lambda b, p, g, u=u: (b * ng + g, base + p * hps + u)) for u in range(hps)]

    return pl.pallas_call(
        functools.partial(_mlstm_kernel, group=group, heads=heads, hps=hps),
        grid=(batch, heads // hps, ng),
        in_specs=(col(ML_QK, 0) + col(ML_QK, heads) + col(ML_V, v_blk)
                  + [pl.BlockSpec((rows, LANES), lambda b, p, g: (b * ng + g, 1))]
                  + col(ML_V, og_blk)
                  + [pl.BlockSpec((1, hps * ML_V), lambda b, p, g: (0, p))]),
        out_specs=pl.BlockSpec((rows, hps * ML_V), lambda b, p, g: (b * ng + g, p)),
        out_shape=jax.ShapeDtypeStruct((n, heads * ML_V), BF16),
        scratch_shapes=[pltpu.VMEM((hps, ML_QK, ML_V), F32), pltpu.VMEM((hps, 1, ML_QK), F32),
                        pltpu.VMEM((hps, 1, 1), F32)],
        compiler_params=_cp(("parallel", "parallel", "arbitrary")),
        name="mlstm_scan",
    )(*([qk] * (2 * hps) + [z] * hps + [zs] + [z] * hps + [norm_g.reshape(1, -1)]))


def _merge_kernel(a_ref, hm_ref, wa_ref, wb_ref, ga_ref, gb_ref, o_ref):
    ya = jnp.dot(a_ref[...], wa_ref[...], preferred_element_type=F32)
    yb = jnp.dot(hm_ref[...], wb_ref[...], preferred_element_type=F32)
    o_ref[...] = (ga_ref[...].astype(F32) * ya + gb_ref[...].astype(F32) * yb).astype(BF16)


def _merge(attn, hm, wa, wb, z, ga_off, gb_off):
    n = attn.shape[0]
    d = wa.shape[1]
    tm = _pick(n, (512, 256, 128))
    tn = _pick(d, (1024, 512, 256, 128))
    assert ga_off % tn == 0 and gb_off % tn == 0
    ga_blk, gb_blk = ga_off // tn, gb_off // tn
    return pl.pallas_call(
        _merge_kernel,
        grid=(d // tn, n // tm),
        in_specs=[pl.BlockSpec((tm, attn.shape[1]), lambda j, i: (i, 0)),
                  pl.BlockSpec((tm, hm.shape[1]), lambda j, i: (i, 0)),
                  pl.BlockSpec((wa.shape[0], tn), lambda j, i: (0, j)),
                  pl.BlockSpec((wb.shape[0], tn), lambda j, i: (0, j)),
                  pl.BlockSpec((tm, tn), lambda j, i: (i, ga_blk + j)),
                  pl.BlockSpec((tm, tn), lambda j, i: (i, gb_blk + j))],
        out_specs=pl.BlockSpec((tm, tn), lambda j, i: (i, j)),
        out_shape=jax.ShapeDtypeStruct((n, d), BF16),
        compiler_params=_cp(("parallel", "parallel")),
        name="branch_merge",
    )(attn, hm, wa, wb, z, z)


def _outln_kernel(y_ref, w_ref, h_ref, g_ref, b_ref, of_ref, os_ref):
    o = jnp.dot(y_ref[...], w_ref[...], preferred_element_type=F32)
    r = _ln_rows(ALPHA * h_ref[...] + o, g_ref[...], b_ref[...])
    of_ref[...] = r
    _store_slab(os_ref, r)


def _outproj_ln(y, w, h, g, b):
    n, d = h.shape
    tm = _pick(n, (256, 128))
    ns = d // HEAD_SLAB
    row = pl.BlockSpec((tm, d), lambda i: (i, 0))
    vec = pl.BlockSpec((1, d), lambda i: (0, 0))
    return pl.pallas_call(
        _outln_kernel,
        grid=(n // tm,),
        in_specs=[row, pl.BlockSpec((d, d), lambda i: (0, 0)), row, vec, vec],
        out_specs=[row, pl.BlockSpec((tm * ns, LANES), lambda i: (i, 0))],
        out_shape=[jax.ShapeDtypeStruct((n, d), F32), jax.ShapeDtypeStruct((n * ns, LANES), U32)],
        compiler_params=_cp(("parallel",)),
        name="outproj_ln",
    )(y, w, h, g.reshape(1, d), b.reshape(1, d))


def _router_kernel(x_ref, wh_ref, wl_ref, eb_ref, idx_ref, wt_ref, rank_ref, cnt_ref, base_sc, *, tm):
    i = pl.program_id(0)
    E, G = N_EXPERTS, N_GROUPS
    per = E // G

    @pl.when(i == 0)
    def _():
        base_sc[...] = jnp.zeros(base_sc.shape, F32)

    x = x_ref[...]
    xh = x.astype(BF16)
    xl = (x - xh.astype(F32)).astype(BF16)
    wh, wl = wh_ref[...], wl_ref[...]
    logits = (lax.dot_general(wh, xh, NT, preferred_element_type=F32)
              + lax.dot_general(wh, xl, NT, preferred_element_type=F32)
              + lax.dot_general(wl, xh, NT, preferred_element_type=F32))
    scores = jax.nn.sigmoid(logits)
    biased = scores + eb_ref[...]
    s3 = scores.reshape(G, per, tm)
    b3 = biased.reshape(G, per, tm)
    j_io = lax.broadcasted_iota(jnp.int32, (G, per, tm), 1).astype(F32)
    g_io3 = lax.broadcasted_iota(jnp.int32, (G, per, tm), 0).astype(F32)
    e_io = g_io3 * per + j_io
    g_io = lax.broadcasted_iota(jnp.int32, (G, 1, tm), 0).astype(F32)
    ninf = -jnp.inf

    m1 = jnp.max(b3, axis=1, keepdims=True)
    i1 = jnp.min(jnp.where(b3 == m1, j_io, float(per)), axis=1, keepdims=True)
    m2 = jnp.max(jnp.where(j_io == i1, ninf, b3), axis=1, keepdims=True)
    gs = m1 + m2
    gsel = jnp.zeros((G, 1, tm), F32)
    for _ in range(TOPK_GROUPS):
        gm = jnp.max(gs, axis=0, keepdims=True)
        gi = jnp.min(jnp.where(gs == gm, g_io, float(G)), axis=0, keepdims=True)
        hit = g_io == gi
        gsel = jnp.where(hit, 1.0, gsel)
        gs = jnp.where(hit, ninf, gs)
    masked = jnp.where(gsel > 0.0, b3, ninf)

    def red2(fn, a):
        return fn(fn(a, axis=1, keepdims=True), axis=0, keepdims=True)

    sel = jnp.zeros((G, per, tm), F32)
    idxs, scs = [], []
    for _ in range(TOP_K):
        mx = red2(jnp.max, masked)
        ei = red2(jnp.min, jnp.where(masked == mx, e_io, float(E)))
        hit = e_io == ei
        scs.append(red2(jnp.sum, jnp.where(hit, s3, 0.0)))
        idxs.append(ei)
        sel = jnp.where(hit, 1.0, sel)
        masked = jnp.where(hit, ninf, masked)
    wsum = scs[0]
    for k in range(1, TOP_K):
        wsum = wsum + scs[k]

    sel2 = sel.reshape(E, tm)
    upper = (lax.broadcasted_iota(jnp.int32, (tm, tm), 0) < lax.broadcasted_iota(jnp.int32, (tm, tm), 1))
    excl = jnp.dot(sel2.astype(BF16), upper.astype(BF16), preferred_element_type=F32)
    cnt3 = (excl + base_sc[...]).reshape(G, per, tm)
    for k in range(TOP_K):
        hit = e_io == idxs[k]
        rank = red2(jnp.sum, jnp.where(hit, cnt3, 0.0))
        idx_ref[k:k + 1, :] = idxs[k].reshape(1, tm).astype(jnp.int32)
        wt_ref[k:k + 1, :] = (scs[k] / wsum * ROUTED_SCALE).reshape(1, tm)
        rank_ref[k:k + 1, :] = rank.reshape(1, tm).astype(jnp.int32)
    base_sc[...] = base_sc[...] + jnp.sum(sel2, axis=1, keepdims=True)
    cnt_ref[...] = jnp.broadcast_to(base_sc[...], cnt_ref.shape).astype(jnp.int32)


def _router(h, wr_hi, wr_lo, e_bias):
    n, d = h.shape
    tm = _pick(n, (512, 256, 128))
    outk = pl.BlockSpec((TOP_K, tm), lambda i: (0, i))
    return pl.pallas_call(
        functools.partial(_router_kernel, tm=tm),
        grid=(n // tm,),
        in_specs=[pl.BlockSpec((tm, d), lambda i: (i, 0)),
                  pl.BlockSpec((N_EXPERTS, d), lambda i: (0, 0)),
                  pl.BlockSpec((N_EXPERTS, d), lambda i: (0, 0)),
                  pl.BlockSpec((N_EXPERTS, 1), lambda i: (0, 0))],
        out_specs=[outk, outk, outk, pl.BlockSpec((N_EXPERTS, LANES), lambda i: (0, 0))],
        out_shape=[jax.ShapeDtypeStruct((TOP_K, n), jnp.int32),
                   jax.ShapeDtypeStruct((TOP_K, n), F32),
                   jax.ShapeDtypeStruct((TOP_K, n), jnp.int32),
                   jax.ShapeDtypeStruct((N_EXPERTS, LANES), jnp.int32)],
        scratch_shapes=[pltpu.VMEM((N_EXPERTS, 1), F32)],
        compiler_params=_cp(("arbitrary",)),
        name="moe_router",
    )(h, wr_hi, wr_lo, e_bias.reshape(N_EXPERTS, 1))


def _row_copy(src, dst, sem):
    return pltpu.make_async_copy(src, dst, sem)


def _slab_rows(i, ns):
    return pl.ds(pl.multiple_of(i * ns, ns), ns)


def _dispatch_kernel(ps_ref, pc_ref, dest_ref, x_ref, xs_ref, zbuf, sem, *, tm, ns, blk, n_blocks):
    i = pl.program_id(0)

    @pl.when(i == 0)
    def _():
        zbuf[...] = jnp.zeros(zbuf.shape, U32)

        def run(start_slot, rows, wait):
            cp = _row_copy(zbuf.at[pl.ds(0, rows * ns), :],
                           xs_ref.at[pl.ds(pl.multiple_of(start_slot * ns, ns), rows * ns), :], sem.at[1])
            cp.wait() if wait else cp.start()

        def per_expert(e, carry):
            base, cnt = ps_ref[e], pc_ref[e]
            for wait in (False, True):
                for bit in range(blk.bit_length() - 1):
                    size = 1 << bit

                    @pl.when((cnt & size) != 0)
                    def _():
                        run(base + (cnt & (size - 1)), size, wait)
            return carry

        lax.fori_loop(0, N_EXPERTS, per_expert, 0)

        def per_block(j, carry):
            run(j * blk, blk, False)
            run(j * blk, blk, True)
            return carry

        lax.fori_loop(ps_ref[N_EXPERTS], n_blocks, per_block, 0)

    def copy(k, r):
        d = dest_ref[k, r]
        return _row_copy(x_ref.at[pl.ds(r * ns, ns), :], xs_ref.at[_slab_rows(d, ns), :], sem.at[0])

    for r in range(tm):
        for k in range(TOP_K):
            copy(k, r).start(priority=k % 2)

    def drain(r, carry):
        for k in range(TOP_K):
            copy(k, r).wait()
        return carry

    lax.fori_loop(0, tm, drain, 0)


def _dispatch(hs, dest, pad_start, pad_count, n_used, n_blocks):
    ns = hs.shape[0] // dest.shape[1]
    n = dest.shape[1]
    tm = LANES
    blk = EXPERT_ROWS
    assert blk & (blk - 1) == 0
    grid_spec = pltpu.PrefetchScalarGridSpec(
        num_scalar_prefetch=2,
        grid=(n // tm,),
        in_specs=[pl.BlockSpec((TOP_K, tm), lambda i, ps, pc: (0, i), memory_space=pltpu.SMEM),
                  pl.BlockSpec((tm * ns, LANES), lambda i, ps, pc: (i, 0))],
        out_specs=pl.BlockSpec(memory_space=pl.ANY),
        scratch_shapes=[pltpu.VMEM((blk * ns, LANES), U32), pltpu.SemaphoreType.DMA((2,))],
    )
    return pl.pallas_call(
        functools.partial(_dispatch_kernel, tm=tm, ns=ns, blk=blk, n_blocks=n_blocks),
        grid_spec=grid_spec,
        out_shape=jax.ShapeDtypeStruct((n_blocks * blk * ns, LANES), U32),
        compiler_params=_cp(("arbitrary",)),
        name="moe_dispatch",
    )(jnp.concatenate([pad_start, n_used]), pad_count, dest, hs)


def _expert_kernel(be_ref, nu_ref, x_ref, w1_ref, w3_ref, w2_ref, o_ref, w1b, w3b, w2b, *, tm, ns):
    j = pl.program_id(0)
    n_used = nu_ref[0]

    @pl.when(j < n_used)
    def _():
        @pl.when((j == 0) | (be_ref[j] != be_ref[jnp.maximum(j - 1, 0)]))
        def _():
            w1b[...] = w1_ref[...].astype(BF16)
            w3b[...] = w3_ref[...].astype(BF16)
            w2b[...] = w2_ref[...].astype(BF16)

        parts = []
        for s in range(ns):
            lo, hi = _load_slab_pairs(x_ref, (), s, tm, ns)
            parts += [lo.astype(BF16), hi.astype(BF16)]
        x = jnp.concatenate(parts, axis=1)
        a = jnp.dot(x, w1b[...], preferred_element_type=F32)
        b = jnp.dot(x, w3b[...], preferred_element_type=F32)
        hb = (_silu(a) * b).astype(BF16)
        _store_slab(o_ref, jnp.dot(hb, w2b[...], preferred_element_type=F32))

    @pl.when(j >= n_used)
    def _():
        o_ref[...] = jnp.zeros(o_ref.shape, U32)


def _experts(xs, blk_e, n_used, w1, w3, w2, layer):
    d, f = w1.shape[2], w1.shape[3]
    ns = d // HEAD_SLAB
    nb = blk_e.shape[0]
    tm = EXPERT_ROWS
    grid_spec = pltpu.PrefetchScalarGridSpec(
        num_scalar_prefetch=2,
        grid=(nb,),
        in_specs=[pl.BlockSpec((tm * ns, LANES), lambda j, be, nu: (jnp.minimum(j, nu[0] - 1), 0)),
                  pl.BlockSpec((None, None, d, f), lambda j, be, nu: (layer, be[j], 0, 0)),
                  pl.BlockSpec((None, None, d, f), lambda j, be, nu: (layer, be[j], 0, 0)),
                  pl.BlockSpec((None, None, f, d), lambda j, be, nu: (layer, be[j], 0, 0))],
        out_specs=pl.BlockSpec((tm * ns, LANES), lambda j, be, nu: (j, 0)),
        scratch_shapes=[pltpu.VMEM((d, f), BF16), pltpu.VMEM((d, f), BF16), pltpu.VMEM((f, d), BF16)],
    )
    return pl.pallas_call(
        functools.partial(_expert_kernel, tm=tm, ns=ns),
        grid_spec=grid_spec,
        out_shape=jax.ShapeDtypeStruct((nb * tm * ns, LANES), U32),
        compiler_params=_cp(("arbitrary",)),
        name="moe_experts",
    )(blk_e, n_used, xs, w1, w3, w2)


def _combine_kernel(dest_ref, dnxt_ref, h_ref, wt_ref, ws1_ref, ws3_ref, ws2_ref, g_ref, b_ref, ys_ref,
                    of_ref, *rest, tm, ns, nsteps):
    ob_ref = rest[0] if len(rest) == 3 else None
    gbuf, sem = rest[-2:]
    i = pl.program_id(0)
    slot = lax.rem(i, 2)

    def copy(d_ref, k, r, dst_slot):
        d = d_ref[k, r]
        return _row_copy(ys_ref.at[_slab_rows(d, ns), :], gbuf.at[dst_slot, k, _slab_rows(r, ns), :],
                         sem.at[dst_slot])

    def loop(d_ref, dst_slot, wait):
        def body(r, carry):
            for k in range(TOP_K):
                cp = copy(d_ref, k, r, dst_slot)
                if wait:
                    cp.wait()
                else:
                    cp.start(priority=k % 2)
            return carry

        lax.fori_loop(0, tm, body, 0)

    @pl.when(i == 0)
    def _():
        loop(dest_ref, slot, False)

    loop(dest_ref, slot, True)
    for r in range(tm):
        for k in range(TOP_K):
            copy(dnxt_ref, k, r, 1 - slot).start(priority=k % 2)
    h = h_ref[...]
    xb = h.astype(BF16)
    a = jnp.dot(xb, ws1_ref[...], preferred_element_type=F32)
    b = jnp.dot(xb, ws3_ref[...], preferred_element_type=F32)
    shared = jnp.dot((_silu(a) * b).astype(BF16), ws2_ref[...], preferred_element_type=F32)
    wt = wt_ref[...]
    parts = []
    for s in range(ns):
        lo = hi = None
        for k in range(TOP_K):
            plo, phi = _load_slab_pairs(gbuf, (slot, k), s, tm, ns)
            w = wt[:, k:k + 1]
            lo = w * plo if lo is None else lo + w * plo
            hi = w * phi if hi is None else hi + w * phi
        parts += [lo, hi]
    acc = shared + jnp.concatenate(parts, axis=1)
    r = _ln_rows(ALPHA * h + acc, g_ref[...], b_ref[...])
    of_ref[...] = r
    if ob_ref is not None:
        ob_ref[...] = r.astype(BF16)

    @pl.when(i == nsteps - 1)
    def _():
        loop(dnxt_ref, 1 - slot, True)


def _combine(h, dest, wt_tok, ws1, ws3, ws2, g, b, ys, final_shape=None):
    n, d = h.shape
    f = ws1.shape[1]
    tm = LANES
    row = pl.BlockSpec((tm, d), lambda i: (i, 0))
    vec = pl.BlockSpec((1, d), lambda i: (0, 0))
    if final_shape is None:
        out_specs = [row, row]
        out_shape = [jax.ShapeDtypeStruct((n, d), F32), jax.ShapeDtypeStruct((n, d), BF16)]
    else:
        batch, seq = final_shape
        nt = seq // tm + 1
        out_specs = [pl.BlockSpec((None, tm, d), lambda i: (i // nt, jnp.maximum(i % nt - 1, 0), 0))]
        out_shape = [jax.ShapeDtypeStruct((batch, seq, d), F32)]
    ns = d // HEAD_SLAB
    nsteps = n // tm
    return pl.pallas_call(
        functools.partial(_combine_kernel, tm=tm, ns=ns, nsteps=nsteps),
        grid=(nsteps,),
        in_specs=[pl.BlockSpec((TOP_K, tm), lambda i: (0, i), memory_space=pltpu.SMEM),
                  pl.BlockSpec((TOP_K, tm), lambda i: (0, jnp.minimum(i + 1, nsteps - 1)), memory_space=pltpu.SMEM),
                  row,
                  pl.BlockSpec((tm, TOP_K), lambda i: (i, 0)),
                  pl.BlockSpec((d, f), lambda i: (0, 0)),
                  pl.BlockSpec((d, f), lambda i: (0, 0)),
                  pl.BlockSpec((f, d), lambda i: (0, 0)),
                  vec, vec,
                  pl.BlockSpec(memory_space=pl.ANY)],
        out_specs=out_specs,
        out_shape=out_shape,
        scratch_shapes=[pltpu.VMEM((2, TOP_K, tm * ns, LANES), U32), pltpu.SemaphoreType.DMA((2,))],
        compiler_params=_cp(("arbitrary",)),
        name="moe_combine",
    )(dest, dest, h, wt_tok, ws1, ws3, ws2, g.reshape(1, d), b.reshape(1, d), ys)


def _moe(h_f32, hs, w_router, e_bias, w1, w3, w2, layer, ws1, ws3, ws2, g, b, final_shape=None):
    n = h_f32.shape[0]
    wr_t = w_router.T
    wr_hi = wr_t.astype(BF16)
    wr_lo = (wr_t - wr_hi.astype(F32)).astype(BF16)
    idx, wts, rank, cnt = _router(h_f32, wr_hi, wr_lo, e_bias)
    counts = cnt[:, 0]
    blk = EXPERT_ROWS
    pcounts = (counts + blk - 1) // blk * blk
    pends = jnp.cumsum(pcounts)
    pstarts = pends - pcounts
    e_ids = jnp.arange(N_EXPERTS, dtype=jnp.int32)
    start_of = jnp.sum(jnp.where(idx[None] == e_ids[:, None, None], pstarts[:, None, None], 0), axis=0)
    dest = (start_of + rank).astype(jnp.int32)
    nb = n * TOP_K // blk + N_EXPERTS
    blk_e = jnp.minimum(jnp.sum(pends[None, :] <= (jnp.arange(nb, dtype=jnp.int32) * blk)[:, None], axis=1),
                        N_EXPERTS - 1).astype(jnp.int32)
    n_used = (pends[-1:] // blk).astype(jnp.int32)
    xs = _dispatch(hs, dest, (pstarts + counts).astype(jnp.int32), (pcounts - counts).astype(jnp.int32), n_used, nb)
    ys = _experts(xs, blk_e, n_used, w1, w3, w2, layer)
    return _combine(h_f32, dest, wts.T, ws1.astype(BF16), ws3.astype(BF16), ws2.astype(BF16), g, b, ys,
                    final_shape)


def _in_proj_layout(w_in, b_in, d, ql, kvl):
    hq, hv = ML_HEADS * ML_QK, ML_HEADS * ML_V
    sizes = (ql, kvl, QK_ROPE, hq, hq, hv, hv, ML_HEADS, ML_HEADS, d, d)
    offs = np.concatenate([[0], np.cumsum(sizes)])
    seg = lambda a, i: a[..., int(offs[i]):int(offs[i + 1])]
    order = (0, 1, 3, 4, 5, 6, 9, 10)
    w_main = jnp.concatenate([seg(w_in, i) for i in order], axis=-1)
    b_main = jnp.concatenate([seg(b_in, i) for i in order], axis=-1)
    half = QK_ROPE // 2

    def small(a):
        kr = seg(a, 2)
        x1, x2 = kr[..., :half], kr[..., half:]
        pad = jnp.zeros(a.shape[:-1] + (LANES - 2 * ML_HEADS,), a.dtype)
        return jnp.concatenate([x1, x2, x2, x1, seg(a, 7), seg(a, 8), pad], axis=-1)

    main_offs = np.concatenate([[0], np.cumsum([sizes[i] for i in order])])
    return w_main, b_main, small(w_in), small(b_in), [int(o) for o in main_offs]


def _uq_layout(w_uq):
    kq = w_uq.shape[0]
    w = w_uq.reshape(kq, MLA_HEADS, QK_NOPE + QK_ROPE)
    half = QK_ROPE // 2
    nope, x1, x2 = w[..., :QK_NOPE], w[..., QK_NOPE:QK_NOPE + half], w[..., QK_NOPE + half:]
    return jnp.concatenate([nope, x1, x2, x2, x1], axis=-1).reshape(kq, MLA_HEADS * HEAD_SLAB)


def _ukv_layout(w_ukv):
    kk = w_ukv.shape[0]
    w = w_ukv.reshape(kk, MLA_HEADS, QK_NOPE + V_HEAD)
    return jnp.concatenate([w[..., :QK_NOPE].reshape(kk, -1), w[..., QK_NOPE:].reshape(kk, -1)], axis=-1)


def _rope_tables(tp):
    half = QK_ROPE // 2
    pos = jnp.arange(tp, dtype=F32) - PADL
    inv_freq = 1.0 / (ROPE_THETA ** (jnp.arange(0, QK_ROPE, 2, dtype=F32) / QK_ROPE))
    ang = pos[:, None] * inv_freq[None, :]
    cos, sin = jnp.cos(ang), jnp.sin(ang)
    zero = jnp.zeros((tp, LANES - 2 * half), F32)
    return jnp.concatenate([cos, cos, zero], axis=1), jnp.concatenate([-sin, sin, zero], axis=1)


def kernel(x, meta, ln_in_g, ln_in_b, w_in, b_in, q_norm_g, kv_norm_g, w_uq, w_ukv, conv_w, conv_b, ml_norm_g, w_br_mla, w_br_mlstm, w_out, ln1_g, ln1_b, w_router, e_bias, w1, w3, w2, ws1, ws3, ws2, ln2_g, ln2_b):
    batch, seq, d = x.shape
    depth = w_in.shape[0]
    ql, kvl = q_norm_g.shape[1], kv_norm_g.shape[1]
    assert ql == kvl and seq % LANES == 0
    tp = LANES + seq
    n = batch * tp
    hq = ML_HEADS * ML_QK

    head = jnp.concatenate([jnp.zeros((PADL, d), x.dtype), meta.astype(x.dtype)], axis=0)
    h_f32, h_bf = _layer_norm_in(x, head, ln_in_g, ln_in_b)
    cos_t, sin_t = _rope_tables(tp)
    col_scale = jnp.concatenate([jnp.ones((1, hq), F32), jnp.full((1, hq), ML_QK ** -0.5, F32)], axis=1)

    for l in range(depth):
        w_main, b_main, w_small, b_small, offs = _in_proj_layout(w_in[l], b_in[l], d, ql, kvl)
        o_mq, o_mv, o_mo, o_ga, o_gb = offs[2], offs[4], offs[5], offs[6], offs[7]
        z = _matmul_bias(h_bf, w_main.astype(BF16), b_main, BF16, sig_col=o_mo, name="in_proj")
        zs = _matmul_bias(h_bf, w_small.astype(BF16), b_small, F32, name="in_proj_small")

        q, k, v = _mla_qkv(z, zs, q_norm_g[l], kv_norm_g[l], _uq_layout(w_uq[l]).astype(BF16),
                           _ukv_layout(w_ukv[l]).astype(BF16), cos_t, sin_t, tp, ql)
        attn = _attention(q, k, v, batch, tp)

        assert o_mq % hq == 0 and o_mv % ML_V == 0 and o_mo % ML_V == 0
        qk = _qk_conv(z, conv_w[l], conv_b[l], col_scale, tp, o_mq // hq)
        hm = _mlstm(qk, z, zs, ml_norm_g[l], batch, tp, o_mv // ML_V, o_mo // ML_V)

        y = _merge(attn, hm, w_br_mla[l].astype(BF16), w_br_mlstm[l].astype(BF16), z, o_ga, o_gb)
        h_f32, hs = _outproj_ln(y, w_out[l].astype(BF16), h_f32, ln1_g[l], ln1_b[l])
        outs = _moe(h_f32, hs, w_router[l], e_bias[l], w1, w3, w2, l, ws1[l], ws3[l], ws2[l],
                    ln2_g[l], ln2_b[l], final_shape=(batch, seq) if l == depth - 1 else None)
        if l == depth - 1:
            return outs[0]
        h_f32, h_bf = outs
```

```python
import functools

import numpy as np
import jax
import jax.numpy as jnp
from jax import lax
from jax.experimental import pallas as pl
from jax.experimental.pallas import tpu as pltpu

N_META = 16
MLA_HEADS = 8
QK_NOPE = 128
QK_ROPE = 64
V_HEAD = 128
ROPE_THETA = 10000.0
ML_HEADS = 8
ML_QK = 128
ML_V = 256
CONV_K = 4
CHUNK = 64
N_EXPERTS = 64
TOP_K = 8
N_GROUPS = 8
TOPK_GROUPS = 4
ROUTED_SCALE = 2.5
DEPTH = 2
ALPHA = (2 * DEPTH) ** 0.25
LN_EPS = 1e-5
RMS_EPS = 1e-6
NEG = -1e30

LANES = 128
HEAD_SLAB = 2 * LANES
PADL = LANES - N_META
EXPERT_ROWS = 512
VMEM_LIMIT = 56 * 1024 * 1024

F32 = jnp.float32
BF16 = jnp.bfloat16
U32 = jnp.uint32
NT = (((1,), (1,)), ((), ()))
TN = (((0,), (0,)), ((), ()))


def _pick(n, cands):
    for c in cands:
        if n % c == 0:
            return c
    raise ValueError(f"no tile in {cands} divides {n}")


def _cp(sem, vmem=None):
    return pltpu.CompilerParams(dimension_semantics=sem, vmem_limit_bytes=vmem or VMEM_LIMIT)


def _ln_rows(x, g, b):
    mu = jnp.mean(x, axis=-1, keepdims=True)
    xc = x - mu
    var = jnp.mean(xc * xc, axis=-1, keepdims=True)
    return xc * lax.rsqrt(var + LN_EPS) * g + b


def _rms_rows(x, g):
    return x * lax.rsqrt(jnp.mean(x * x, axis=-1, keepdims=True) + RMS_EPS) * g


def _silu(x):
    return x * jax.nn.sigmoid(x)


def _store_slab(ref, x):
    rows, ns = x.shape[0], x.shape[1] // HEAD_SLAB
    for s in range(ns):
        lo = lax.bitcast_convert_type(x[:, s * HEAD_SLAB:s * HEAD_SLAB + LANES].astype(BF16).astype(F32), U32)
        hi = lax.bitcast_convert_type(x[:, s * HEAD_SLAB + LANES:(s + 1) * HEAD_SLAB].astype(BF16).astype(F32), U32)
        ref[pl.ds(s, rows, stride=ns), :] = (lo >> 16) | hi


def _load_slab_pairs(ref, lead, s, rows, ns):
    u = ref[lead + (pl.ds(s, rows, stride=ns), slice(None))]
    return (lax.bitcast_convert_type(u << 16, F32),
            lax.bitcast_convert_type(u & jnp.uint32(0xFFFF0000), F32))


def _ln_in_kernel(head_ref, x_ref, g_ref, b_ref, of_ref, ob_ref):
    i = pl.program_id(1)

    def emit(src):
        y = _ln_rows(src, g_ref[...], b_ref[...])
        of_ref[...] = y
        ob_ref[...] = y.astype(BF16)

    @pl.when(i == 0)
    def _():
        emit(head_ref[...])

    @pl.when(i > 0)
    def _():
        emit(x_ref[...])


def _layer_norm_in(x, head, g, b):
    batch, seq, d = x.shape
    nt = seq // LANES + 1
    row = pl.BlockSpec((LANES, d), lambda bb, i: (bb * nt + i, 0))
    vec = pl.BlockSpec((1, d), lambda bb, i: (0, 0))
    n = batch * nt * LANES
    return pl.pallas_call(
        _ln_in_kernel,
        grid=(batch, nt),
        in_specs=[pl.BlockSpec((LANES, d), lambda bb, i: (0, 0)),
                  pl.BlockSpec((None, LANES, d), lambda bb, i: (bb, jnp.maximum(i - 1, 0), 0)),
                  vec, vec],
        out_specs=[row, row],
        out_shape=[jax.ShapeDtypeStruct((n, d), F32), jax.ShapeDtypeStruct((n, d), BF16)],
        compiler_params=_cp(("parallel", "arbitrary")),
        name="ln_in",
    )(head, x, g.reshape(1, d), b.reshape(1, d))


def _mm_kernel(x_ref, w_ref, b_ref, o_ref, *, sig_tile):
    acc = jnp.dot(x_ref[...], w_ref[...], preferred_element_type=F32) + b_ref[...]
    if sig_tile is None:
        o_ref[...] = acc.astype(o_ref.dtype)
        return
    j = pl.program_id(0)

    @pl.when(j < sig_tile)
    def _():
        o_ref[...] = acc.astype(o_ref.dtype)

    @pl.when(j >= sig_tile)
    def _():
        o_ref[...] = jax.nn.sigmoid(acc).astype(o_ref.dtype)


def _matmul_bias(x, w, b, out_dtype, sig_col=None, name="mm"):
    n, kd = x.shape
    nc = w.shape[1]
    tm = _pick(n, (1024, 512, 256, 128))
    tn = _pick(nc, (1024, 512, 256, 128))
    sig_tile = None
    if sig_col is not None:
        assert sig_col % tn == 0
        sig_tile = sig_col // tn
    return pl.pallas_call(
        functools.partial(_mm_kernel, sig_tile=sig_tile),
        grid=(nc // tn, n // tm),
        in_specs=[pl.BlockSpec((tm, kd), lambda j, i: (i, 0)),
                  pl.BlockSpec((kd, tn), lambda j, i: (0, j)),
                  pl.BlockSpec((1, tn), lambda j, i: (0, j))],
        out_specs=pl.BlockSpec((tm, tn), lambda j, i: (i, j)),
        out_shape=jax.ShapeDtypeStruct((n, nc), out_dtype),
        compiler_params=_cp(("parallel", "parallel")),
        name=name,
    )(x, w, b.reshape(1, nc))


def _rope_slab(r, cos_t, sin_t):
    return r * cos_t + pltpu.roll(r, 2 * (QK_ROPE // 2), 1) * sin_t


def _qproj_kernel(c_ref, g_ref, w_ref, cos_ref, sin_ref, o_ref, *, heads, scale):
    xn = _rms_rows(c_ref[...].astype(F32), g_ref[...])
    q = jnp.dot(xn.astype(BF16), w_ref[...], preferred_element_type=F32) * scale
    cos_t, sin_t = cos_ref[...], sin_ref[...]
    for h in range(heads):
        lo = h * HEAD_SLAB
        o_ref[:, lo:lo + LANES] = q[:, lo:lo + LANES].astype(BF16)
        o_ref[:, lo + LANES:lo + HEAD_SLAB] = _rope_slab(q[:, lo + LANES:lo + HEAD_SLAB], cos_t, sin_t).astype(BF16)


def _kproj_kernel(c_ref, g_ref, w_ref, kr_ref, cos_ref, sin_ref, k_ref, v_ref, *, heads):
    xn = _rms_rows(c_ref[...].astype(F32), g_ref[...])
    kv = jnp.dot(xn.astype(BF16), w_ref[...], preferred_element_type=F32)
    rr_t = _rope_slab(kr_ref[...], cos_ref[...], sin_ref[...]).T.astype(BF16)
    for h in range(heads):
        lo = h * HEAD_SLAB
        k_ref[lo:lo + LANES, :] = kv[:, h * LANES:(h + 1) * LANES].T.astype(BF16)
        k_ref[lo + LANES:lo + HEAD_SLAB, :] = rr_t
    v_ref[...] = kv[:, heads * LANES:].astype(BF16)


def _mla_qkv(z, zs, qg, kvg, wq, wkv, cos_t, sin_t, tp, ql):
    n = z.shape[0]
    heads = MLA_HEADS
    tm = _pick(tp, (384, 256, 128))
    nt = tp // tm
    scale = (QK_NOPE + QK_ROPE) ** -0.5 * float(np.log2(np.e))
    tab = pl.BlockSpec((tm, LANES), lambda i: (i % nt, 0))
    q = pl.pallas_call(
        functools.partial(_qproj_kernel, heads=heads, scale=scale),
        grid=(n // tm,),
        in_specs=[pl.BlockSpec((tm, ql), lambda i: (i, 0)),
                  pl.BlockSpec((1, ql), lambda i: (0, 0)),
                  pl.BlockSpec((ql, heads * HEAD_SLAB), lambda i: (0, 0)),
                  tab, tab],
        out_specs=pl.BlockSpec((tm, heads * HEAD_SLAB), lambda i: (i, 0)),
        out_shape=jax.ShapeDtypeStruct((n, heads * HEAD_SLAB), BF16),
        compiler_params=_cp(("parallel",)),
        name="mla_qproj",
    )(z, qg.reshape(1, ql), wq, cos_t, sin_t)
    k, v = pl.pallas_call(
        functools.partial(_kproj_kernel, heads=heads),
        grid=(n // tm,),
        in_specs=[pl.BlockSpec((tm, ql), lambda i: (i, 1)),
                  pl.BlockSpec((1, ql), lambda i: (0, 0)),
                  pl.BlockSpec((ql, 2 * heads * LANES), lambda i: (0, 0)),
                  pl.BlockSpec((tm, LANES), lambda i: (i, 0)),
                  tab, tab],
        out_specs=[pl.BlockSpec((heads * HEAD_SLAB, tm), lambda i: (0, i)),
                   pl.BlockSpec((tm, heads * LANES), lambda i: (i, 0))],
        out_shape=[jax.ShapeDtypeStruct((heads * HEAD_SLAB, n), BF16),
                   jax.ShapeDtypeStruct((n, heads * LANES), BF16)],
        compiler_params=_cp(("parallel",)),
        name="mla_kvproj",
    )(z, kvg.reshape(1, ql), wkv, zs, cos_t, sin_t)
    return q, k, v


ATTN_HEADS_PER_STEP = 2


def _attn_kernel(q_ref, k_ref, v_ref, o_ref, s_sc, m_sc, l_sc, acc_sc, *, t, hps):
    i = pl.program_id(2)
    nl = t // LANES

    def lane_fold(x, fn):
        r = x[:, :LANES]
        for c in range(1, nl):
            r = fn(r, x[:, c * LANES:(c + 1) * LANES])
        return r

    m_sc[...] = jnp.full(m_sc.shape, NEG, F32)

    def scores(j, masked):
        rows = pl.ds(pl.multiple_of(j * t, t), t)
        if masked:
            qpos = i * t + lax.broadcasted_iota(jnp.int32, (t, t), 0)
            kpos = j * t + lax.broadcasted_iota(jnp.int32, (t, t), 1)
            keep = (kpos <= qpos) & (kpos >= PADL)
        for h in range(hps):
            q = q_ref[:, h * HEAD_SLAB:(h + 1) * HEAD_SLAB]
            s = jnp.dot(q, k_ref[h * HEAD_SLAB:(h + 1) * HEAD_SLAB, rows], preferred_element_type=F32)
            if masked:
                s = jnp.where(keep, s, NEG)
            s_sc[h, j] = s
            m_sc[h] = jnp.maximum(m_sc[h], lane_fold(s, jnp.maximum))

    scores(0, True)

    def score_body(j, carry):
        scores(j, False)
        return carry

    lax.fori_loop(1, i, score_body, 0)

    @pl.when(i > 0)
    def _():
        scores(i, True)

    for h in range(hps):
        m_sc[h] = jnp.broadcast_to(jnp.max(m_sc[h], axis=1, keepdims=True), (t, LANES))
    l_sc[...] = jnp.zeros(l_sc.shape, F32)
    acc_sc[...] = jnp.zeros(acc_sc.shape, F32)

    def pv_body(j, carry):
        rows = pl.ds(pl.multiple_of(j * t, t), t)
        for h in range(hps):
            s = s_sc[h, j]
            mb = m_sc[h]
            p = jnp.concatenate([jnp.exp2(s[:, c * LANES:(c + 1) * LANES] - mb) for c in range(nl)], axis=1)
            l_sc[h] = l_sc[h] + lane_fold(p, jnp.add)
            acc_sc[h] = acc_sc[h] + jnp.dot(p.astype(BF16), v_ref[rows, h * V_HEAD:(h + 1) * V_HEAD],
                                            preferred_element_type=F32)
        return carry

    lax.fori_loop(0, i + 1, pv_body, 0)

    for h in range(hps):
        l = jnp.sum(l_sc[h], axis=1, keepdims=True)
        o_ref[:, h * V_HEAD:(h + 1) * V_HEAD] = (acc_sc[h] / l).astype(o_ref.dtype)


def _attention(q, k, v, batch, tp):
    n = q.shape[0]
    heads = MLA_HEADS
    hps = ATTN_HEADS_PER_STEP
    assert heads % hps == 0
    t = _pick(tp, (384, 256, 128))
    nt = tp // t
    return pl.pallas_call(
        functools.partial(_attn_kernel, t=t, hps=hps),
        grid=(batch, heads // hps, nt),
        in_specs=[pl.BlockSpec((t, hps * HEAD_SLAB), lambda b, h, i: (b * nt + i, h)),
                  pl.BlockSpec((hps * HEAD_SLAB, tp), lambda b, h, i: (h, b)),
                  pl.BlockSpec((tp, hps * V_HEAD), lambda b, h, i: (b, h))],
        out_specs=pl.BlockSpec((t, hps * V_HEAD), lambda b, h, i: (b * nt + i, h)),
        out_shape=jax.ShapeDtypeStruct((n, heads * V_HEAD), BF16),
        scratch_shapes=[pltpu.VMEM((hps, nt, t, t), F32), pltpu.VMEM((hps, t, LANES), F32),
                        pltpu.VMEM((hps, t, LANES), F32), pltpu.VMEM((hps, t, V_HEAD), F32)],
        compiler_params=_cp(("parallel", "parallel", "parallel")),
        name="mla_attention",
    )(q, k, v)


HALO = 16


def _conv_kernel(x_ref, halo_ref, w_ref, b_ref, s_ref, o_ref, *, tm, tp):
    i = pl.program_id(0)
    start = lax.rem(i * tm, tp)
    pos = start + lax.broadcasted_iota(jnp.int32, (tm, 1), 0)
    x = jnp.where(pos >= PADL, x_ref[...].astype(F32), 0.0)
    hpos = start - HALO + lax.broadcasted_iota(jnp.int32, (HALO, 1), 0)
    halo = jnp.where(hpos >= PADL, halo_ref[...].astype(F32), 0.0)
    ext = jnp.concatenate([halo, x], axis=0)
    w = w_ref[...]
    y = b_ref[...] + w[CONV_K - 1:CONV_K, :] * x
    for j in range(1, CONV_K):
        y = y + w[CONV_K - 1 - j:CONV_K - j, :] * pltpu.roll(ext, j, 0)[HALO:, :]
    o_ref[...] = (_silu(y) * s_ref[...]).astype(BF16)


def _qk_conv(z, conv_w, conv_b, col_scale, tp, off_blk):
    n = z.shape[0]
    cw = conv_w.shape[1] // 2
    tm = _pick(tp, (384, 256, 128))
    return pl.pallas_call(
        functools.partial(_conv_kernel, tm=tm, tp=tp),
        grid=(n // tm, 2),
        in_specs=[pl.BlockSpec((tm, cw), lambda i, c: (i, off_blk + c)),
                  pl.BlockSpec((HALO, cw), lambda i, c: (jnp.maximum(i * (tm // HALO) - 1, 0), off_blk + c)),
                  pl.BlockSpec((CONV_K, cw), lambda i, c: (0, c)),
                  pl.BlockSpec((1, cw), lambda i, c: (0, c)),
                  pl.BlockSpec((1, cw), lambda i, c: (0, c))],
        out_specs=pl.BlockSpec((tm, cw), lambda i, c: (i, c)),
        out_shape=jax.ShapeDtypeStruct((n, 2 * cw), BF16),
        compiler_params=_cp(("parallel", "parallel")),
        name="mlstm_qk_conv",
    )(z, z, conv_w, conv_b.reshape(1, -1), col_scale)


MLSTM_HEADS_PER_STEP = 2


def _mlstm_kernel(*refs, group, heads, hps):
    q_refs, k_refs, v_refs = refs[:hps], refs[hps:2 * hps], refs[2 * hps:3 * hps]
    gt_ref = refs[3 * hps]
    og_refs = refs[3 * hps + 1:4 * hps + 1]
    ng_ref, o_ref, c_sc, n_sc, m_sc = refs[4 * hps + 1:]
    hp = pl.program_id(1)
    g = pl.program_id(2)
    L = CHUNK

    @pl.when(g == 0)
    def _():
        c_sc[...] = jnp.zeros(c_sc.shape, F32)
        n_sc[...] = jnp.zeros(n_sc.shape, F32)
        m_sc[...] = jnp.zeros(m_sc.shape, F32)

    R = group * L
    lane = lax.broadcasted_iota(jnp.int32, (R, LANES), 1)
    r_i = lax.broadcasted_iota(jnp.int32, (L, L), 0)
    c_i = lax.broadcasted_iota(jnp.int32, (L, L), 1)
    eye = r_i == c_i
    causal = c_i <= r_i

    gt = gt_ref[...]
    pad = (g * R + lax.broadcasted_iota(jnp.int32, (R, 1), 0)) < PADL
    i_parts, lf_parts = [], []
    for hh in range(hps):
        h = hp * hps + hh
        i_all = jnp.sum(jnp.where(lane == h, gt, 0.0), axis=1, keepdims=True)
        f_all = jnp.sum(jnp.where(lane == heads + h, gt, 0.0), axis=1, keepdims=True)
        i_parts.append(jnp.where(pad, NEG, i_all).reshape(group, L, 1))
        lf_parts.append(jnp.where(pad, 0.0, -(jnp.maximum(-f_all, 0.0) + jnp.log1p(jnp.exp(-jnp.abs(f_all)))))
                        .reshape(group, L, 1))
    i_col = jnp.concatenate(i_parts, axis=0)
    lf_col = jnp.concatenate(lf_parts, axis=0)
    b_row = jnp.sum(jnp.where((r_i <= c_i)[None], lf_col, 0.0), axis=1, keepdims=True)
    b_col = jnp.sum(jnp.where(eye[None], b_row, 0.0), axis=2, keepdims=True)
    i_row = jnp.sum(jnp.where(eye[None], i_col, 0.0), axis=1, keepdims=True)
    gtot = jnp.sum(lf_col, axis=1, keepdims=True)
    dmat = jnp.where(causal[None], b_col - b_row + i_row, NEG)
    rmax = jnp.max(dmat, axis=2, keepdims=True)
    a_col = gtot - b_col + i_col
    amax = jnp.max(a_col, axis=1, keepdims=True)

    m_prev_parts, m_new_parts = [], []
    for hh in range(hps):
        m = m_sc[hh].reshape(1, 1, 1)
        for c in range(group):
            idx = hh * group + c
            m_prev_parts.append(m)
            m = jnp.maximum(gtot[idx:idx + 1] + m, amax[idx:idx + 1])
            m_new_parts.append(m)
        m_sc[hh] = m.reshape(1, 1)
    m_prev = jnp.concatenate(m_prev_parts, axis=0)
    m_new = jnp.concatenate(m_new_parts, axis=0)
    inter = b_col + m_prev
    mrow = jnp.maximum(rmax, inter)
    dexp = jnp.exp(dmat - mrow)
    e_in = jnp.exp(inter - mrow)
    floor = jnp.exp(-mrow)
    ea = jnp.exp(a_col - m_new)
    decay = jnp.exp(gtot + m_prev - m_new)

    def stack(refs):
        return jnp.concatenate([r[...].reshape(group, L, r.shape[1]) for r in refs], axis=0)

    q3, k3, v3 = stack(q_refs), stack(k_refs), stack(v_refs)
    s3 = jnp.einsum("bqd,bkd->bqk", q3, k3, preferred_element_type=F32) * dexp
    intra = jnp.einsum("bqk,bkv->bqv", s3.astype(BF16), v3, preferred_element_type=F32)
    rowsum = jnp.sum(s3, axis=2, keepdims=True)
    ks3 = k3.astype(F32) * ea
    kv3 = jnp.einsum("bdk,bkv->bdv", jnp.swapaxes(ks3, 1, 2).astype(BF16), v3, preferred_element_type=F32)
    ksum = jnp.sum(ks3, axis=1, keepdims=True)

    c_parts, n_parts = [], []
    for hh in range(hps):
        cmat, nvec = c_sc[hh], n_sc[hh]
        for c in range(group):
            idx = hh * group + c
            c_parts.append(cmat[None])
            n_parts.append(nvec[None])
            cmat = decay[idx] * cmat + kv3[idx]
            nvec = decay[idx] * nvec + ksum[idx]
        c_sc[hh], n_sc[hh] = cmat, nvec
    c_prev = jnp.concatenate(c_parts, axis=0)
    n_prev = jnp.concatenate(n_parts, axis=0)
    num = intra + e_in * jnp.einsum("bqd,bdv->bqv", q3, c_prev.astype(BF16), preferred_element_type=F32)
    den = rowsum + e_in * jnp.sum(q3.astype(F32) * n_prev, axis=2, keepdims=True)
    hc = num / jnp.maximum(jnp.abs(den), floor)
    hc = hc * lax.rsqrt(jnp.mean(hc * hc, axis=2, keepdims=True) + RMS_EPS)
    for hh in range(hps):
        cols = slice(hh * ML_V, (hh + 1) * ML_V)
        hn = hc[hh * group:(hh + 1) * group].reshape(R, ML_V) * ng_ref[:, cols]
        o_ref[:, cols] = (hn * og_refs[hh][...].astype(F32)).astype(BF16)


def _mlstm(qk, z, zs, norm_g, batch, tp, v_blk, og_blk):
    n = qk.shape[0]
    heads = ML_HEADS
    hps = MLSTM_HEADS_PER_STEP
    assert heads % hps == 0
    nchunks = tp // CHUNK
    group = _pick(nchunks, (6, 5, 4, 3, 2, 1))
    rows = group * CHUNK
    ng = nchunks // group

    def col(width, base):
        return [pl.BlockSpec((rows, width), lambda b, p, g, u=u: (b * ng + g, base + p * hps + u)) for u in range(hps)]

    return pl.pallas_call(
        functools.partial(_mlstm_kernel, group=group, heads=heads, hps=hps),
        grid=(batch, heads // hps, ng),
        in_specs=(col(ML_QK, 0) + col(ML_QK, heads) + col(ML_V, v_blk)
                  + [pl.BlockSpec((rows, LANES), lambda b, p, g: (b * ng + g, 1))]
                  + col(ML_V, og_blk)
                  + [pl.BlockSpec((1, hps * ML_V), lambda b, p, g: (0, p))]),
        out_specs=pl.BlockSpec((rows, hps * ML_V), lambda b, p, g: (b * ng + g, p)),
        out_shape=jax.ShapeDtypeStruct((n, heads * ML_V), BF16),
        scratch_shapes=[pltpu.VMEM((hps, ML_QK, ML_V), F32), pltpu.VMEM((hps, 1, ML_QK), F32),
                        pltpu.VMEM((hps, 1, 1), F32)],
        compiler_params=_cp(("parallel", "parallel", "arbitrary")),
        name="mlstm_scan",
    )(*([qk] * (2 * hps) + [z] * hps + [zs] + [z] * hps + [norm_g.reshape(1, -1)]))


def _merge_kernel(a_ref, hm_ref, wa_ref, wb_ref, ga_ref, gb_ref, o_ref):
    ya = jnp.dot(a_ref[...], wa_ref[...], preferred_element_type=F32)
    yb = jnp.dot(hm_ref[...], wb_ref[...], preferred_element_type=F32)
    o_ref[...] = (ga_ref[...].astype(F32) * ya + gb_ref[...].astype(F32) * yb).astype(BF16)


def _merge(attn, hm, wa, wb, z, ga_off, gb_off):
    n = attn.shape[0]
    d = wa.shape[1]
    tm = _pick(n, (512, 256, 128))
    tn = _pick(d, (1024, 512, 256, 128))
    assert ga_off % tn == 0 and gb_off % tn == 0
    ga_blk, gb_blk = ga_off // tn, gb_off // tn
    return pl.pallas_call(
        _merge_kernel,
        grid=(d // tn, n // tm),
        in_specs=[pl.BlockSpec((tm, attn.shape[1]), lambda j, i: (i, 0)),
                  pl.BlockSpec((tm, hm.shape[1]), lambda j, i: (i, 0)),
                  pl.BlockSpec((wa.shape[0], tn), lambda j, i: (0, j)),
                  pl.BlockSpec((wb.shape[0], tn), lambda j, i: (0, j)),
                  pl.BlockSpec((tm, tn), lambda j, i: (i, ga_blk + j)),
                  pl.BlockSpec((tm, tn), lambda j, i: (i, gb_blk + j))],
        out_specs=pl.BlockSpec((tm, tn), lambda j, i: (i, j)),
        out_shape=jax.ShapeDtypeStruct((n, d), BF16),
        compiler_params=_cp(("parallel", "parallel")),
        name="branch_merge",
    )(attn, hm, wa, wb, z, z)


def _outln_kernel(y_ref, w_ref, h_ref, g_ref, b_ref, of_ref, os_ref):
    o = jnp.dot(y_ref[...], w_ref[...], preferred_element_type=F32)
    r = _ln_rows(ALPHA * h_ref[...] + o, g_ref[...], b_ref[...])
    of_ref[...] = r
    _store_slab(os_ref, r)


def _outproj_ln(y, w, h, g, b):
    n, d = h.shape
    tm = _pick(n, (256, 128))
    ns = d // HEAD_SLAB
    row = pl.BlockSpec((tm, d), lambda i: (i, 0))
    vec = pl.BlockSpec((1, d), lambda i: (0, 0))
    return pl.pallas_call(
        _outln_kernel,
        grid=(n // tm,),
        in_specs=[row, pl.BlockSpec((d, d), lambda i: (0, 0)), row, vec, vec],
        out_specs=[row, pl.BlockSpec((tm * ns, LANES), lambda i: (i, 0))],
        out_shape=[jax.ShapeDtypeStruct((n, d), F32), jax.ShapeDtypeStruct((n * ns, LANES), U32)],
        compiler_params=_cp(("parallel",)),
        name="outproj_ln",
    )(y, w, h, g.reshape(1, d), b.reshape(1, d))


def _router_kernel(x_ref, wh_ref, wl_ref, eb_ref, idx_ref, wt_ref, rank_ref, cnt_ref, base_sc, *, tm):
    i = pl.program_id(0)
    E, G = N_EXPERTS, N_GROUPS
    per = E // G

    @pl.when(i == 0)
    def _():
        base_sc[...] = jnp.zeros(base_sc.shape, F32)

    x = x_ref[...]
    xh = x.astype(BF16)
    xl = (x - xh.astype(F32)).astype(BF16)
    wh, wl = wh_ref[...], wl_ref[...]
    logits = (lax.dot_general(wh, xh, NT, preferred_element_type=F32)
              + lax.dot_general(wh, xl, NT, preferred_element_type=F32)
              + lax.dot_general(wl, xh, NT, preferred_element_type=F32))
    scores = jax.nn.sigmoid(logits)
    biased = scores + eb_ref[...]
    s3 = scores.reshape(G, per, tm)
    b3 = biased.reshape(G, per, tm)
    j_io = lax.broadcasted_iota(jnp.int32, (G, per, tm), 1).astype(F32)
    g_io3 = lax.broadcasted_iota(jnp.int32, (G, per, tm), 0).astype(F32)
    e_io = g_io3 * per + j_io
    g_io = lax.broadcasted_iota(jnp.int32, (G, 1, tm), 0).astype(F32)
    ninf = -jnp.inf

    m1 = jnp.max(b3, axis=1, keepdims=True)
    i1 = jnp.min(jnp.where(b3 == m1, j_io, float(per)), axis=1, keepdims=True)
    m2 = jnp.max(jnp.where(j_io == i1, ninf, b3), axis=1, keepdims=True)
    gs = m1 + m2
    gsel = jnp.zeros((G, 1, tm), F32)
    for _ in range(TOPK_GROUPS):
        gm = jnp.max(gs, axis=0, keepdims=True)
        gi = jnp.min(jnp.where(gs == gm, g_io, float(G)), axis=0, keepdims=True)
        hit = g_io == gi
        gsel = jnp.where(hit, 1.0, gsel)
        gs = jnp.where(hit, ninf, gs)
    masked = jnp.where(gsel > 0.0, b3, ninf)

    def red2(fn, a):
        return fn(fn(a, axis=1, keepdims=True), axis=0, keepdims=True)

    sel = jnp.zeros((G, per, tm), F32)
    idxs, scs = [], []
    for _ in range(TOP_K):
        mx = red2(jnp.max, masked)
        ei = red2(jnp.min, jnp.where(masked == mx, e_io, float(E)))
        hit = e_io == ei
        scs.append(red2(jnp.sum, jnp.where(hit, s3, 0.0)))
        idxs.append(ei)
        sel = jnp.where(hit, 1.0, sel)
        masked = jnp.where(hit, ninf, masked)
    wsum = scs[0]
    for k in range(1, TOP_K):
        wsum = wsum + scs[k]

    sel2 = sel.reshape(E, tm)
    upper = (lax.broadcasted_iota(jnp.int32, (tm, tm), 0) < lax.broadcasted_iota(jnp.int32, (tm, tm), 1))
    excl = jnp.dot(sel2.astype(BF16), upper.astype(BF16), preferred_element_type=F32)
    cnt3 = (excl + base_sc[...]).reshape(G, per, tm)
    for k in range(TOP_K):
        hit = e_io == idxs[k]
        rank = red2(jnp.sum, jnp.where(hit, cnt3, 0.0))
        idx_ref[k:k + 1, :] = idxs[k].reshape(1, tm).astype(jnp.int32)
        wt_ref[k:k + 1, :] = (scs[k] / wsum * ROUTED_SCALE).reshape(1, tm)
        rank_ref[k:k + 1, :] = rank.reshape(1, tm).astype(jnp.int32)
    base_sc[...] = base_sc[...] + jnp.sum(sel2, axis=1, keepdims=True)
    cnt_ref[...] = jnp.broadcast_to(base_sc[...], cnt_ref.shape).astype(jnp.int32)


def _router(h, wr_hi, wr_lo, e_bias):
    n, d = h.shape
    tm = _pick(n, (512, 256, 128))
    outk = pl.BlockSpec((TOP_K, tm), lambda i: (0, i))
    return pl.pallas_call(
        functools.partial(_router_kernel, tm=tm),
        grid=(n // tm,),
        in_specs=[pl.BlockSpec((tm, d), lambda i: (i, 0)),
                  pl.BlockSpec((N_EXPERTS, d), lambda i: (0, 0)),
                  pl.BlockSpec((N_EXPERTS, d), lambda i: (0, 0)),
                  pl.BlockSpec((N_EXPERTS, 1), lambda i: (0, 0))],
        out_specs=[outk, outk, outk, pl.BlockSpec((N_EXPERTS, LANES), lambda i: (0, 0))],
        out_shape=[jax.ShapeDtypeStruct((TOP_K, n), jnp.int32),
                   jax.ShapeDtypeStruct((TOP_K, n), F32),
                   jax.ShapeDtypeStruct((TOP_K, n), jnp.int32),
                   jax.ShapeDtypeStruct((N_EXPERTS, LANES), jnp.int32)],
        scratch_shapes=[pltpu.VMEM((N_EXPERTS, 1), F32)],
        compiler_params=_cp(("arbitrary",)),
        name="moe_router",
    )(h, wr_hi, wr_lo, e_bias.reshape(N_EXPERTS, 1))


def _row_copy(src, dst, sem):
    return pltpu.make_async_copy(src, dst, sem)


def _slab_rows(i, ns):
    return pl.ds(pl.multiple_of(i * ns, ns), ns)


def _dispatch_kernel(ps_ref, pc_ref, dest_ref, x_ref, xs_ref, zbuf, sem, *, tm, ns, blk, n_blocks):
    i = pl.program_id(0)

    @pl.when(i == 0)
    def _():
        zbuf[...] = jnp.zeros(zbuf.shape, U32)

        def run(start_slot, rows, wait):
            cp = _row_copy(zbuf.at[pl.ds(0, rows * ns), :],
                           xs_ref.at[pl.ds(pl.multiple_of(start_slot * ns, ns), rows * ns), :], sem.at[1])
            cp.wait() if wait else cp.start()

        def per_expert(e, carry):
            base, cnt = ps_ref[e], pc_ref[e]
            for wait in (False, True):
                for bit in range(blk.bit_length() - 1):
                    size = 1 << bit

                    @pl.when((cnt & size) != 0)
                    def _():
                        run(base + (cnt & (size - 1)), size, wait)
            return carry

        lax.fori_loop(0, N_EXPERTS, per_expert, 0)

        def per_block(j, carry):
            run(j * blk, blk, False)
            run(j * blk, blk, True)
            return carry

        lax.fori_loop(ps_ref[N_EXPERTS], n_blocks, per_block, 0)

    def copy(k, r):
        d = dest_ref[k, r]
        return _row_copy(x_ref.at[pl.ds(r * ns, ns), :], xs_ref.at[_slab_rows(d, ns), :], sem.at[0])

    for r in range(tm):
        for k in range(TOP_K):
            copy(k, r).start(priority=k % 2)

    def drain(r, carry):
        for k in range(TOP_K):
            copy(k, r).wait()
        return carry

    lax.fori_loop(0, tm, drain, 0)


def _dispatch(hs, dest, pad_start, pad_count, n_used, n_blocks):
    ns = hs.shape[0] // dest.shape[1]
    n = dest.shape[1]
    tm = LANES
    blk = EXPERT_ROWS
    assert blk & (blk - 1) == 0
    grid_spec = pltpu.PrefetchScalarGridSpec(
        num_scalar_prefetch=2,
        grid=(n // tm,),
        in_specs=[pl.BlockSpec((TOP_K, tm), lambda i, ps, pc: (0, i), memory_space=pltpu.SMEM),
                  pl.BlockSpec((tm * ns, LANES), lambda i, ps, pc: (i, 0))],
        out_specs=pl.BlockSpec(memory_space=pl.ANY),
        scratch_shapes=[pltpu.VMEM((blk * ns, LANES), U32), pltpu.SemaphoreType.DMA((2,))],
    )
    return pl.pallas_call(
        functools.partial(_dispatch_kernel, tm=tm, ns=ns, blk=blk, n_blocks=n_blocks),
        grid_spec=grid_spec,
        out_shape=jax.ShapeDtypeStruct((n_blocks * blk * ns, LANES), U32),
        compiler_params=_cp(("arbitrary",)),
        name="moe_dispatch",
    )(jnp.concatenate([pad_start, n_used]), pad_count, dest, hs)


def _expert_kernel(be_ref, nu_ref, x_ref, w1_ref, w3_ref, w2_ref, o_ref, w1b, w3b, w2b, *, tm, ns):
    j = pl.program_id(0)
    n_used = nu_ref[0]

    @pl.when(j < n_used)
    def _():
        @pl.when((j == 0) | (be_ref[j] != be_ref[jnp.maximum(j - 1, 0)]))
        def _():
            w1b[...] = w1_ref[...].astype(BF16)
            w3b[...] = w3_ref[...].astype(BF16)
            w2b[...] = w2_ref[...].astype(BF16)

        parts = []
        for s in range(ns):
            lo, hi = _load_slab_pairs(x_ref, (), s, tm, ns)
            parts += [lo.astype(BF16), hi.astype(BF16)]
        x = jnp.concatenate(parts, axis=1)
        a = jnp.dot(x, w1b[...], preferred_element_type=F32)
        b = jnp.dot(x, w3b[...], preferred_element_type=F32)
        hb = (_silu(a) * b).astype(BF16)
        _store_slab(o_ref, jnp.dot(hb, w2b[...], preferred_element_type=F32))

    @pl.when(j >= n_used)
    def _():
        o_ref[...] = jnp.zeros(o_ref.shape, U32)


def _experts(xs, blk_e, n_used, w1, w3, w2, layer):
    d, f = w1.shape[2], w1.shape[3]
    ns = d // HEAD_SLAB
    nb = blk_e.shape[0]
    tm = EXPERT_ROWS
    grid_spec = pltpu.PrefetchScalarGridSpec(
        num_scalar_prefetch=2,
        grid=(nb,),
        in_specs=[pl.BlockSpec((tm * ns, LANES), lambda j, be, nu: (jnp.minimum(j, nu[0] - 1), 0)),
                  pl.BlockSpec((None, None, d, f), lambda j, be, nu: (layer, be[j], 0, 0)),
                  pl.BlockSpec((None, None, d, f), lambda j, be, nu: (layer, be[j], 0, 0)),
                  pl.BlockSpec((None, None, f, d), lambda j, be, nu: (layer, be[j], 0, 0))],
        out_specs=pl.BlockSpec((tm * ns, LANES), lambda j, be, nu: (j, 0)),
        scratch_shapes=[pltpu.VMEM((d, f), BF16), pltpu.VMEM((d, f), BF16), pltpu.VMEM((f, d), BF16)],
    )
    return pl.pallas_call(
        functools.partial(_expert_kernel, tm=tm, ns=ns),
        grid_spec=grid_spec,
        out_shape=jax.ShapeDtypeStruct((nb * tm * ns, LANES), U32),
        compiler_params=_cp(("arbitrary",)),
        name="moe_experts",
    )(blk_e, n_used, xs, w1, w3, w2)


def _combine_kernel(dest_ref, dnxt_ref, h_ref, wt_ref, ws1_ref, ws3_ref, ws2_ref, g_ref, b_ref, ys_ref,
                    of_ref, *rest, tm, ns, nsteps):
    ob_ref = rest[0] if len(rest) == 3 else None
    gbuf, sem = rest[-2:]
    i = pl.program_id(0)
    slot = lax.rem(i, 2)

    def copy(d_ref, k, r, dst_slot):
        d = d_ref[k, r]
        return _row_copy(ys_ref.at[_slab_rows(d, ns), :], gbuf.at[dst_slot, k, _slab_rows(r, ns), :],
                         sem.at[dst_slot])

    def loop(d_ref, dst_slot, wait):
        def body(r, carry):
            for k in range(TOP_K):
                cp = copy(d_ref, k, r, dst_slot)
                if wait:
                    cp.wait()
                else:
                    cp.start(priority=k % 2)
            return carry

        lax.fori_loop(0, tm, body, 0)

    @pl.when(i == 0)
    def _():
        loop(dest_ref, slot, False)

    loop(dest_ref, slot, True)
    for r in range(tm):
        for k in range(TOP_K):
            copy(dnxt_ref, k, r, 1 - slot).start(priority=k % 2)
    h = h_ref[...]
    xb = h.astype(BF16)
    a = jnp.dot(xb, ws1_ref[...], preferred_element_type=F32)
    b = jnp.dot(xb, ws3_ref[...], preferred_element_type=F32)
    shared = jnp.dot((_silu(a) * b).astype(BF16), ws2_ref[...], preferred_element_type=F32)
    wt = wt_ref[...]
    parts = []
    for s in range(ns):
        lo = hi = None
        for k in range(TOP_K):
            plo, phi = _load_slab_pairs(gbuf, (slot, k), s, tm, ns)
            w = wt[:, k:k + 1]
            lo = w * plo if lo is None else lo + w * plo
            hi = w * phi if hi is None else hi + w * phi
        parts += [lo, hi]
    acc = shared + jnp.concatenate(parts, axis=1)
    r = _ln_rows(ALPHA * h + acc, g_ref[...], b_ref[...])
    of_ref[...] = r
    if ob_ref is not None:
        ob_ref[...] = r.astype(BF16)

    @pl.when(i == nsteps - 1)
    def _():
        loop(dnxt_ref, 1 - slot, True)


def _combine(h, dest, wt_tok, ws1, ws3, ws2, g, b, ys, final_shape=None):
    n, d = h.shape
    f = ws1.shape[1]
    tm = LANES
    row = pl.BlockSpec((tm, d), lambda i: (i, 0))
    vec = pl.BlockSpec((1, d), lambda i: (0, 0))
    if final_shape is None:
        out_specs = [row, row]
        out_shape = [jax.ShapeDtypeStruct((n, d), F32), jax.ShapeDtypeStruct((n, d), BF16)]
    else:
        batch, seq = final_shape
        nt = seq // tm + 1
        out_specs = [pl.BlockSpec((None, tm, d), lambda i: (i // nt, jnp.maximum(i % nt - 1, 0), 0))]
        out_shape = [jax.ShapeDtypeStruct((batch, seq, d), F32)]
    ns = d // HEAD_SLAB
    nsteps = n // tm
    return pl.pallas_call(
        functools.partial(_combine_kernel, tm=tm, ns=ns, nsteps=nsteps),
        grid=(nsteps,),
        in_specs=[pl.BlockSpec((TOP_K, tm), lambda i: (0, i), memory_space=pltpu.SMEM),
                  pl.BlockSpec((TOP_K, tm), lambda i: (0, jnp.minimum(i + 1, nsteps - 1)), memory_space=pltpu.SMEM),
                  row,
                  pl.BlockSpec((tm, TOP_K), lambda i: (i, 0)),
                  pl.BlockSpec((d, f), lambda i: (0, 0)),
                  pl.BlockSpec((d, f), lambda i: (0, 0)),
                  pl.BlockSpec((f, d), lambda i: (0, 0)),
                  vec, vec,
                  pl.BlockSpec(memory_space=pl.ANY)],
        out_specs=out_specs,
        out_shape=out_shape,
        scratch_shapes=[pltpu.VMEM((2, TOP_K, tm * ns, LANES), U32), pltpu.SemaphoreType.DMA((2,))],
        compiler_params=_cp(("arbitrary",)),
        name="moe_combine",
    )(dest, dest, h, wt_tok, ws1, ws3, ws2, g.reshape(1, d), b.reshape(1, d), ys)


def _moe(h_f32, hs, w_router, e_bias, w1, w3, w2, layer, ws1, ws3, ws2, g, b, final_shape=None):
    n = h_f32.shape[0]
    wr_t = w_router.T
    wr_hi = wr_t.astype(BF16)
    wr_lo = (wr_t - wr_hi.astype(F32)).astype(BF16)
    idx, wts, rank, cnt = _router(h_f32, wr_hi, wr_lo, e_bias)
    counts = cnt[:, 0]
    blk = EXPERT_ROWS
    pcounts = (counts + blk - 1) // blk * blk
    pends = jnp.cumsum(pcounts)
    pstarts = pends - pcounts
    e_ids = jnp.arange(N_EXPERTS, dtype=jnp.int32)
    start_of = jnp.sum(jnp.where(idx[None] == e_ids[:, None, None], pstarts[:, None, None], 0), axis=0)
    dest = (start_of + rank).astype(jnp.int32)
    nb = n * TOP_K // blk + N_EXPERTS
    blk_e = jnp.minimum(jnp.sum(pends[None, :] <= (jnp.arange(nb, dtype=jnp.int32) * blk)[:, None], axis=1),
                        N_EXPERTS - 1).astype(jnp.int32)
    n_used = (pends[-1:] // blk).astype(jnp.int32)
    xs = _dispatch(hs, dest, (pstarts + counts).astype(jnp.int32), (pcounts - counts).astype(jnp.int32), n_used, nb)
    ys = _experts(xs, blk_e, n_used, w1, w3, w2, layer)
    return _combine(h_f32, dest, wts.T, ws1.astype(BF16), ws3.astype(BF16), ws2.astype(BF16), g, b, ys,
                    final_shape)


def _in_proj_layout(w_in, b_in, d, ql, kvl):
    hq, hv = ML_HEADS * ML_QK, ML_HEADS * ML_V
    sizes = (ql, kvl, QK_ROPE, hq, hq, hv, hv, ML_HEADS, ML_HEADS, d, d)
    offs = np.concatenate([[0], np.cumsum(sizes)])
    seg = lambda a, i: a[..., int(offs[i]):int(offs[i + 1])]
    order = (0, 1, 3, 4, 5, 6, 9, 10)
    w_main = jnp.concatenate([seg(w_in, i) for i in order], axis=-1)
    b_main = jnp.concatenate([seg(b_in, i) for i in order], axis=-1)
    half = QK_ROPE // 2

    def small(a):
        kr = seg(a, 2)
        x1, x2 = kr[..., :half], kr[..., half:]
        pad = jnp.zeros(a.shape[:-1] + (LANES - 2 * ML_HEADS,), a.dtype)
        return jnp.concatenate([x1, x2, x2, x1, seg(a, 7), seg(a, 8), pad], axis=-1)

    main_offs = np.concatenate([[0], np.cumsum([sizes[i] for i in order])])
    return w_main, b_main, small(w_in), small(b_in), [int(o) for o in main_offs]


def _uq_layout(w_uq):
    kq = w_uq.shape[0]
    w = w_uq.reshape(kq, MLA_HEADS, QK_NOPE + QK_ROPE)
    half = QK_ROPE // 2
    nope, x1, x2 = w[..., :QK_NOPE], w[..., QK_NOPE:QK_NOPE + half], w[..., QK_NOPE + half:]
    return jnp.concatenate([nope, x1, x2, x2, x1], axis=-1).reshape(kq, MLA_HEADS * HEAD_SLAB)


def _ukv_layout(w_ukv):
    kk = w_ukv.shape[0]
    w = w_ukv.reshape(kk, MLA_HEADS, QK_NOPE + V_HEAD)
    return jnp.concatenate([w[..., :QK_NOPE].reshape(kk, -1), w[..., QK_NOPE:].reshape(kk, -1)], axis=-1)


def _rope_tables(tp):
    half = QK_ROPE // 2
    pos = jnp.arange(tp, dtype=F32) - PADL
    inv_freq = 1.0 / (ROPE_THETA ** (jnp.arange(0, QK_ROPE, 2, dtype=F32) / QK_ROPE))
    ang = pos[:, None] * inv_freq[None, :]
    cos, sin = jnp.cos(ang), jnp.sin(ang)
    zero = jnp.zeros((tp, LANES - 2 * half), F32)
    return jnp.concatenate([cos, cos, zero], axis=1), jnp.concatenate([-sin, sin, zero], axis=1)


def kernel(x, meta, ln_in_g, ln_in_b, w_in, b_in, q_norm_g, kv_norm_g, w_uq, w_ukv, conv_w, conv_b, ml_norm_g, w_br_mla, w_br_mlstm, w_out, ln1_g, ln1_b, w_router, e_bias, w1, w3, w2, ws1, ws3, ws2, ln2_g, ln2_b):
    batch, seq, d = x.shape
    depth = w_in.shape[0]
    ql, kvl = q_norm_g.shape[1], kv_norm_g.shape[1]
    assert ql == kvl and seq % LANES == 0
    tp = LANES + seq
    n = batch * tp
    hq = ML_HEADS * ML_QK

    head = jnp.concatenate([jnp.zeros((PADL, d), x.dtype), meta.astype(x.dtype)], axis=0)
    h_f32, h_bf = _layer_norm_in(x, head, ln_in_g, ln_in_b)
    cos_t, sin_t = _rope_tables(tp)
    col_scale = jnp.concatenate([jnp.ones((1, hq), F32), jnp.full((1, hq), ML_QK ** -0.5, F32)], axis=1)

    for l in range(depth):
        w_main, b_main, w_small, b_small, offs = _in_proj_layout(w_in[l], b_in[l], d, ql, kvl)
        o_mq, o_mv, o_mo, o_ga, o_gb = offs[2], offs[4], offs[5], offs[6], offs[7]
        z = _matmul_bias(h_bf, w_main.astype(BF16), b_main, BF16, sig_col=o_mo, name="in_proj")
        zs = _matmul_bias(h_bf, w_small.astype(BF16), b_small, F32, name="in_proj_small")

        q, k, v = _mla_qkv(z, zs, q_norm_g[l], kv_norm_g[l], _uq_layout(w_uq[l]).astype(BF16),
                           _ukv_layout(w_ukv[l]).astype(BF16), cos_t, sin_t, tp, ql)
        attn = _attention(q, k, v, batch, tp)

        assert o_mq % hq == 0 and o_mv % ML_V == 0 and o_mo % ML_V == 0
        qk = _qk_conv(z, conv_w[l], conv_b[l], col_scale, tp, o_mq // hq)
        hm = _mlstm(qk, z, zs, ml_norm_g[l], batch, tp, o_mv // ML_V, o_mo // ML_V)

        y = _merge(attn, hm, w_br_mla[l].astype(BF16), w_br_mlstm[l].astype(BF16), z, o_ga, o_gb)
        h_f32, hs = _outproj_ln(y, w_out[l].astype(BF16), h_f32, ln1_g[l], ln1_b[l])
        outs = _moe(h_f32, hs, w_router[l], e_bias[l], w1, w3, w2, l, ws1[l], ws3[l], ws2[l],
                    ln2_g[l], ln2_b[l], final_shape=(batch, seq) if l == depth - 1 else None)
        if l == depth - 1:
            return outs[0]
        h_f32, h_bf = outs
```

```python
import functools

import numpy as np
import jax
import jax.numpy as jnp
from jax import lax
from jax.experimental import pallas as pl
from jax.experimental.pallas import tpu as pltpu

N_META = 16
MLA_HEADS = 8
QK_NOPE = 128
QK_ROPE = 64
V_HEAD = 128
ROPE_THETA = 10000.0
ML_HEADS = 8
ML_QK = 128
ML_V = 256
CONV_K = 4
CHUNK = 64
N_EXPERTS = 64
TOP_K = 8
N_GROUPS = 8
TOPK_GROUPS = 4
ROUTED_SCALE = 2.5
DEPTH = 2
ALPHA = (2 * DEPTH) ** 0.25
LN_EPS = 1e-5
RMS_EPS = 1e-6
NEG = -1e30

LANES = 128
HEAD_SLAB = 2 * LANES
PADL = LANES - N_META
EXPERT_ROWS = 1024
VMEM_LIMIT = 56 * 1024 * 1024

F32 = jnp.float32
BF16 = jnp.bfloat16
U32 = jnp.uint32
NT = (((1,), (1,)), ((), ()))


def _pick(n, cands):
    for c in cands:
        if n % c == 0:
            return c
    raise ValueError(f"no tile in {cands} divides {n}")


def _cp(sem):
    return pltpu.CompilerParams(dimension_semantics=sem, vmem_limit_bytes=VMEM_LIMIT)


def _ln_rows(x, g, b):
    mu = jnp.mean(x, axis=-1, keepdims=True)
    xc = x - mu
    var = jnp.mean(xc * xc, axis=-1, keepdims=True)
    return xc * lax.rsqrt(var + LN_EPS) * g + b


def _rms_rows(x, g):
    return x * lax.rsqrt(jnp.mean(x * x, axis=-1, keepdims=True) + RMS_EPS) * g


def _silu(x):
    return x * jax.nn.sigmoid(x)


def _store_slab(ref, x):
    rows, ns = x.shape[0], x.shape[1] // HEAD_SLAB
    for s in range(ns):
        lo = lax.bitcast_convert_type(x[:, s * HEAD_SLAB:s * HEAD_SLAB + LANES].astype(BF16).astype(F32), U32)
        hi = lax.bitcast_convert_type(x[:, s * HEAD_SLAB + LANES:(s + 1) * HEAD_SLAB].astype(BF16).astype(F32), U32)
        ref[pl.ds(s, rows, stride=ns), :] = (lo >> 16) | hi


def _load_slab_pairs(ref, lead, s, rows, ns):
    u = ref[lead + (pl.ds(s, rows, stride=ns), slice(None))]
    return (lax.bitcast_convert_type(u << 16, F32),
            lax.bitcast_convert_type(u & jnp.uint32(0xFFFF0000), F32))


def _ln_in_kernel(head_ref, x_ref, g_ref, b_ref, of_ref, ob_ref):
    i = pl.program_id(1)

    def emit(src):
        y = _ln_rows(src, g_ref[...], b_ref[...])
        of_ref[...] = y
        ob_ref[...] = y.astype(BF16)

    @pl.when(i == 0)
    def _():
        emit(head_ref[...])

    @pl.when(i > 0)
    def _():
        emit(x_ref[...])


def _layer_norm_in(x, head, g, b):
    batch, seq, d = x.shape
    nt = seq // LANES + 1
    row = pl.BlockSpec((LANES, d), lambda bb, i: (bb * nt + i, 0))
    vec = pl.BlockSpec((1, d), lambda bb, i: (0, 0))
    n = batch * nt * LANES
    return pl.pallas_call(
        _ln_in_kernel,
        grid=(batch, nt),
        in_specs=[pl.BlockSpec((LANES, d), lambda bb, i: (0, 0)),
                  pl.BlockSpec((None, LANES, d), lambda bb, i: (bb, jnp.maximum(i - 1, 0), 0)),
                  vec, vec],
        out_specs=[row, row],
        out_shape=[jax.ShapeDtypeStruct((n, d), F32), jax.ShapeDtypeStruct((n, d), BF16)],
        compiler_params=_cp(("parallel", "arbitrary")),
        name="ln_in",
    )(head, x, g.reshape(1, d), b.reshape(1, d))


def _mm_kernel(x_ref, w_ref, b_ref, o_ref, *, sig_tile):
    acc = jnp.dot(x_ref[...], w_ref[...], preferred_element_type=F32) + b_ref[...]
    if sig_tile is None:
        o_ref[...] = acc.astype(o_ref.dtype)
        return
    j = pl.program_id(0)

    @pl.when(j < sig_tile)
    def _():
        o_ref[...] = acc.astype(o_ref.dtype)

    @pl.when(j >= sig_tile)
    def _():
        o_ref[...] = jax.nn.sigmoid(acc).astype(o_ref.dtype)


def _matmul_bias(x, w, b, out_dtype, sig_col=None, name="mm"):
    n, kd = x.shape
    nc = w.shape[1]
    tm = _pick(n, (1024, 512, 256, 128))
    tn = _pick(nc, (1024, 512, 256, 128))
    sig_tile = None
    if sig_col is not None:
        assert sig_col % tn == 0
        sig_tile = sig_col // tn
    return pl.pallas_call(
        functools.partial(_mm_kernel, sig_tile=sig_tile),
        grid=(nc // tn, n // tm),
        in_specs=[pl.BlockSpec((tm, kd), lambda j, i: (i, 0)),
                  pl.BlockSpec((kd, tn), lambda j, i: (0, j)),
                  pl.BlockSpec((1, tn), lambda j, i: (0, j))],
        out_specs=pl.BlockSpec((tm, tn), lambda j, i: (i, j)),
        out_shape=jax.ShapeDtypeStruct((n, nc), out_dtype),
        compiler_params=_cp(("parallel", "parallel")),
        name=name,
    )(x, w, b.reshape(1, nc))


def _rope_slab(r, cos_t, sin_t):
    return r * cos_t + pltpu.roll(r, 2 * (QK_ROPE // 2), 1) * sin_t


def _qproj_kernel(c_ref, g_ref, w_ref, cos_ref, sin_ref, o_ref, *, heads, scale):
    xn = _rms_rows(c_ref[...].astype(F32), g_ref[...])
    q = jnp.dot(xn.astype(BF16), w_ref[...], preferred_element_type=F32) * scale
    cos_t, sin_t = cos_ref[...], sin_ref[...]
    for h in range(heads):
        lo = h * HEAD_SLAB
        o_ref[:, lo:lo + LANES] = q[:, lo:lo + LANES].astype(BF16)
        o_ref[:, lo + LANES:lo + HEAD_SLAB] = _rope_slab(q[:, lo + LANES:lo + HEAD_SLAB], cos_t, sin_t).astype(BF16)


def _kproj_kernel(c_ref, g_ref, w_ref, kr_ref, cos_ref, sin_ref, k_ref, v_ref, *, heads):
    xn = _rms_rows(c_ref[...].astype(F32), g_ref[...])
    kv = jnp.dot(xn.astype(BF16), w_ref[...], preferred_element_type=F32)
    rr_t = _rope_slab(kr_ref[...], cos_ref[...], sin_ref[...]).T.astype(BF16)
    for h in range(heads):
        lo = h * HEAD_SLAB
        k_ref[lo:lo + LANES, :] = kv[:, h * LANES:(h + 1) * LANES].T.astype(BF16)
        k_ref[lo + LANES:lo + HEAD_SLAB, :] = rr_t
    v_ref[...] = kv[:, heads * LANES:].astype(BF16)


def _mla_qkv(z, zs, qg, kvg, wq, wkv, cos_t, sin_t, tp, ql):
    n = z.shape[0]
    heads = MLA_HEADS
    tm = _pick(tp, (384, 256, 128))
    nt = tp // tm
    scale = (QK_NOPE + QK_ROPE) ** -0.5 * float(np.log2(np.e))
    tab = pl.BlockSpec((tm, LANES), lambda i: (i % nt, 0))
    q = pl.pallas_call(
        functools.partial(_qproj_kernel, heads=heads, scale=scale),
        grid=(n // tm,),
        in_specs=[pl.BlockSpec((tm, ql), lambda i: (i, 0)),
                  pl.BlockSpec((1, ql), lambda i: (0, 0)),
                  pl.BlockSpec((ql, heads * HEAD_SLAB), lambda i: (0, 0)),
                  tab, tab],
        out_specs=pl.BlockSpec((tm, heads * HEAD_SLAB), lambda i: (i, 0)),
        out_shape=jax.ShapeDtypeStruct((n, heads * HEAD_SLAB), BF16),
        compiler_params=_cp(("parallel",)),
        name="mla_qproj",
    )(z, qg.reshape(1, ql), wq, cos_t, sin_t)
    k, v = pl.pallas_call(
        functools.partial(_kproj_kernel, heads=heads),
        grid=(n // tm,),
        in_specs=[pl.BlockSpec((tm, ql), lambda i: (i, 1)),
                  pl.BlockSpec((1, ql), lambda i: (0, 0)),
                  pl.BlockSpec((ql, 2 * heads * LANES), lambda i: (0, 0)),
                  pl.BlockSpec((tm, LANES), lambda i: (i, 0)),
                  tab, tab],
        out_specs=[pl.BlockSpec((heads * HEAD_SLAB, tm), lambda i: (0, i)),
                   pl.BlockSpec((tm, heads * LANES), lambda i: (i, 0))],
        out_shape=[jax.ShapeDtypeStruct((heads * HEAD_SLAB, n), BF16),
                   jax.ShapeDtypeStruct((n, heads * LANES), BF16)],
        compiler_params=_cp(("parallel",)),
        name="mla_kvproj",
    )(z, kvg.reshape(1, ql), wkv, zs, cos_t, sin_t)
    return q, k, v


ATTN_HEADS_PER_STEP = 2


def _attn_kernel(q_ref, k_ref, v_ref, o_ref, s_sc, m_sc, l_sc, acc_sc, *, t, hps):
    i = pl.program_id(2)
    nl = t // LANES

    def lane_fold(x, fn):
        r = x[:, :LANES]
        for c in range(1, nl):
            r = fn(r, x[:, c * LANES:(c + 1) * LANES])
        return r

    m_sc[...] = jnp.full(m_sc.shape, NEG, F32)

    def scores(j, masked):
        rows = pl.ds(pl.multiple_of(j * t, t), t)
        if masked:
            qpos = i * t + lax.broadcasted_iota(jnp.int32, (t, t), 0)
            kpos = j * t + lax.broadcasted_iota(jnp.int32, (t, t), 1)
            keep = (kpos <= qpos) & (kpos >= PADL)
        for h in range(hps):
            q = q_ref[:, h * HEAD_SLAB:(h + 1) * HEAD_SLAB]
            s = jnp.dot(q, k_ref[h * HEAD_SLAB:(h + 1) * HEAD_SLAB, rows], preferred_element_type=F32)
            if masked:
                s = jnp.where(keep, s, NEG)
            s_sc[h, j] = s
            m_sc[h] = jnp.maximum(m_sc[h], lane_fold(s, jnp.maximum))

    scores(0, True)

    def score_body(j, carry):
        scores(j, False)
        return carry

    lax.fori_loop(1, i, score_body, 0)

    @pl.when(i > 0)
    def _():
        scores(i, True)

    for h in range(hps):
        m_sc[h] = jnp.broadcast_to(jnp.max(m_sc[h], axis=1, keepdims=True), (t, LANES))
    l_sc[...] = jnp.zeros(l_sc.shape, F32)
    acc_sc[...] = jnp.zeros(acc_sc.shape, F32)

    def pv_body(j, carry):
        rows = pl.ds(pl.multiple_of(j * t, t), t)
        for h in range(hps):
            s = s_sc[h, j]
            mb = m_sc[h]
            p = jnp.concatenate([jnp.exp2(s[:, c * LANES:(c + 1) * LANES] - mb) for c in range(nl)], axis=1)
            l_sc[h] = l_sc[h] + lane_fold(p, jnp.add)
            acc_sc[h] = acc_sc[h] + jnp.dot(p.astype(BF16), v_ref[rows, h * V_HEAD:(h + 1) * V_HEAD],
                                            preferred_element_type=F32)
        return carry

    lax.fori_loop(0, i + 1, pv_body, 0)

    for h in range(hps):
        l = jnp.sum(l_sc[h], axis=1, keepdims=True)
        o_ref[:, h * V_HEAD:(h + 1) * V_HEAD] = (acc_sc[h] / l).astype(o_ref.dtype)


def _attention(q, k, v, batch, tp):
    n = q.shape[0]
    heads = MLA_HEADS
    hps = ATTN_HEADS_PER_STEP
    assert heads % hps == 0
    t = _pick(tp, (384, 256, 128))
    nt = tp // t
    return pl.pallas_call(
        functools.partial(_attn_kernel, t=t, hps=hps),
        grid=(batch, heads // hps, nt),
        in_specs=[pl.BlockSpec((t, hps * HEAD_SLAB), lambda b, h, i: (b * nt + i, h)),
                  pl.BlockSpec((hps * HEAD_SLAB, tp), lambda b, h, i: (h, b)),
                  pl.BlockSpec((tp, hps * V_HEAD), lambda b, h, i: (b, h))],
        out_specs=pl.BlockSpec((t, hps * V_HEAD), lambda b, h, i: (b * nt + i, h)),
        out_shape=jax.ShapeDtypeStruct((n, heads * V_HEAD), BF16),
        scratch_shapes=[pltpu.VMEM((hps, nt, t, t), F32), pltpu.VMEM((hps, t, LANES), F32),
                        pltpu.VMEM((hps, t, LANES), F32), pltpu.VMEM((hps, t, V_HEAD), F32)],
        compiler_params=_cp(("parallel", "parallel", "parallel")),
        name="mla_attention",
    )(q, k, v)


HALO = 16


def _conv_kernel(x_ref, halo_ref, w_ref, b_ref, s_ref, o_ref, *, tm, tp):
    i = pl.program_id(0)
    start = lax.rem(i * tm, tp)
    pos = start + lax.broadcasted_iota(jnp.int32, (tm, 1), 0)
    x = jnp.where(pos >= PADL, x_ref[...].astype(F32), 0.0)
    hpos = start - HALO + lax.broadcasted_iota(jnp.int32, (HALO, 1), 0)
    halo = jnp.where(hpos >= PADL, halo_ref[...].astype(F32), 0.0)
    ext = jnp.concatenate([halo, x], axis=0)
    w = w_ref[...]
    y = b_ref[...] + w[CONV_K - 1:CONV_K, :] * x
    for j in range(1, CONV_K):
        y = y + w[CONV_K - 1 - j:CONV_K - j, :] * pltpu.roll(ext, j, 0)[HALO:, :]
    o_ref[...] = (_silu(y) * s_ref[...]).astype(BF16)


def _qk_conv(z, conv_w, conv_b, col_scale, tp, off_blk):
    n = z.shape[0]
    cw = conv_w.shape[1] // 2
    tm = _pick(tp, (384, 256, 128))
    return pl.pallas_call(
        functools.partial(_conv_kernel, tm=tm, tp=tp),
        grid=(n // tm, 2),
        in_specs=[pl.BlockSpec((tm, cw), lambda i, c: (i, off_blk + c)),
                  pl.BlockSpec((HALO, cw), lambda i, c: (jnp.maximum(i * (tm // HALO) - 1, 0), off_blk + c)),
                  pl.BlockSpec((CONV_K, cw), lambda i, c: (0, c)),
                  pl.BlockSpec((1, cw), lambda i, c: (0, c)),
                  pl.BlockSpec((1, cw), lambda i, c: (0, c))],
        out_specs=pl.BlockSpec((tm, cw), lambda i, c: (i, c)),
        out_shape=jax.ShapeDtypeStruct((n, 2 * cw), BF16),
        compiler_params=_cp(("parallel", "parallel")),
        name="mlstm_qk_conv",
    )(z, z, conv_w, conv_b.reshape(1, -1), col_scale)


MLSTM_HEADS_PER_STEP = 2


def _mlstm_kernel(*refs, group, heads, hps):
    q_refs, k_refs, v_refs = refs[:hps], refs[hps:2 * hps], refs[2 * hps:3 * hps]
    gt_ref = refs[3 * hps]
    og_refs = refs[3 * hps + 1:4 * hps + 1]
    ng_ref, o_ref, c_sc, n_sc, m_sc = refs[4 * hps + 1:]
    hp = pl.program_id(1)
    g = pl.program_id(2)
    L = CHUNK

    @pl.when(g == 0)
    def _():
        c_sc[...] = jnp.zeros(c_sc.shape, F32)
        n_sc[...] = jnp.zeros(n_sc.shape, F32)
        m_sc[...] = jnp.zeros(m_sc.shape, F32)

    R = group * L
    lane = lax.broadcasted_iota(jnp.int32, (R, LANES), 1)
    r_i = lax.broadcasted_iota(jnp.int32, (L, L), 0)
    c_i = lax.broadcasted_iota(jnp.int32, (L, L), 1)
    eye = r_i == c_i
    causal = c_i <= r_i

    gt = gt_ref[...]
    pad = (g * R + lax.broadcasted_iota(jnp.int32, (R, 1), 0)) < PADL
    i_parts, lf_parts = [], []
    for hh in range(hps):
        h = hp * hps + hh
        i_all = jnp.sum(jnp.where(lane == h, gt, 0.0), axis=1, keepdims=True)
        f_all = jnp.sum(jnp.where(lane == heads + h, gt, 0.0), axis=1, keepdims=True)
        i_parts.append(jnp.where(pad, NEG, i_all).reshape(group, L, 1))
        lf_parts.append(jnp.where(pad, 0.0, -(jnp.maximum(-f_all, 0.0) + jnp.log1p(jnp.exp(-jnp.abs(f_all)))))
                        .reshape(group, L, 1))
    i_col = jnp.concatenate(i_parts, axis=0)
    lf_col = jnp.concatenate(lf_parts, axis=0)
    b_row = jnp.sum(jnp.where((r_i <= c_i)[None], lf_col, 0.0), axis=1, keepdims=True)
    b_col = jnp.sum(jnp.where(eye[None], b_row, 0.0), axis=2, keepdims=True)
    i_row = jnp.sum(jnp.where(eye[None], i_col, 0.0), axis=1, keepdims=True)
    gtot = jnp.sum(lf_col, axis=1, keepdims=True)
    dmat = jnp.where(causal[None], b_col - b_row + i_row, NEG)
    rmax = jnp.max(dmat, axis=2, keepdims=True)
    a_col = gtot - b_col + i_col
    amax = jnp.max(a_col, axis=1, keepdims=True)

    m_prev_parts, m_new_parts = [], []
    for hh in range(hps):
        m = m_sc[hh].reshape(1, 1, 1)
        for c in range(group):
            idx = hh * group + c
            m_prev_parts.append(m)
            m = jnp.maximum(gtot[idx:idx + 1] + m, amax[idx:idx + 1])
            m_new_parts.append(m)
        m_sc[hh] = m.reshape(1, 1)
    m_prev = jnp.concatenate(m_prev_parts, axis=0)
    m_new = jnp.concatenate(m_new_parts, axis=0)
    inter = b_col + m_prev
    mrow = jnp.maximum(rmax, inter)
    dexp = jnp.exp(dmat - mrow)
    e_in = jnp.exp(inter - mrow)
    floor = jnp.exp(-mrow)
    ea = jnp.exp(a_col - m_new)
    decay = jnp.exp(gtot + m_prev - m_new)

    def stack(refs):
        return jnp.concatenate([r[...].reshape(group, L, r.shape[1]) for r in refs], axis=0)

    q3, k3, v3 = stack(q_refs), stack(k_refs), stack(v_refs)
    s3 = jnp.einsum("bqd,bkd->bqk", q3, k3, preferred_element_type=F32) * dexp
    intra = jnp.einsum("bqk,bkv->bqv", s3.astype(BF16), v3, preferred_element_type=F32)
    rowsum = jnp.sum(s3, axis=2, keepdims=True)
    ks3 = k3.astype(F32) * ea
    kv3 = jnp.einsum("bdk,bkv->bdv", jnp.swapaxes(ks3, 1, 2).astype(BF16), v3, preferred_element_type=F32)
    ksum = jnp.sum(ks3, axis=1, keepdims=True)

    c_parts, n_parts = [], []
    for hh in range(hps):
        cmat, nvec = c_sc[hh], n_sc[hh]
        for c in range(group):
            idx = hh * group + c
            c_parts.append(cmat[None])
            n_parts.append(nvec[None])
            cmat = decay[idx] * cmat + kv3[idx]
            nvec = decay[idx] * nvec + ksum[idx]
        c_sc[hh], n_sc[hh] = cmat, nvec
    c_prev = jnp.concatenate(c_parts, axis=0)
    n_prev = jnp.concatenate(n_parts, axis=0)
    num = intra + e_in * jnp.einsum("bqd,bdv->bqv", q3, c_prev.astype(BF16), preferred_element_type=F32)
    den = rowsum + e_in * jnp.sum(q3.astype(F32) * n_prev, axis=2, keepdims=True)
    hc = num / jnp.maximum(jnp.abs(den), floor)
    hc = hc * lax.rsqrt(jnp.mean(hc * hc, axis=2, keepdims=True) + RMS_EPS)
    for hh in range(hps):
        cols = slice(hh * ML_V, (hh + 1) * ML_V)
        hn = hc[hh * group:(hh + 1) * group].reshape(R, ML_V) * ng_ref[:, cols]
        o_ref[:, cols] = (hn * og_refs[hh][...].astype(F32)).astype(BF16)


def _mlstm(qk, z, zs, norm_g, batch, tp, v_blk, og_blk):
    n = qk.shape[0]
    heads = ML_HEADS
    hps = MLSTM_HEADS_PER_STEP
    assert heads % hps == 0
    nchunks = tp // CHUNK
    group = _pick(nchunks, (11, 6, 5, 4, 3, 2, 1))
    rows = group * CHUNK
    ng = nchunks // group

    def col(width, base):
        return [pl.BlockSpec((rows, width), lambda b, p, g, u=u: (b * ng + g, base + p * hps + u)) for u in range(hps)]

    return pl.pallas_call(
        functools.partial(_mlstm_kernel, group=group, heads=heads, hps=hps),
        grid=(batch, heads // hps, ng),
        in_specs=(col(ML_QK, 0) + col(ML_QK, heads) + col(ML_V, v_blk)
                  + [pl.BlockSpec((rows, LANES), lambda b, p, g: (b * ng + g, 1))]
                  + col(ML_V, og_blk)
                  + [pl.BlockSpec((1, hps * ML_V), lambda b, p, g: (0, p))]),
        out_specs=pl.BlockSpec((rows, hps * ML_V), lambda b, p, g: (b * ng + g, p)),
        out_shape=jax.ShapeDtypeStruct((n, heads * ML_V), BF16),
        scratch_shapes=[pltpu.VMEM((hps, ML_QK, ML_V), F32), pltpu.VMEM((hps, 1, ML_QK), F32),
                        pltpu.VMEM((hps, 1, 1), F32)],
        compiler_params=_cp(("parallel", "parallel", "arbitrary")),
        name="mlstm_scan",
    )(*([qk] * (2 * hps) + [z] * hps + [zs] + [z] * hps + [norm_g.reshape(1, -1)]))


def _merge_kernel(a_ref, hm_ref, wa_ref, wb_ref, ga_ref, gb_ref, o_ref):
    ya = jnp.dot(a_ref[...], wa_ref[...], preferred_element_type=F32)
    yb = jnp.dot(hm_ref[...], wb_ref[...], preferred_element_type=F32)
    o_ref[...] = (ga_ref[...].astype(F32) * ya + gb_ref[...].astype(F32) * yb).astype(BF16)


def _merge(attn, hm, wa, wb, z, ga_off, gb_off):
    n = attn.shape[0]
    d = wa.shape[1]
    tm = _pick(n, (512, 256, 128))
    tn = _pick(d, (1024, 512, 256, 128))
    assert ga_off % tn == 0 and gb_off % tn == 0
    ga_blk, gb_blk = ga_off // tn, gb_off // tn
    return pl.pallas_call(
        _merge_kernel,
        grid=(d // tn, n // tm),
        in_specs=[pl.BlockSpec((tm, attn.shape[1]), lambda j, i: (i, 0)),
                  pl.BlockSpec((tm, hm.shape[1]), lambda j, i: (i, 0)),
                  pl.BlockSpec((wa.shape[0], tn), lambda j, i: (0, j)),
                  pl.BlockSpec((wb.shape[0], tn), lambda j, i: (0, j)),
                  pl.BlockSpec((tm, tn), lambda j, i: (i, ga_blk + j)),
                  pl.BlockSpec((tm, tn), lambda j, i: (i, gb_blk + j))],
        out_specs=pl.BlockSpec((tm, tn), lambda j, i: (i, j)),
        out_shape=jax.ShapeDtypeStruct((n, d), BF16),
        compiler_params=_cp(("parallel", "parallel")),
        name="branch_merge",
    )(attn, hm, wa, wb, z, z)


def _outln_kernel(y_ref, w_ref, h_ref, g_ref, b_ref, of_ref, os_ref):
    o = jnp.dot(y_ref[...], w_ref[...], preferred_element_type=F32)
    r = _ln_rows(ALPHA * h_ref[...] + o, g_ref[...], b_ref[...])
    of_ref[...] = r
    _store_slab(os_ref, r)


def _outproj_ln(y, w, h, g, b):
    n, d = h.shape
    tm = _pick(n, (256, 128))
    ns = d // HEAD_SLAB
    row = pl.BlockSpec((tm, d), lambda i: (i, 0))
    vec = pl.BlockSpec((1, d), lambda i: (0, 0))
    return pl.pallas_call(
        _outln_kernel,
        grid=(n // tm,),
        in_specs=[row, pl.BlockSpec((d, d), lambda i: (0, 0)), row, vec, vec],
        out_specs=[row, pl.BlockSpec((tm * ns, LANES), lambda i: (i, 0))],
        out_shape=[jax.ShapeDtypeStruct((n, d), F32), jax.ShapeDtypeStruct((n * ns, LANES), U32)],
        compiler_params=_cp(("parallel",)),
        name="outproj_ln",
    )(y, w, h, g.reshape(1, d), b.reshape(1, d))


def _router_kernel(x_ref, wh_ref, wl_ref, eb_ref, idx_ref, wt_ref, rank_ref, cnt_ref, base_sc, *, tm):
    i = pl.program_id(0)
    E, G = N_EXPERTS, N_GROUPS
    per = E // G

    @pl.when(i == 0)
    def _():
        base_sc[...] = jnp.zeros(base_sc.shape, F32)

    x = x_ref[...]
    xh = x.astype(BF16)
    xl = (x - xh.astype(F32)).astype(BF16)
    wh, wl = wh_ref[...], wl_ref[...]
    logits = (lax.dot_general(wh, xh, NT, preferred_element_type=F32)
              + lax.dot_general(wh, xl, NT, preferred_element_type=F32)
              + lax.dot_general(wl, xh, NT, preferred_element_type=F32))
    scores = jax.nn.sigmoid(logits)
    biased = scores + eb_ref[...]
    s3 = scores.reshape(G, per, tm)
    b3 = biased.reshape(G, per, tm)
    j_io = lax.broadcasted_iota(jnp.int32, (G, per, tm), 1).astype(F32)
    g_io3 = lax.broadcasted_iota(jnp.int32, (G, per, tm), 0).astype(F32)
    e_io = g_io3 * per + j_io
    g_io = lax.broadcasted_iota(jnp.int32, (G, 1, tm), 0).astype(F32)
    ninf = -jnp.inf

    m1 = jnp.max(b3, axis=1, keepdims=True)
    i1 = jnp.min(jnp.where(b3 == m1, j_io, float(per)), axis=1, keepdims=True)
    m2 = jnp.max(jnp.where(j_io == i1, ninf, b3), axis=1, keepdims=True)
    gs = m1 + m2
    gsel = jnp.zeros((G, 1, tm), F32)
    for _ in range(TOPK_GROUPS):
        gm = jnp.max(gs, axis=0, keepdims=True)
        gi = jnp.min(jnp.where(gs == gm, g_io, float(G)), axis=0, keepdims=True)
        hit = g_io == gi
        gsel = jnp.where(hit, 1.0, gsel)
        gs = jnp.where(hit, ninf, gs)
    masked = jnp.where(gsel > 0.0, b3, ninf)

    def red2(fn, a):
        return fn(fn(a, axis=1, keepdims=True), axis=0, keepdims=True)

    sel = jnp.zeros((G, per, tm), F32)
    idxs, scs = [], []
    for _ in range(TOP_K):
        mx = red2(jnp.max, masked)
        ei = red2(jnp.min, jnp.where(masked == mx, e_io, float(E)))
        hit = e_io == ei
        scs.append(red2(jnp.sum, jnp.where(hit, s3, 0.0)))
        idxs.append(ei)
        sel = jnp.where(hit, 1.0, sel)
        masked = jnp.where(hit, ninf, masked)
    wsum = scs[0]
    for k in range(1, TOP_K):
        wsum = wsum + scs[k]

    sel2 = sel.reshape(E, tm)
    upper = (lax.broadcasted_iota(jnp.int32, (tm, tm), 0) < lax.broadcasted_iota(jnp.int32, (tm, tm), 1))
    excl = jnp.dot(sel2.astype(BF16), upper.astype(BF16), preferred_element_type=F32)
    cnt3 = (excl + base_sc[...]).reshape(G, per, tm)
    for k in range(TOP_K):
        hit = e_io == idxs[k]
        rank = red2(jnp.sum, jnp.where(hit, cnt3, 0.0))
        idx_ref[k:k + 1, :] = idxs[k].reshape(1, tm).astype(jnp.int32)
        wt_ref[k:k + 1, :] = (scs[k] / wsum * ROUTED_SCALE).reshape(1, tm)
        rank_ref[k:k + 1, :] = rank.reshape(1, tm).astype(jnp.int32)
    base_sc[...] = base_sc[...] + jnp.sum(sel2, axis=1, keepdims=True)
    cnt_ref[...] = jnp.broadcast_to(base_sc[...], cnt_ref.shape).astype(jnp.int32)


def _router(h, wr_hi, wr_lo, e_bias):
    n, d = h.shape
    tm = _pick(n, (512, 256, 128))
    outk = pl.BlockSpec((TOP_K, tm), lambda i: (0, i))
    return pl.pallas_call(
        functools.partial(_router_kernel, tm=tm),
        grid=(n // tm,),
        in_specs=[pl.BlockSpec((tm, d), lambda i: (i, 0)),
                  pl.BlockSpec((N_EXPERTS, d), lambda i: (0, 0)),
                  pl.BlockSpec((N_EXPERTS, d), lambda i: (0, 0)),
                  pl.BlockSpec((N_EXPERTS, 1), lambda i: (0, 0))],
        out_specs=[outk, outk, outk, pl.BlockSpec((N_EXPERTS, LANES), lambda i: (0, 0))],
        out_shape=[jax.ShapeDtypeStruct((TOP_K, n), jnp.int32),
                   jax.ShapeDtypeStruct((TOP_K, n), F32),
                   jax.ShapeDtypeStruct((TOP_K, n), jnp.int32),
                   jax.ShapeDtypeStruct((N_EXPERTS, LANES), jnp.int32)],
        scratch_shapes=[pltpu.VMEM((N_EXPERTS, 1), F32)],
        compiler_params=_cp(("arbitrary",)),
        name="moe_router",
    )(h, wr_hi, wr_lo, e_bias.reshape(N_EXPERTS, 1))


def _row_copy(src, dst, sem):
    return pltpu.make_async_copy(src, dst, sem)


def _slab_rows(i, ns):
    return pl.ds(pl.multiple_of(i * ns, ns), ns)


def _dispatch_kernel(ps_ref, pc_ref, dest_ref, x_ref, xs_ref, zbuf, sem, *, tm, ns, blk, n_blocks):
    i = pl.program_id(0)

    @pl.when(i == 0)
    def _():
        zbuf[...] = jnp.zeros(zbuf.shape, U32)

        def run(start_slot, rows, wait):
            cp = _row_copy(zbuf.at[pl.ds(0, rows * ns), :],
                           xs_ref.at[pl.ds(pl.multiple_of(start_slot * ns, ns), rows * ns), :], sem.at[1])
            cp.wait() if wait else cp.start()

        def per_expert(e, carry):
            base, cnt = ps_ref[e], pc_ref[e]
            for wait in (False, True):
                for bit in range(blk.bit_length() - 1):
                    size = 1 << bit

                    @pl.when((cnt & size) != 0)
                    def _():
                        run(base + (cnt & (size - 1)), size, wait)
            return carry

        lax.fori_loop(0, N_EXPERTS, per_expert, 0)

        def per_block(j, carry):
            run(j * blk, blk, False)
            run(j * blk, blk, True)
            return carry

        lax.fori_loop(ps_ref[N_EXPERTS], n_blocks, per_block, 0)

    def copy(k, r):
        d = dest_ref[k, r]
        return _row_copy(x_ref.at[pl.ds(r * ns, ns), :], xs_ref.at[_slab_rows(d, ns), :], sem.at[0])

    for r in range(tm):
        for k in range(TOP_K):
            copy(k, r).start(priority=k % 2)

    def drain(r, carry):
        for k in range(TOP_K):
            copy(k, r).wait()
        return carry

    lax.fori_loop(0, tm, drain, 0)


def _dispatch(hs, dest, pad_start, pad_count, n_used, n_blocks):
    ns = hs.shape[0] // dest.shape[1]
    n = dest.shape[1]
    tm = _pick(n, (2 * LANES, LANES))
    blk = EXPERT_ROWS
    assert blk & (blk - 1) == 0
    grid_spec = pltpu.PrefetchScalarGridSpec(
        num_scalar_prefetch=2,
        grid=(n // tm,),
        in_specs=[pl.BlockSpec((TOP_K, tm), lambda i, ps, pc: (0, i), memory_space=pltpu.SMEM),
                  pl.BlockSpec((tm * ns, LANES), lambda i, ps, pc: (i, 0))],
        out_specs=pl.BlockSpec(memory_space=pl.ANY),
        scratch_shapes=[pltpu.VMEM((blk * ns, LANES), U32), pltpu.SemaphoreType.DMA((2,))],
    )
    return pl.pallas_call(
        functools.partial(_dispatch_kernel, tm=tm, ns=ns, blk=blk, n_blocks=n_blocks),
        grid_spec=grid_spec,
        out_shape=jax.ShapeDtypeStruct((n_blocks * blk * ns, LANES), U32),
        compiler_params=_cp(("arbitrary",)),
        name="moe_dispatch",
    )(jnp.concatenate([pad_start, n_used]), pad_count, dest, hs)


def _expert_kernel(be_ref, nu_ref, x_ref, w1_ref, w3_ref, w2_ref, o_ref, w1b, w3b, w2b, *, tm, ns):
    j = pl.program_id(0)
    n_used = nu_ref[0]

    @pl.when(j < n_used)
    def _():
        @pl.when((j == 0) | (be_ref[j] != be_ref[jnp.maximum(j - 1, 0)]))
        def _():
            w1b[...] = w1_ref[...].astype(BF16)
            w3b[...] = w3_ref[...].astype(BF16)
            w2b[...] = w2_ref[...].astype(BF16)

        parts = []
        for s in range(ns):
            lo, hi = _load_slab_pairs(x_ref, (), s, tm, ns)
            parts += [lo.astype(BF16), hi.astype(BF16)]
        x = jnp.concatenate(parts, axis=1)
        a = jnp.dot(x, w1b[...], preferred_element_type=F32)
        b = jnp.dot(x, w3b[...], preferred_element_type=F32)
        hb = (_silu(a) * b).astype(BF16)
        _store_slab(o_ref, jnp.dot(hb, w2b[...], preferred_element_type=F32))

    @pl.when(j >= n_used)
    def _():
        o_ref[...] = jnp.zeros(o_ref.shape, U32)


def _experts(xs, blk_e, n_used, w1, w3, w2, layer):
    d, f = w1.shape[2], w1.shape[3]
    ns = d // HEAD_SLAB
    nb = blk_e.shape[0]
    tm = EXPERT_ROWS
    grid_spec = pltpu.PrefetchScalarGridSpec(
        num_scalar_prefetch=2,
        grid=(nb,),
        in_specs=[pl.BlockSpec((tm * ns, LANES), lambda j, be, nu: (jnp.minimum(j, nu[0] - 1), 0)),
                  pl.BlockSpec((None, None, d, f), lambda j, be, nu: (layer, be[j], 0, 0)),
                  pl.BlockSpec((None, None, d, f), lambda j, be, nu: (layer, be[j], 0, 0)),
                  pl.BlockSpec((None, None, f, d), lambda j, be, nu: (layer, be[j], 0, 0))],
        out_specs=pl.BlockSpec((tm * ns, LANES), lambda j, be, nu: (j, 0)),
        scratch_shapes=[pltpu.VMEM((d, f), BF16), pltpu.VMEM((d, f), BF16), pltpu.VMEM((f, d), BF16)],
    )
    return pl.pallas_call(
        functools.partial(_expert_kernel, tm=tm, ns=ns),
        grid_spec=grid_spec,
        out_shape=jax.ShapeDtypeStruct((nb * tm * ns, LANES), U32),
        compiler_params=_cp(("arbitrary",)),
        name="moe_experts",
    )(blk_e, n_used, xs, w1, w3, w2)


def _combine_kernel(dest_ref, dnxt_ref, h_ref, wt_ref, ws1_ref, ws3_ref, ws2_ref, g_ref, b_ref, ys_ref,
                    of_ref, *rest, tm, ns, nsteps):
    ob_ref = rest[0] if len(rest) == 3 else None
    gbuf, sem = rest[-2:]
    i = pl.program_id(0)
    slot = lax.rem(i, 2)

    def copy(d_ref, k, r, dst_slot):
        d = d_ref[k, r]
        return _row_copy(ys_ref.at[_slab_rows(d, ns), :], gbuf.at[dst_slot, k, _slab_rows(r, ns), :],
                         sem.at[dst_slot])

    def loop(d_ref, dst_slot, wait):
        def body(r, carry):
            for k in range(TOP_K):
                cp = copy(d_ref, k, r, dst_slot)
                if wait:
                    cp.wait()
                else:
                    cp.start(priority=k % 2)
            return carry

        lax.fori_loop(0, tm, body, 0)

    @pl.when(i == 0)
    def _():
        loop(dest_ref, slot, False)

    loop(dest_ref, slot, True)
    for r in range(tm):
        for k in range(TOP_K):
            copy(dnxt_ref, k, r, 1 - slot).start(priority=k % 2)
    h = h_ref[...]
    xb = h.astype(BF16)
    a = jnp.dot(xb, ws1_ref[...], preferred_element_type=F32)
    b = jnp.dot(xb, ws3_ref[...], preferred_element_type=F32)
    shared = jnp.dot((_silu(a) * b).astype(BF16), ws2_ref[...], preferred_element_type=F32)
    wt = wt_ref[...]
    parts = []
    for s in range(ns):
        lo = hi = None
        for k in range(TOP_K):
            plo, phi = _load_slab_pairs(gbuf, (slot, k), s, tm, ns)
            w = wt[:, k:k + 1]
            lo = w * plo if lo is None else lo + w * plo
            hi = w * phi if hi is None else hi + w * phi
        parts += [lo, hi]
    acc = shared + jnp.concatenate(parts, axis=1)
    r = _ln_rows(ALPHA * h + acc, g_ref[...], b_ref[...])
    of_ref[...] = r
    if ob_ref is not None:
        ob_ref[...] = r.astype(BF16)

    @pl.when(i == nsteps - 1)
    def _():
        loop(dnxt_ref, 1 - slot, True)


def _combine(h, dest, wt_tok, ws1, ws3, ws2, g, b, ys, final_shape=None):
    n, d = h.shape
    f = ws1.shape[1]
    tm = LANES
    row = pl.BlockSpec((tm, d), lambda i: (i, 0))
    vec = pl.BlockSpec((1, d), lambda i: (0, 0))
    if final_shape is None:
        out_specs = [row, row]
        out_shape = [jax.ShapeDtypeStruct((n, d), F32), jax.ShapeDtypeStruct((n, d), BF16)]
    else:
        batch, seq = final_shape
        nt = seq // tm + 1
        out_specs = [pl.BlockSpec((None, tm, d), lambda i: (i // nt, jnp.maximum(i % nt - 1, 0), 0))]
        out_shape = [jax.ShapeDtypeStruct((batch, seq, d), F32)]
    ns = d // HEAD_SLAB
    nsteps = n // tm
    return pl.pallas_call(
        functools.partial(_combine_kernel, tm=tm, ns=ns, nsteps=nsteps),
        grid=(nsteps,),
        in_specs=[pl.BlockSpec((TOP_K, tm), lambda i: (0, i), memory_space=pltpu.SMEM),
                  pl.BlockSpec((TOP_K, tm), lambda i: (0, jnp.minimum(i + 1, nsteps - 1)), memory_space=pltpu.SMEM),
                  row,
                  pl.BlockSpec((tm, TOP_K), lambda i: (i, 0)),
                  pl.BlockSpec((d, f), lambda i: (0, 0)),
                  pl.BlockSpec((d, f), lambda i: (0, 0)),
                  pl.BlockSpec((f, d), lambda i: (0, 0)),
                  vec, vec,
                  pl.BlockSpec(memory_space=pl.ANY)],
        out_specs=out_specs,
        out_shape=out_shape,
        scratch_shapes=[pltpu.VMEM((2, TOP_K, tm * ns, LANES), U32), pltpu.SemaphoreType.DMA((2,))],
        compiler_params=_cp(("arbitrary",)),
        name="moe_combine",
    )(dest, dest, h, wt_tok, ws1, ws3, ws2, g.reshape(1, d), b.reshape(1, d), ys)


def _moe(h_f32, hs, w_router, e_bias, w1, w3, w2, layer, ws1, ws3, ws2, g, b, final_shape=None):
    n = h_f32.shape[0]
    wr_t = w_router.T
    wr_hi = wr_t.astype(BF16)
    wr_lo = (wr_t - wr_hi.astype(F32)).astype(BF16)
    idx, wts, rank, cnt = _router(h_f32, wr_hi, wr_lo, e_bias)
    counts = cnt[:, 0]
    blk = EXPERT_ROWS
    pcounts = (counts + blk - 1) // blk * blk
    pends = jnp.cumsum(pcounts)
    pstarts = pends - pcounts
    e_ids = jnp.arange(N_EXPERTS, dtype=jnp.int32)
    start_of = jnp.sum(jnp.where(idx[None] == e_ids[:, None, None], pstarts[:, None, None], 0), axis=0)
    dest = (start_of + rank).astype(jnp.int32)
    nb = n * TOP_K // blk + N_EXPERTS
    blk_e = jnp.minimum(jnp.sum(pends[None, :] <= (jnp.arange(nb, dtype=jnp.int32) * blk)[:, None], axis=1),
                        N_EXPERTS - 1).astype(jnp.int32)
    n_used = (pends[-1:] // blk).astype(jnp.int32)
    xs = _dispatch(hs, dest, (pstarts + counts).astype(jnp.int32), (pcounts - counts).astype(jnp.int32), n_used, nb)
    ys = _experts(xs, blk_e, n_used, w1, w3, w2, layer)
    return _combine(h_f32, dest, wts.T, ws1.astype(BF16), ws3.astype(BF16), ws2.astype(BF16), g, b, ys,
                    final_shape)


def _in_proj_layout(w_in, b_in, d, ql, kvl):
    hq, hv = ML_HEADS * ML_QK, ML_HEADS * ML_V
    sizes = (ql, kvl, QK_ROPE, hq, hq, hv, hv, ML_HEADS, ML_HEADS, d, d)
    offs = np.concatenate([[0], np.cumsum(sizes)])
    seg = lambda a, i: a[..., int(offs[i]):int(offs[i + 1])]
    order = (0, 1, 3, 4, 5, 6, 9, 10)
    w_main = jnp.concatenate([seg(w_in, i) for i in order], axis=-1)
    b_main = jnp.concatenate([seg(b_in, i) for i in order], axis=-1)
    half = QK_ROPE // 2

    def small(a):
        kr = seg(a, 2)
        x1, x2 = kr[..., :half], kr[..., half:]
        pad = jnp.zeros(a.shape[:-1] + (LANES - 2 * ML_HEADS,), a.dtype)
        return jnp.concatenate([x1, x2, x2, x1, seg(a, 7), seg(a, 8), pad], axis=-1)

    main_offs = np.concatenate([[0], np.cumsum([sizes[i] for i in order])])
    return w_main, b_main, small(w_in), small(b_in), [int(o) for o in main_offs]


def _uq_layout(w_uq):
    kq = w_uq.shape[0]
    w = w_uq.reshape(kq, MLA_HEADS, QK_NOPE + QK_ROPE)
    half = QK_ROPE // 2
    nope, x1, x2 = w[..., :QK_NOPE], w[..., QK_NOPE:QK_NOPE + half], w[..., QK_NOPE + half:]
    return jnp.concatenate([nope, x1, x2, x2, x1], axis=-1).reshape(kq, MLA_HEADS * HEAD_SLAB)


def _ukv_layout(w_ukv):
    kk = w_ukv.shape[0]
    w = w_ukv.reshape(kk, MLA_HEADS, QK_NOPE + V_HEAD)
    return jnp.concatenate([w[..., :QK_NOPE].reshape(kk, -1), w[..., QK_NOPE:].reshape(kk, -1)], axis=-1)


def _rope_tables(tp):
    half = QK_ROPE // 2
    pos = jnp.arange(tp, dtype=F32) - PADL
    inv_freq = 1.0 / (ROPE_THETA ** (jnp.arange(0, QK_ROPE, 2, dtype=F32) / QK_ROPE))
    ang = pos[:, None] * inv_freq[None, :]
    cos, sin = jnp.cos(ang), jnp.sin(ang)
    zero = jnp.zeros((tp, LANES - 2 * half), F32)
    return jnp.concatenate([cos, cos, zero], axis=1), jnp.concatenate([-sin, sin, zero], axis=1)


def kernel(x, meta, ln_in_g, ln_in_b, w_in, b_in, q_norm_g, kv_norm_g, w_uq, w_ukv, conv_w, conv_b, ml_norm_g, w_br_mla, w_br_mlstm, w_out, ln1_g, ln1_b, w_router, e_bias, w1, w3, w2, ws1, ws3, ws2, ln2_g, ln2_b):
    batch, seq, d = x.shape
    depth = w_in.shape[0]
    ql, kvl = q_norm_g.shape[1], kv_norm_g.shape[1]
    assert depth == DEPTH and ql == kvl and seq % LANES == 0
    tp = LANES + seq
    hq = ML_HEADS * ML_QK

    head = jnp.concatenate([jnp.zeros((PADL, d), x.dtype), meta.astype(x.dtype)], axis=0)
    h_f32, h_bf = _layer_norm_in(x, head, ln_in_g, ln_in_b)
    cos_t, sin_t = _rope_tables(tp)
    col_scale = jnp.concatenate([jnp.ones((1, hq), F32), jnp.full((1, hq), ML_QK ** -0.5, F32)], axis=1)

    for l in range(depth):
        w_main, b_main, w_small, b_small, offs = _in_proj_layout(w_in[l], b_in[l], d, ql, kvl)
        o_mq, o_mv, o_mo, o_ga, o_gb = offs[2], offs[4], offs[5], offs[6], offs[7]
        z = _matmul_bias(h_bf, w_main.astype(BF16), b_main, BF16, sig_col=o_mo, name="in_proj")
        zs = _matmul_bias(h_bf, w_small.astype(BF16), b_small, F32, name="in_proj_small")

        q, k, v = _mla_qkv(z, zs, q_norm_g[l], kv_norm_g[l], _uq_layout(w_uq[l]).astype(BF16),
                           _ukv_layout(w_ukv[l]).astype(BF16), cos_t, sin_t, tp, ql)
        attn = _attention(q, k, v, batch, tp)

        assert o_mq % hq == 0 and o_mv % ML_V == 0 and o_mo % ML_V == 0
        qk = _qk_conv(z, conv_w[l], conv_b[l], col_scale, tp, o_mq // hq)
        hm = _mlstm(qk, z, zs, ml_norm_g[l], batch, tp, o_mv // ML_V, o_mo // ML_V)

        y = _merge(attn, hm, w_br_mla[l].astype(BF16), w_br_mlstm[l].astype(BF16), z, o_ga, o_gb)
        h_f32, hs = _outproj_ln(y, w_out[l].astype(BF16), h_f32, ln1_g[l], ln1_b[l])
        outs = _moe(h_f32, hs, w_router[l], e_bias[l], w1, w3, w2, l, ws1[l], ws3[l], ws2[l],
                    ln2_g[l], ln2_b[l], final_shape=(batch, seq) if l == depth - 1 else None)
        if l == depth - 1:
            return outs[0]
        h_f32, h_bf = outs
```

```python
import functools

import numpy as np
import jax
import jax.numpy as jnp
from jax import lax
from jax.experimental import pallas as pl
from jax.experimental.pallas import tpu as pltpu

N_META = 16
MLA_HEADS = 8
QK_NOPE = 128
QK_ROPE = 64
V_HEAD = 128
ROPE_THETA = 10000.0
ML_HEADS = 8
ML_QK = 128
ML_V = 256
CONV_K = 4
CHUNK = 64
N_EXPERTS = 64
TOP_K = 8
N_GROUPS = 8
TOPK_GROUPS = 4
ROUTED_SCALE = 2.5
DEPTH = 2
ALPHA = (2 * DEPTH) ** 0.25
LN_EPS = 1e-5
RMS_EPS = 1e-6
NEG = -1e30

LANES = 128
HEAD_SLAB = 2 * LANES
PADL = LANES - N_META
EXPERT_ROWS = 1024
VMEM_LIMIT = 56 * 1024 * 1024

F32 = jnp.float32
BF16 = jnp.bfloat16
U32 = jnp.uint32
NT = (((1,), (1,)), ((), ()))


def _pick(n, cands):
    for c in cands:
        if n % c == 0:
            return c
    raise ValueError(f"no tile in {cands} divides {n}")


def _cp(sem):
    return pltpu.CompilerParams(dimension_semantics=sem, vmem_limit_bytes=VMEM_LIMIT)


def _ln_rows(x, g, b):
    mu = jnp.mean(x, axis=-1, keepdims=True)
    xc = x - mu
    var = jnp.mean(xc * xc, axis=-1, keepdims=True)
    return xc * lax.rsqrt(var + LN_EPS) * g + b


def _rms_rows(x, g):
    return x * lax.rsqrt(jnp.mean(x * x, axis=-1, keepdims=True) + RMS_EPS) * g


def _silu(x):
    return x * jax.nn.sigmoid(x)


def _store_slab(ref, x):
    rows, ns = x.shape[0], x.shape[1] // HEAD_SLAB
    for s in range(ns):
        lo = lax.bitcast_convert_type(x[:, s * HEAD_SLAB:s * HEAD_SLAB + LANES].astype(BF16).astype(F32), U32)
        hi = lax.bitcast_convert_type(x[:, s * HEAD_SLAB + LANES:(s + 1) * HEAD_SLAB].astype(BF16).astype(F32), U32)
        ref[pl.ds(s, rows, stride=ns), :] = (lo >> 16) | hi


def _load_slab_pairs(ref, lead, s, rows, ns):
    u = ref[lead + (pl.ds(s, rows, stride=ns), slice(None))]
    return (lax.bitcast_convert_type(u << 16, F32),
            lax.bitcast_convert_type(u & jnp.uint32(0xFFFF0000), F32))


def _ln_in_kernel(head_ref, x_ref, g_ref, b_ref, of_ref, ob_ref):
    i = pl.program_id(1)

    def emit(src):
        y = _ln_rows(src, g_ref[...], b_ref[...])
        of_ref[...] = y
        ob_ref[...] = y.astype(BF16)

    @pl.when(i == 0)
    def _():
        emit(head_ref[...])

    @pl.when(i > 0)
    def _():
        emit(x_ref[...])


def _layer_norm_in(x, head, g, b):
    batch, seq, d = x.shape
    nt = seq // LANES + 1
    row = pl.BlockSpec((LANES, d), lambda bb, i: (bb * nt + i, 0))
    vec = pl.BlockSpec((1, d), lambda bb, i: (0, 0))
    n = batch * nt * LANES
    return pl.pallas_call(
        _ln_in_kernel,
        grid=(batch, nt),
        in_specs=[pl.BlockSpec((LANES, d), lambda bb, i: (0, 0)),
                  pl.BlockSpec((None, LANES, d), lambda bb, i: (bb, jnp.maximum(i - 1, 0), 0)),
                  vec, vec],
        out_specs=[row, row],
        out_shape=[jax.ShapeDtypeStruct((n, d), F32), jax.ShapeDtypeStruct((n, d), BF16)],
        compiler_params=_cp(("parallel", "arbitrary")),
        name="ln_in",
    )(head, x, g.reshape(1, d), b.reshape(1, d))


def _mm_kernel(x_ref, w_ref, b_ref, o_ref, *, sig_tile):
    acc = jnp.dot(x_ref[...], w_ref[...], preferred_element_type=F32) + b_ref[...]
    if sig_tile is None:
        o_ref[...] = acc.astype(o_ref.dtype)
        return
    j = pl.program_id(0)

    @pl.when(j < sig_tile)
    def _():
        o_ref[...] = acc.astype(o_ref.dtype)

    @pl.when(j >= sig_tile)
    def _():
        o_ref[...] = jax.nn.sigmoid(acc).astype(o_ref.dtype)


def _matmul_bias(x, w, b, out_dtype, sig_col=None, name="mm"):
    n, kd = x.shape
    nc = w.shape[1]
    tm = _pick(n, (1024, 512, 256, 128))
    tn = _pick(nc, (1024, 512, 256, 128))
    sig_tile = None
    if sig_col is not None:
        assert sig_col % tn == 0
        sig_tile = sig_col // tn
    return pl.pallas_call(
        functools.partial(_mm_kernel, sig_tile=sig_tile),
        grid=(nc // tn, n // tm),
        in_specs=[pl.BlockSpec((tm, kd), lambda j, i: (i, 0)),
                  pl.BlockSpec((kd, tn), lambda j, i: (0, j)),
                  pl.BlockSpec((1, tn), lambda j, i: (0, j))],
        out_specs=pl.BlockSpec((tm, tn), lambda j, i: (i, j)),
        out_shape=jax.ShapeDtypeStruct((n, nc), out_dtype),
        compiler_params=_cp(("parallel", "parallel")),
        name=name,
    )(x, w, b.reshape(1, nc))


def _rope_slab(r, cos_t, sin_t):
    return r * cos_t + pltpu.roll(r, 2 * (QK_ROPE // 2), 1) * sin_t


def _qproj_kernel(c_ref, g_ref, w_ref, cos_ref, sin_ref, o_ref, *, heads, scale):
    xn = _rms_rows(c_ref[...].astype(F32), g_ref[...])
    q = jnp.dot(xn.astype(BF16), w_ref[...], preferred_element_type=F32) * scale
    cos_t, sin_t = cos_ref[...], sin_ref[...]
    for h in range(heads):
        lo = h * HEAD_SLAB
        o_ref[:, lo:lo + LANES] = q[:, lo:lo + LANES].astype(BF16)
        o_ref[:, lo + LANES:lo + HEAD_SLAB] = _rope_slab(q[:, lo + LANES:lo + HEAD_SLAB], cos_t, sin_t).astype(BF16)


def _kproj_kernel(c_ref, g_ref, w_ref, kr_ref, cos_ref, sin_ref, k_ref, v_ref, *, heads):
    xn = _rms_rows(c_ref[...].astype(F32), g_ref[...])
    kv = jnp.dot(xn.astype(BF16), w_ref[...], preferred_element_type=F32)
    rr_t = _rope_slab(kr_ref[...], cos_ref[...], sin_ref[...]).T.astype(BF16)
    for h in range(heads):
        lo = h * HEAD_SLAB
        k_ref[lo:lo + LANES, :] = kv[:, h * LANES:(h + 1) * LANES].T.astype(BF16)
        k_ref[lo + LANES:lo + HEAD_SLAB, :] = rr_t
    v_ref[...] = kv[:, heads * LANES:].astype(BF16)


def _mla_qkv(z, zs, qg, kvg, wq, wkv, cos_t, sin_t, tp, ql):
    n = z.shape[0]
    heads = MLA_HEADS
    tm = _pick(tp, (384, 256, 128))
    nt = tp // tm
    scale = (QK_NOPE + QK_ROPE) ** -0.5 * float(np.log2(np.e))
    tab = pl.BlockSpec((tm, LANES), lambda i: (i % nt, 0))
    q = pl.pallas_call(
        functools.partial(_qproj_kernel, heads=heads, scale=scale),
        grid=(n // tm,),
        in_specs=[pl.BlockSpec((tm, ql), lambda i: (i, 0)),
                  pl.BlockSpec((1, ql), lambda i: (0, 0)),
                  pl.BlockSpec((ql, heads * HEAD_SLAB), lambda i: (0, 0)),
                  tab, tab],
        out_specs=pl.BlockSpec((tm, heads * HEAD_SLAB), lambda i: (i, 0)),
        out_shape=jax.ShapeDtypeStruct((n, heads * HEAD_SLAB), BF16),
        compiler_params=_cp(("parallel",)),
        name="mla_qproj",
    )(z, qg.reshape(1, ql), wq, cos_t, sin_t)
    k, v = pl.pallas_call(
        functools.partial(_kproj_kernel, heads=heads),
        grid=(n // tm,),
        in_specs=[pl.BlockSpec((tm, ql), lambda i: (i, 1)),
                  pl.BlockSpec((1, ql), lambda i: (0, 0)),
                  pl.BlockSpec((ql, 2 * heads * LANES), lambda i: (0, 0)),
                  pl.BlockSpec((tm, LANES), lambda i: (i, 0)),
                  tab, tab],
        out_specs=[pl.BlockSpec((heads * HEAD_SLAB, tm), lambda i: (0, i)),
                   pl.BlockSpec((tm, heads * LANES), lambda i: (i, 0))],
        out_shape=[jax.ShapeDtypeStruct((heads * HEAD_SLAB, n), BF16),
                   jax.ShapeDtypeStruct((n, heads * LANES), BF16)],
        compiler_params=_cp(("parallel",)),
        name="mla_kvproj",
    )(z, kvg.reshape(1, ql), wkv, zs, cos_t, sin_t)
    return q, k, v


ATTN_HEADS_PER_STEP = 2


def _attn_kernel(q_ref, k_ref, v_ref, o_ref, s_sc, m_sc, l_sc, acc_sc, *, t, hps):
    i = pl.program_id(2)
    nl = t // LANES

    def lane_fold(x, fn):
        r = x[:, :LANES]
        for c in range(1, nl):
            r = fn(r, x[:, c * LANES:(c + 1) * LANES])
        return r

    m_sc[...] = jnp.full(m_sc.shape, NEG, F32)

    def scores(j, masked):
        rows = pl.ds(pl.multiple_of(j * t, t), t)
        if masked:
            qpos = i * t + lax.broadcasted_iota(jnp.int32, (t, t), 0)
            kpos = j * t + lax.broadcasted_iota(jnp.int32, (t, t), 1)
            keep = (kpos <= qpos) & (kpos >= PADL)
        for h in range(hps):
            q = q_ref[:, h * HEAD_SLAB:(h + 1) * HEAD_SLAB]
            s = jnp.dot(q, k_ref[h * HEAD_SLAB:(h + 1) * HEAD_SLAB, rows], preferred_element_type=F32)
            if masked:
                s = jnp.where(keep, s, NEG)
            s_sc[h, j] = s
            m_sc[h] = jnp.maximum(m_sc[h], lane_fold(s, jnp.maximum))

    scores(0, True)

    def score_body(j, carry):
        scores(j, False)
        return carry

    lax.fori_loop(1, i, score_body, 0)

    @pl.when(i > 0)
    def _():
        scores(i, True)

    for h in range(hps):
        m_sc[h] = jnp.broadcast_to(jnp.max(m_sc[h], axis=1, keepdims=True), (t, LANES))
    l_sc[...] = jnp.zeros(l_sc.shape, F32)
    acc_sc[...] = jnp.zeros(acc_sc.shape, F32)

    def pv_body(j, carry):
        rows = pl.ds(pl.multiple_of(j * t, t), t)
        for h in range(hps):
            s = s_sc[h, j]
            mb = m_sc[h]
            p = jnp.concatenate([jnp.exp2(s[:, c * LANES:(c + 1) * LANES] - mb) for c in range(nl)], axis=1)
            l_sc[h] = l_sc[h] + lane_fold(p, jnp.add)
            acc_sc[h] = acc_sc[h] + jnp.dot(p.astype(BF16), v_ref[rows, h * V_HEAD:(h + 1) * V_HEAD],
                                            preferred_element_type=F32)
        return carry

    lax.fori_loop(0, i + 1, pv_body, 0)

    for h in range(hps):
        l = jnp.sum(l_sc[h], axis=1, keepdims=True)
        o_ref[:, h * V_HEAD:(h + 1) * V_HEAD] = (acc_sc[h] / l).astype(o_ref.dtype)


def _attention(q, k, v, batch, tp):
    n = q.shape[0]
    heads = MLA_HEADS
    hps = ATTN_HEADS_PER_STEP
    assert heads % hps == 0
    t = _pick(tp, (384, 256, 128))
    nt = tp // t
    return pl.pallas_call(
        functools.partial(_attn_kernel, t=t, hps=hps),
        grid=(batch, heads // hps, nt),
        in_specs=[pl.BlockSpec((t, hps * HEAD_SLAB), lambda b, h, i: (b * nt + i, h)),
                  pl.BlockSpec((hps * HEAD_SLAB, tp), lambda b, h, i: (h, b)),
                  pl.BlockSpec((tp, hps * V_HEAD), lambda b, h, i: (b, h))],
        out_specs=pl.BlockSpec((t, hps * V_HEAD), lambda b, h, i: (b * nt + i, h)),
        out_shape=jax.ShapeDtypeStruct((n, heads * V_HEAD), BF16),
        scratch_shapes=[pltpu.VMEM((hps, nt, t, t), F32), pltpu.VMEM((hps, t, LANES), F32),
                        pltpu.VMEM((hps, t, LANES), F32), pltpu.VMEM((hps, t, V_HEAD), F32)],
        compiler_params=_cp(("parallel", "parallel", "parallel")),
        name="mla_attention",
    )(q, k, v)


HALO = 16


def _conv_kernel(x_ref, halo_ref, w_ref, b_ref, s_ref, o_ref, *, tm, tp):
    i = pl.program_id(0)
    start = lax.rem(i * tm, tp)
    pos = start + lax.broadcasted_iota(jnp.int32, (tm, 1), 0)
    x = jnp.where(pos >= PADL, x_ref[...].astype(F32), 0.0)
    hpos = start - HALO + lax.broadcasted_iota(jnp.int32, (HALO, 1), 0)
    halo = jnp.where(hpos >= PADL, halo_ref[...].astype(F32), 0.0)
    ext = jnp.concatenate([halo, x], axis=0)
    w = w_ref[...]
    y = b_ref[...] + w[CONV_K - 1:CONV_K, :] * x
    for j in range(1, CONV_K):
        y = y + w[CONV_K - 1 - j:CONV_K - j, :] * pltpu.roll(ext, j, 0)[HALO:, :]
    o_ref[...] = (_silu(y) * s_ref[...]).astype(BF16)


def _qk_conv(z, conv_w, conv_b, col_scale, tp, off_blk):
    n = z.shape[0]
    cw = conv_w.shape[1] // 2
    tm = _pick(tp, (384, 256, 128))
    return pl.pallas_call(
        functools.partial(_conv_kernel, tm=tm, tp=tp),
        grid=(n // tm, 2),
        in_specs=[pl.BlockSpec((tm, cw), lambda i, c: (i, off_blk + c)),
                  pl.BlockSpec((HALO, cw), lambda i, c: (jnp.maximum(i * (tm // HALO) - 1, 0), off_blk + c)),
                  pl.BlockSpec((CONV_K, cw), lambda i, c: (0, c)),
                  pl.BlockSpec((1, cw), lambda i, c: (0, c)),
                  pl.BlockSpec((1, cw), lambda i, c: (0, c))],
        out_specs=pl.BlockSpec((tm, cw), lambda i, c: (i, c)),
        out_shape=jax.ShapeDtypeStruct((n, 2 * cw), BF16),
        compiler_params=_cp(("parallel", "parallel")),
        name="mlstm_qk_conv",
    )(z, z, conv_w, conv_b.reshape(1, -1), col_scale)


MLSTM_HEADS_PER_STEP = 2


def _mlstm_kernel(*refs, group, heads, hps):
    q_refs, k_refs, v_refs = refs[:hps], refs[hps:2 * hps], refs[2 * hps:3 * hps]
    gt_ref = refs[3 * hps]
    og_refs = refs[3 * hps + 1:4 * hps + 1]
    ng_ref, o_ref, c_sc, n_sc, m_sc = refs[4 * hps + 1:]
    hp = pl.program_id(1)
    g = pl.program_id(2)
    L = CHUNK

    @pl.when(g == 0)
    def _():
        c_sc[...] = jnp.zeros(c_sc.shape, F32)
        n_sc[...] = jnp.zeros(n_sc.shape, F32)
        m_sc[...] = jnp.zeros(m_sc.shape, F32)

    R = group * L
    lane = lax.broadcasted_iota(jnp.int32, (R, LANES), 1)
    r_i = lax.broadcasted_iota(jnp.int32, (L, L), 0)
    c_i = lax.broadcasted_iota(jnp.int32, (L, L), 1)
    eye = r_i == c_i
    causal = c_i <= r_i

    gt = gt_ref[...]
    pad = (g * R + lax.broadcasted_iota(jnp.int32, (R, 1), 0)) < PADL
    i_parts, lf_parts = [], []
    for hh in range(hps):
        h = hp * hps + hh
        i_all = jnp.sum(jnp.where(lane == h, gt, 0.0), axis=1, keepdims=True)
        f_all = jnp.sum(jnp.where(lane == heads + h, gt, 0.0), axis=1, keepdims=True)
        i_parts.append(jnp.where(pad, NEG, i_all).reshape(group, L, 1))
        lf_parts.append(jnp.where(pad, 0.0, -(jnp.maximum(-f_all, 0.0) + jnp.log1p(jnp.exp(-jnp.abs(f_all)))))
                        .reshape(group, L, 1))
    i_col = jnp.concatenate(i_parts, axis=0)
    lf_col = jnp.concatenate(lf_parts, axis=0)
    b_row = jnp.sum(jnp.where((r_i <= c_i)[None], lf_col, 0.0), axis=1, keepdims=True)
    b_col = jnp.sum(jnp.where(eye[None], b_row, 0.0), axis=2, keepdims=True)
    i_row = jnp.sum(jnp.where(eye[None], i_col, 0.0), axis=1, keepdims=True)
    gtot = jnp.sum(lf_col, axis=1, keepdims=True)
    dmat = jnp.where(causal[None], b_col - b_row + i_row, NEG)
    rmax = jnp.max(dmat, axis=2, keepdims=True)
    a_col = gtot - b_col + i_col
    amax = jnp.max(a_col, axis=1, keepdims=True)

    m_prev_parts, m_new_parts = [], []
    for hh in range(hps):
        m = m_sc[hh].reshape(1, 1, 1)
        for c in range(group):
            idx = hh * group + c
            m_prev_parts.append(m)
            m = jnp.maximum(gtot[idx:idx + 1] + m, amax[idx:idx + 1])
            m_new_parts.append(m)
        m_sc[hh] = m.reshape(1, 1)
    m_prev = jnp.concatenate(m_prev_parts, axis=0)
    m_new = jnp.concatenate(m_new_parts, axis=0)
    inter = b_col + m_prev
    mrow = jnp.maximum(rmax, inter)
    dexp = jnp.exp(dmat - mrow)
    e_in = jnp.exp(inter - mrow)
    floor = jnp.exp(-mrow)
    ea = jnp.exp(a_col - m_new)
    decay = jnp.exp(gtot + m_prev - m_new)

    def stack(refs):
        return jnp.concatenate([r[...].reshape(group, L, r.shape[1]) for r in refs], axis=0)

    q3, k3, v3 = stack(q_refs), stack(k_refs), stack(v_refs)
    s3 = jnp.einsum("bqd,bkd->bqk", q3, k3, preferred_element_type=F32) * dexp
    intra = jnp.einsum("bqk,bkv->bqv", s3.astype(BF16), v3, preferred_element_type=F32)
    rowsum = jnp.sum(s3, axis=2, keepdims=True)
    ks3 = k3.astype(F32) * ea
    kv3 = jnp.einsum("bdk,bkv->bdv", jnp.swapaxes(ks3, 1, 2).astype(BF16), v3, preferred_element_type=F32)
    ksum = jnp.sum(ks3, axis=1, keepdims=True)

    c_parts, n_parts = [], []
    for hh in range(hps):
        cmat, nvec = c_sc[hh], n_sc[hh]
        for c in range(group):
            idx = hh * group + c
            c_parts.append(cmat[None])
            n_parts.append(nvec[None])
            cmat = decay[idx] * cmat + kv3[idx]
            nvec = decay[idx] * nvec + ksum[idx]
        c_sc[hh], n_sc[hh] = cmat, nvec
    c_prev = jnp.concatenate(c_parts, axis=0)
    n_prev = jnp.concatenate(n_parts, axis=0)
    num = intra + e_in * jnp.einsum("bqd,bdv->bqv", q3, c_prev.astype(BF16), preferred_element_type=F32)
    den = rowsum + e_in * jnp.sum(q3.astype(F32) * n_prev, axis=2, keepdims=True)
    hc = num / jnp.maximum(jnp.abs(den), floor)
    hc = hc * lax.rsqrt(jnp.mean(hc * hc, axis=2, keepdims=True) + RMS_EPS)
    for hh in range(hps):
        cols = slice(hh * ML_V, (hh + 1) * ML_V)
        hn = hc[hh * group:(hh + 1) * group].reshape(R, ML_V) * ng_ref[:, cols]
        o_ref[:, cols] = (hn * og_refs[hh][...].astype(F32)).astype(BF16)


def _mlstm(qk, z, zs, norm_g, batch, tp, v_blk, og_blk):
    n = qk.shape[0]
    heads = ML_HEADS
    hps = MLSTM_HEADS_PER_STEP
    assert heads % hps == 0
    nchunks = tp // CHUNK
    group = _pick(nchunks, (11, 6, 5, 4, 3, 2, 1))
    rows = group * CHUNK
    ng = nchunks // group

    def col(width, base):
        return [pl.BlockSpec((rows, width), lambda b, p, g, u=u: (b * ng + g, base + p * hps + u)) for u in range(hps)]

    return pl.pallas_call(
        functools.partial(_mlstm_kernel, group=group, heads=heads, hps=hps),
        grid=(batch, heads // hps, ng),
        in_specs=(col(ML_QK, 0) + col(ML_QK, heads) + col(ML_V, v_blk)
                  + [pl.BlockSpec((rows, LANES), lambda b, p, g: (b * ng + g, 1))]
                  + col(ML_V, og_blk)
                  + [pl.BlockSpec((1, hps * ML_V), lambda b, p, g: (0, p))]),
        out_specs=pl.BlockSpec((rows, hps * ML_V), lambda b, p, g: (b * ng + g, p)),
        out_shape=jax.ShapeDtypeStruct((n, heads * ML_V), BF16),
        scratch_shapes=[pltpu.VMEM((hps, ML_QK, ML_V), F32), pltpu.VMEM((hps, 1, ML_QK), F32),
                        pltpu.VMEM((hps, 1, 1), F32)],
        compiler_params=_cp(("parallel", "parallel", "arbitrary")),
        name="mlstm_scan",
    )(*([qk] * (2 * hps) + [z] * hps + [zs] + [z] * hps + [norm_g.reshape(1, -1)]))


def _merge_kernel(a_ref, hm_ref, wa_ref, wb_ref, ga_ref, gb_ref, o_ref):
    ya = jnp.dot(a_ref[...], wa_ref[...], preferred_element_type=F32)
    yb = jnp.dot(hm_ref[...], wb_ref[...], preferred_element_type=F32)
    o_ref[...] = (ga_ref[...].astype(F32) * ya + gb_ref[...].astype(F32) * yb).astype(BF16)


def _merge(attn, hm, wa, wb, z, ga_off, gb_off):
    n = attn.shape[0]
    d = wa.shape[1]
    tm = _pick(n, (512, 256, 128))
    tn = _pick(d, (1024, 512, 256, 128))
    assert ga_off % tn == 0 and gb_off % tn == 0
    ga_blk, gb_blk = ga_off // tn, gb_off // tn
    return pl.pallas_call(
        _merge_kernel,
        grid=(d // tn, n // tm),
        in_specs=[pl.BlockSpec((tm, attn.shape[1]), lambda j, i: (i, 0)),
                  pl.BlockSpec((tm, hm.shape[1]), lambda j, i: (i, 0)),
                  pl.BlockSpec((wa.shape[0], tn), lambda j, i: (0, j)),
                  pl.BlockSpec((wb.shape[0], tn), lambda j, i: (0, j)),
                  pl.BlockSpec((tm, tn), lambda j, i: (i, ga_blk + j)),
                  pl.BlockSpec((tm, tn), lambda j, i: (i, gb_blk + j))],
        out_specs=pl.BlockSpec((tm, tn), lambda j, i: (i, j)),
        out_shape=jax.ShapeDtypeStruct((n, d), BF16),
        compiler_params=_cp(("parallel", "parallel")),
        name="branch_merge",
    )(attn, hm, wa, wb, z, z)


def _outln_kernel(y_ref, w_ref, h_ref, g_ref, b_ref, of_ref, os_ref):
    o = jnp.dot(y_ref[...], w_ref[...], preferred_element_type=F32)
    r = _ln_rows(ALPHA * h_ref[...] + o, g_ref[...], b_ref[...])
    of_ref[...] = r
    _store_slab(os_ref, r)


def _outproj_ln(y, w, h, g, b):
    n, d = h.shape
    tm = _pick(n, (256, 128))
    ns = d // HEAD_SLAB
    row = pl.BlockSpec((tm, d), lambda i: (i, 0))
    vec = pl.BlockSpec((1, d), lambda i: (0, 0))
    return pl.pallas_call(
        _outln_kernel,
        grid=(n // tm,),
        in_specs=[row, pl.BlockSpec((d, d), lambda i: (0, 0)), row, vec, vec],
        out_specs=[row, pl.BlockSpec((tm * ns, LANES), lambda i: (i, 0))],
        out_shape=[jax.ShapeDtypeStruct((n, d), F32), jax.ShapeDtypeStruct((n * ns, LANES), U32)],
        compiler_params=_cp(("parallel",)),
        name="outproj_ln",
    )(y, w, h, g.reshape(1, d), b.reshape(1, d))


def _router_kernel(x_ref, wh_ref, wl_ref, eb_ref, idx_ref, wt_ref, rank_ref, cnt_ref, base_sc, *, tm):
    i = pl.program_id(0)
    E, G = N_EXPERTS, N_GROUPS
    per = E // G

    @pl.when(i == 0)
    def _():
        base_sc[...] = jnp.zeros(base_sc.shape, F32)

    x = x_ref[...]
    xh = x.astype(BF16)
    xl = (x - xh.astype(F32)).astype(BF16)
    wh, wl = wh_ref[...], wl_ref[...]
    logits = (lax.dot_general(wh, xh, NT, preferred_element_type=F32)
              + lax.dot_general(wh, xl, NT, preferred_element_type=F32)
              + lax.dot_general(wl, xh, NT, preferred_element_type=F32))
    scores = jax.nn.sigmoid(logits)
    biased = scores + eb_ref[...]
    s3 = scores.reshape(G, per, tm)
    b3 = biased.reshape(G, per, tm)
    j_io = lax.broadcasted_iota(jnp.int32, (G, per, tm), 1).astype(F32)
    g_io3 = lax.broadcasted_iota(jnp.int32, (G, per, tm), 0).astype(F32)
    e_io = g_io3 * per + j_io
    g_io = lax.broadcasted_iota(jnp.int32, (G, 1, tm), 0).astype(F32)
    ninf = -jnp.inf

    m1 = jnp.max(b3, axis=1, keepdims=True)
    i1 = jnp.min(jnp.where(b3 == m1, j_io, float(per)), axis=1, keepdims=True)
    m2 = jnp.max(jnp.where(j_io == i1, ninf, b3), axis=1, keepdims=True)
    gs = m1 + m2
    gsel = jnp.zeros((G, 1, tm), F32)
    for _ in range(TOPK_GROUPS):
        gm = jnp.max(gs, axis=0, keepdims=True)
        gi = jnp.min(jnp.where(gs == gm, g_io, float(G)), axis=0, keepdims=True)
        hit = g_io == gi
        gsel = jnp.where(hit, 1.0, gsel)
        gs = jnp.where(hit, ninf, gs)
    masked = jnp.where(gsel > 0.0, b3, ninf)

    def red2(fn, a):
        return fn(fn(a, axis=1, keepdims=True), axis=0, keepdims=True)

    sel = jnp.zeros((G, per, tm), F32)
    idxs, scs = [], []
    for _ in range(TOP_K):
        mx = red2(jnp.max, masked)
        ei = red2(jnp.min, jnp.where(masked == mx, e_io, float(E)))
        hit = e_io == ei
        scs.append(red2(jnp.sum, jnp.where(hit, s3, 0.0)))
        idxs.append(ei)
        sel = jnp.where(hit, 1.0, sel)
        masked = jnp.where(hit, ninf, masked)
    wsum = scs[0]
    for k in range(1, TOP_K):
        wsum = wsum + scs[k]

    sel2 = sel.reshape(E, tm)
    upper = (lax.broadcasted_iota(jnp.int32, (tm, tm), 0) < lax.broadcasted_iota(jnp.int32, (tm, tm), 1))
    excl = jnp.dot(sel2.astype(BF16), upper.astype(BF16), preferred_element_type=F32)
    cnt3 = (excl + base_sc[...]).reshape(G, per, tm)
    for k in range(TOP_K):
        hit = e_io == idxs[k]
        rank = red2(jnp.sum, jnp.where(hit, cnt3, 0.0))
        idx_ref[k:k + 1, :] = idxs[k].reshape(1, tm).astype(jnp.int32)
        wt_ref[k:k + 1, :] = (scs[k] / wsum * ROUTED_SCALE).reshape(1, tm)
        rank_ref[k:k + 1, :] = rank.reshape(1, tm).astype(jnp.int32)
    base_sc[...] = base_sc[...] + jnp.sum(sel2, axis=1, keepdims=True)
    cnt_ref[...] = jnp.broadcast_to(base_sc[...], cnt_ref.shape).astype(jnp.int32)


def _router(h, wr_hi, wr_lo, e_bias):
    n, d = h.shape
    tm = _pick(n, (512, 256, 128))
    outk = pl.BlockSpec((TOP_K, tm), lambda i: (0, i))
    return pl.pallas_call(
        functools.partial(_router_kernel, tm=tm),
        grid=(n // tm,),
        in_specs=[pl.BlockSpec((tm, d), lambda i: (i, 0)),
                  pl.BlockSpec((N_EXPERTS, d), lambda i: (0, 0)),
                  pl.BlockSpec((N_EXPERTS, d), lambda i: (0, 0)),
                  pl.BlockSpec((N_EXPERTS, 1), lambda i: (0, 0))],
        out_specs=[outk, outk, outk, pl.BlockSpec((N_EXPERTS, LANES), lambda i: (0, 0))],
        out_shape=[jax.ShapeDtypeStruct((TOP_K, n), jnp.int32),
                   jax.ShapeDtypeStruct((TOP_K, n), F32),
                   jax.ShapeDtypeStruct((TOP_K, n), jnp.int32),
                   jax.ShapeDtypeStruct((N_EXPERTS, LANES), jnp.int32)],
        scratch_shapes=[pltpu.VMEM((N_EXPERTS, 1), F32)],
        compiler_params=_cp(("arbitrary",)),
        name="moe_router",
    )(h, wr_hi, wr_lo, e_bias.reshape(N_EXPERTS, 1))


def _row_copy(src, dst, sem):
    return pltpu.make_async_copy(src, dst, sem)


def _slab_rows(i, ns):
    return pl.ds(pl.multiple_of(i * ns, ns), ns)


def _dispatch_kernel(ps_ref, pc_ref, dest_ref, x_ref, xs_ref, zbuf, sem, *, tm, ns, blk, n_blocks):
    i = pl.program_id(0)

    @pl.when(i == 0)
    def _():
        zbuf[...] = jnp.zeros(zbuf.shape, U32)

        def run(start_slot, rows, wait):
            cp = _row_copy(zbuf.at[pl.ds(0, rows * ns), :],
                           xs_ref.at[pl.ds(pl.multiple_of(start_slot * ns, ns), rows * ns), :], sem.at[1])
            cp.wait() if wait else cp.start()

        def per_expert(e, carry):
            base, cnt = ps_ref[e], pc_ref[e]
            for wait in (False, True):
                for bit in range(blk.bit_length() - 1):
                    size = 1 << bit

                    @pl.when((cnt & size) != 0)
                    def _():
                        run(base + (cnt & (size - 1)), size, wait)
            return carry

        lax.fori_loop(0, N_EXPERTS, per_expert, 0)

        def per_block(j, carry):
            run(j * blk, blk, False)
            run(j * blk, blk, True)
            return carry

        lax.fori_loop(ps_ref[N_EXPERTS], n_blocks, per_block, 0)

    def copy(k, r):
        d = dest_ref[k, r]
        return _row_copy(x_ref.at[pl.ds(r * ns, ns), :], xs_ref.at[_slab_rows(d, ns), :], sem.at[0])

    for r in range(tm):
        for k in range(TOP_K):
            copy(k, r).start(priority=k % 2)

    def drain(r, carry):
        for k in range(TOP_K):
            copy(k, r).wait()
        return carry

    lax.fori_loop(0, tm, drain, 0)


def _dispatch(hs, dest, pad_start, pad_count, n_used, n_blocks):
    ns = hs.shape[0] // dest.shape[1]
    n = dest.shape[1]
    tm = LANES
    blk = EXPERT_ROWS
    assert blk & (blk - 1) == 0
    grid_spec = pltpu.PrefetchScalarGridSpec(
        num_scalar_prefetch=2,
        grid=(n // tm,),
        in_specs=[pl.BlockSpec((TOP_K, tm), lambda i, ps, pc: (0, i), memory_space=pltpu.SMEM),
                  pl.BlockSpec((tm * ns, LANES), lambda i, ps, pc: (i, 0))],
        out_specs=pl.BlockSpec(memory_space=pl.ANY),
        scratch_shapes=[pltpu.VMEM((blk * ns, LANES), U32), pltpu.SemaphoreType.DMA((2,))],
    )
    return pl.pallas_call(
        functools.partial(_dispatch_kernel, tm=tm, ns=ns, blk=blk, n_blocks=n_blocks),
        grid_spec=grid_spec,
        out_shape=jax.ShapeDtypeStruct((n_blocks * blk * ns, LANES), U32),
        compiler_params=_cp(("arbitrary",)),
        name="moe_dispatch",
    )(jnp.concatenate([pad_start, n_used]), pad_count, dest, hs)


def _expert_kernel(be_ref, nu_ref, x_ref, w1_ref, w3_ref, w2_ref, o_ref, w1b, w3b, w2b, *, tm, ns):
    j = pl.program_id(0)
    n_used = nu_ref[0]

    @pl.when(j < n_used)
    def _():
        @pl.when((j == 0) | (be_ref[j] != be_ref[jnp.maximum(j - 1, 0)]))
        def _():
            w1b[...] = w1_ref[...].astype(BF16)
            w3b[...] = w3_ref[...].astype(BF16)
            w2b[...] = w2_ref[...].astype(BF16)

        parts = []
        for s in range(ns):
            lo, hi = _load_slab_pairs(x_ref, (), s, tm, ns)
            parts += [lo.astype(BF16), hi.astype(BF16)]
        x = jnp.concatenate(parts, axis=1)
        a = jnp.dot(x, w1b[...], preferred_element_type=F32)
        b = jnp.dot(x, w3b[...], preferred_element_type=F32)
        hb = (_silu(a) * b).astype(BF16)
        _store_slab(o_ref, jnp.dot(hb, w2b[...], preferred_element_type=F32))

    @pl.when(j >= n_used)
    def _():
        o_ref[...] = jnp.zeros(o_ref.shape, U32)


def _experts(xs, blk_e, n_used, w1, w3, w2, layer):
    d, f = w1.shape[2], w1.shape[3]
    ns = d // HEAD_SLAB
    nb = blk_e.shape[0]
    tm = EXPERT_ROWS
    grid_spec = pltpu.PrefetchScalarGridSpec(
        num_scalar_prefetch=2,
        grid=(nb,),
        in_specs=[pl.BlockSpec((tm * ns, LANES), lambda j, be, nu: (jnp.minimum(j, nu[0] - 1), 0)),
                  pl.BlockSpec((None, None, d, f), lambda j, be, nu: (layer, be[j], 0, 0)),
                  pl.BlockSpec((None, None, d, f), lambda j, be, nu: (layer, be[j], 0, 0)),
                  pl.BlockSpec((None, None, f, d), lambda j, be, nu: (layer, be[j], 0, 0))],
        out_specs=pl.BlockSpec((tm * ns, LANES), lambda j, be, nu: (j, 0)),
        scratch_shapes=[pltpu.VMEM((d, f), BF16), pltpu.VMEM((d, f), BF16), pltpu.VMEM((f, d), BF16)],
    )
    return pl.pallas_call(
        functools.partial(_expert_kernel, tm=tm, ns=ns),
        grid_spec=grid_spec,
        out_shape=jax.ShapeDtypeStruct((nb * tm * ns, LANES), U32),
        compiler_params=_cp(("arbitrary",)),
        name="moe_experts",
    )(blk_e, n_used, xs, w1, w3, w2)


def _combine_kernel(dest_ref, dnxt_ref, h_ref, wt_ref, ws1_ref, ws3_ref, ws2_ref, g_ref, b_ref, ys_ref,
                    of_ref, *rest, tm, ns, nsteps):
    ob_ref = rest[0] if len(rest) == 3 else None
    gbuf, sem = rest[-2:]
    i = pl.program_id(0)
    slot = lax.rem(i, 2)

    def copy(d_ref, k, r, dst_slot):
        d = d_ref[k, r]
        return _row_copy(ys_ref.at[_slab_rows(d, ns), :], gbuf.at[dst_slot, k, _slab_rows(r, ns), :],
                         sem.at[dst_slot])

    def loop(d_ref, dst_slot, wait):
        def body(r, carry):
            for k in range(TOP_K):
                cp = copy(d_ref, k, r, dst_slot)
                if wait:
                    cp.wait()
                else:
                    cp.start(priority=k % 2)
            return carry

        lax.fori_loop(0, tm, body, 0)

    @pl.when(i == 0)
    def _():
        loop(dest_ref, slot, False)

    loop(dest_ref, slot, True)
    for r in range(tm):
        for k in range(TOP_K):
            copy(dnxt_ref, k, r, 1 - slot).start(priority=k % 2)
    h = h_ref[...]
    xb = h.astype(BF16)
    a = jnp.dot(xb, ws1_ref[...], preferred_element_type=F32)
    b = jnp.dot(xb, ws3_ref[...], preferred_element_type=F32)
    shared = jnp.dot((_silu(a) * b).astype(BF16), ws2_ref[...], preferred_element_type=F32)
    wt = wt_ref[...]
    parts = []
    for s in range(ns):
        lo = hi = None
        for k in range(TOP_K):
            plo, phi = _load_slab_pairs(gbuf, (slot, k), s, tm, ns)
            w = wt[:, k:k + 1]
            lo = w * plo if lo is None else lo + w * plo
            hi = w * phi if hi is None else hi + w * phi
        parts += [lo, hi]
    acc = shared + jnp.concatenate(parts, axis=1)
    r = _ln_rows(ALPHA * h + acc, g_ref[...], b_ref[...])
    of_ref[...] = r
    if ob_ref is not None:
        ob_ref[...] = r.astype(BF16)

    @pl.when(i == nsteps - 1)
    def _():
        loop(dnxt_ref, 1 - slot, True)


def _combine(h, dest, wt_tok, ws1, ws3, ws2, g, b, ys, final_shape=None):
    n, d = h.shape
    f = ws1.shape[1]
    tm = LANES
    row = pl.BlockSpec((tm, d), lambda i: (i, 0))
    vec = pl.BlockSpec((1, d), lambda i: (0, 0))
    if final_shape is None:
        out_specs = [row, row]
        out_shape = [jax.ShapeDtypeStruct((n, d), F32), jax.ShapeDtypeStruct((n, d), BF16)]
    else:
        batch, seq = final_shape
        nt = seq // tm + 1
        out_specs = [pl.BlockSpec((None, tm, d), lambda i: (i // nt, jnp.maximum(i % nt - 1, 0), 0))]
        out_shape = [jax.ShapeDtypeStruct((batch, seq, d), F32)]
    ns = d // HEAD_SLAB
    nsteps = n // tm
    return pl.pallas_call(
        functools.partial(_combine_kernel, tm=tm, ns=ns, nsteps=nsteps),
        grid=(nsteps,),
        in_specs=[pl.BlockSpec((TOP_K, tm), lambda i: (0, i), memory_space=pltpu.SMEM),
                  pl.BlockSpec((TOP_K, tm), lambda i: (0, jnp.minimum(i + 1, nsteps - 1)), memory_space=pltpu.SMEM),
                  row,
                  pl.BlockSpec((tm, TOP_K), lambda i: (i, 0)),
                  pl.BlockSpec((d, f), lambda i: (0, 0)),
                  pl.BlockSpec((d, f), lambda i: (0, 0)),
                  pl.BlockSpec((f, d), lambda i: (0, 0)),
                  vec, vec,
                  pl.BlockSpec(memory_space=pl.ANY)],
        out_specs=out_specs,
        out_shape=out_shape,
        scratch_shapes=[pltpu.VMEM((2, TOP_K, tm * ns, LANES), U32), pltpu.SemaphoreType.DMA((2,))],
        compiler_params=_cp(("arbitrary",)),
        name="moe_combine",
    )(dest, dest, h, wt_tok, ws1, ws3, ws2, g.reshape(1, d), b.reshape(1, d), ys)


def _moe(h_f32, hs, w_router, e_bias, w1, w3, w2, layer, ws1, ws3, ws2, g, b, final_shape=None):
    n = h_f32.shape[0]
    wr_t = w_router.T
    wr_hi = wr_t.astype(BF16)
    wr_lo = (wr_t - wr_hi.astype(F32)).astype(BF16)
    idx, wts, rank, cnt = _router(h_f32, wr_hi, wr_lo, e_bias)
    counts = cnt[:, 0]
    blk = EXPERT_ROWS
    pcounts = (counts + blk - 1) // blk * blk
    pends = jnp.cumsum(pcounts)
    pstarts = pends - pcounts
    e_ids = jnp.arange(N_EXPERTS, dtype=jnp.int32)
    start_of = jnp.sum(jnp.where(idx[None] == e_ids[:, None, None], pstarts[:, None, None], 0), axis=0)
    dest = (start_of + rank).astype(jnp.int32)
    nb = n * TOP_K // blk + N_EXPERTS
    blk_e = jnp.minimum(jnp.sum(pends[None, :] <= (jnp.arange(nb, dtype=jnp.int32) * blk)[:, None], axis=1),
                        N_EXPERTS - 1).astype(jnp.int32)
    n_used = (pends[-1:] // blk).astype(jnp.int32)
    xs = _dispatch(hs, dest, (pstarts + counts).astype(jnp.int32), (pcounts - counts).astype(jnp.int32), n_used, nb)
    ys = _experts(xs, blk_e, n_used, w1, w3, w2, layer)
    return _combine(h_f32, dest, wts.T, ws1.astype(BF16), ws3.astype(BF16), ws2.astype(BF16), g, b, ys,
                    final_shape)


def _in_proj_layout(w_in, b_in, d, ql, kvl):
    hq, hv = ML_HEADS * ML_QK, ML_HEADS * ML_V
    sizes = (ql, kvl, QK_ROPE, hq, hq, hv, hv, ML_HEADS, ML_HEADS, d, d)
    offs = np.concatenate([[0], np.cumsum(sizes)])
    seg = lambda a, i: a[..., int(offs[i]):int(offs[i + 1])]
    order = (0, 1, 3, 4, 5, 6, 9, 10)
    w_main = jnp.concatenate([seg(w_in, i) for i in order], axis=-1)
    b_main = jnp.concatenate([seg(b_in, i) for i in order], axis=-1)
    half = QK_ROPE // 2

    def small(a):
        kr = seg(a, 2)
        x1, x2 = kr[..., :half], kr[..., half:]
        pad = jnp.zeros(a.shape[:-1] + (LANES - 2 * ML_HEADS,), a.dtype)
        return jnp.concatenate([x1, x2, x2, x1, seg(a, 7), seg(a, 8), pad], axis=-1)

    main_offs = np.concatenate([[0], np.cumsum([sizes[i] for i in order])])
    return w_main, b_main, small(w_in), small(b_in), [int(o) for o in main_offs]


def _uq_layout(w_uq):
    kq = w_uq.shape[0]
    w = w_uq.reshape(kq, MLA_HEADS, QK_NOPE + QK_ROPE)
    half = QK_ROPE // 2
    nope, x1, x2 = w[..., :QK_NOPE], w[..., QK_NOPE:QK_NOPE + half], w[..., QK_NOPE + half:]
    return jnp.concatenate([nope, x1, x2, x2, x1], axis=-1).reshape(kq, MLA_HEADS * HEAD_SLAB)


def _ukv_layout(w_ukv):
    kk = w_ukv.shape[0]
    w = w_ukv.reshape(kk, MLA_HEADS, QK_NOPE + V_HEAD)
    return jnp.concatenate([w[..., :QK_NOPE].reshape(kk, -1), w[..., QK_NOPE:].reshape(kk, -1)], axis=-1)


def _rope_tables(tp):
    half = QK_ROPE // 2
    pos = jnp.arange(tp, dtype=F32) - PADL
    inv_freq = 1.0 / (ROPE_THETA ** (jnp.arange(0, QK_ROPE, 2, dtype=F32) / QK_ROPE))
    ang = pos[:, None] * inv_freq[None, :]
    cos, sin = jnp.cos(ang), jnp.sin(ang)
    zero = jnp.zeros((tp, LANES - 2 * half), F32)
    return jnp.concatenate([cos, cos, zero], axis=1), jnp.concatenate([-sin, sin, zero], axis=1)


def kernel(x, meta, ln_in_g, ln_in_b, w_in, b_in, q_norm_g, kv_norm_g, w_uq, w_ukv, conv_w, conv_b, ml_norm_g, w_br_mla, w_br_mlstm, w_out, ln1_g, ln1_b, w_router, e_bias, w1, w3, w2, ws1, ws3, ws2, ln2_g, ln2_b):
    batch, seq, d = x.shape
    depth = w_in.shape[0]
    ql, kvl = q_norm_g.shape[1], kv_norm_g.shape[1]
    assert depth == DEPTH and ql == kvl and seq % LANES == 0
    tp = LANES + seq
    hq = ML_HEADS * ML_QK

    head = jnp.concatenate([jnp.zeros((PADL, d), x.dtype), meta.astype(x.dtype)], axis=0)
    h_f32, h_bf = _layer_norm_in(x, head, ln_in_g, ln_in_b)
    cos_t, sin_t = _rope_tables(tp)
    col_scale = jnp.concatenate([jnp.ones((1, hq), F32), jnp.full((1, hq), ML_QK ** -0.5, F32)], axis=1)

    for l in range(depth):
        w_main, b_main, w_small, b_small, offs = _in_proj_layout(w_in[l], b_in[l], d, ql, kvl)
        o_mq, o_mv, o_mo, o_ga, o_gb = offs[2], offs[4], offs[5], offs[6], offs[7]
        z = _matmul_bias(h_bf, w_main.astype(BF16), b_main, BF16, sig_col=o_mo, name="in_proj")
        zs = _matmul_bias(h_bf, w_small.astype(BF16), b_small, F32, name="in_proj_small")

        q, k, v = _mla_qkv(z, zs, q_norm_g[l], kv_norm_g[l], _uq_layout(w_uq[l]).astype(BF16),
                           _ukv_layout(w_ukv[l]).astype(BF16), cos_t, sin_t, tp, ql)
        attn = _attention(q, k, v, batch, tp)

        assert o_mq % hq == 0 and o_mv % ML_V == 0 and o_mo % ML_V == 0
        qk = _qk_conv(z, conv_w[l], conv_b[l], col_scale, tp, o_mq // hq)
        hm = _mlstm(qk, z, zs, ml_norm_g[l], batch, tp, o_mv // ML_V, o_mo // ML_V)

        y = _merge(attn, hm, w_br_mla[l].astype(BF16), w_br_mlstm[l].astype(BF16), z, o_ga, o_gb)
        h_f32, hs = _outproj_ln(y, w_out[l].astype(BF16), h_f32, ln1_g[l], ln1_b[l])
        outs = _moe(h_f32, hs, w_router[l], e_bias[l], w1, w3, w2, l, ws1[l], ws3[l], ws2[l],
                    ln2_g[l], ln2_b[l], final_shape=(batch, seq) if l == depth - 1 else None)
        if l == depth - 1:
            return outs[0]
        h_f32, h_bf = outs
```
